```python
import jax, jax.numpy as jnp
from jax import lax
import numpy as np

D_MODEL = 2048
BATCH = 2
SEQ = 8192
DEPTH = 1

MOBA_HEADS = 8
MOBA_HEAD_DIM = 128
MOBA_WIDTH = MOBA_HEADS * MOBA_HEAD_DIM
MOBA_BLOCK = 256
MOBA_TOPK = 3
MOBA_Q_CHUNK = 32

MLA_HEADS = 8
MLA_NOPE_DIM = 128
MLA_ROPE_DIM = 64
MLA_V_DIM = 128
MLA_Q_RANK = 512
MLA_KV_RANK = 512
MLA_WIDTH = MLA_HEADS * MLA_V_DIM
MLA_Q_CHUNK = 128
ROPE_THETA = 10000.0

MIX_WIDTH = MOBA_WIDTH + MLA_WIDTH
IN_COLS = 3 * MOBA_WIDTH + MLA_Q_RANK + MLA_KV_RANK + MLA_ROPE_DIM

N_EXPERTS = 32
TOP_K = 4
D_FF = D_MODEL
SWIGLU_LIMIT = 7.0
SWIGLU_ALPHA = 1.702
EXPERT_BLOCK = 256

NORM_EPS = 1e-5

kernel_name = "hybrid_moba_mla_moe_block"


def rms_norm(x, g):
    xf = x.astype(jnp.float32)
    y = xf * lax.rsqrt(jnp.mean(xf * xf, axis=-1, keepdims=True) + NORM_EPS)
    return (y * g.astype(jnp.float32)).astype(x.dtype)


def alibi_slopes(n_heads):
    return jnp.asarray(np.array([2.0 ** (-8.0 * (h + 1) / n_heads) for h in range(n_heads)], np.float32))


def rope_cos_sin(seq_len):
    inv = 1.0 / (ROPE_THETA ** (jnp.arange(0, MLA_ROPE_DIM, 2, dtype=jnp.float32) / MLA_ROPE_DIM))
    ang = jnp.arange(seq_len, dtype=jnp.float32)[:, None] * inv[None, :]
    ang = jnp.concatenate([ang, ang], axis=-1)
    return jnp.cos(ang), jnp.sin(ang)


def apply_rope(x, cos, sin):
    half = x.shape[-1] // 2
    rot = jnp.concatenate([-x[..., half:], x[..., :half]], axis=-1)
    return (x * cos + rot * sin).astype(x.dtype)


def moba_attention(q, k, v):
    B, H, S, Dh = q.shape
    n_blk = -(-S // MOBA_BLOCK)
    s_pad = n_blk * MOBA_BLOCK
    pad = ((0, 0), (0, 0), (0, s_pad - S), (0, 0))
    kp = jnp.pad(k, pad)
    vp = jnp.pad(v, pad)
    kb = kp.reshape(B, H, n_blk, MOBA_BLOCK, Dh)
    vb = vp.reshape(B, H, n_blk, MOBA_BLOCK, Dh)
    k_mean = jnp.mean(kb.astype(jnp.float32), axis=3).astype(k.dtype)
    k_sel = min(MOBA_TOPK, n_blk)
    slopes = alibi_slopes(H)
    scale = Dh ** -0.5
    bi = jnp.arange(B)[:, None, None, None]
    hi = jnp.arange(H)[None, :, None, None]
    key_off = jnp.arange(MOBA_BLOCK)
    blk_ids = jnp.arange(n_blk)

    def chunk(c):
        t0 = c * MOBA_Q_CHUNK
        q_c = lax.dynamic_slice_in_dim(q, t0, MOBA_Q_CHUNK, axis=2)
        t = t0 + jnp.arange(MOBA_Q_CHUNK)
        cur = t0 // MOBA_BLOCK
        gate = jnp.einsum('bhqd,bhnd->bhqn', q_c, k_mean).astype(jnp.float32)
        gate = jnp.where(blk_ids < cur, gate, -jnp.inf)
        _, sel = lax.top_k(gate, k_sel)
        sel_valid = sel < cur
        k_g = kb[bi, hi, sel]
        v_g = vb[bi, hi, sel]
        s_sel = jnp.einsum('bhqd,bhqjkd->bhqjk', q_c, k_g).astype(jnp.float32) * scale
        pos_sel = sel[..., None] * MOBA_BLOCK + key_off
        dist_sel = t[None, None, :, None, None] - pos_sel
        s_sel = s_sel - slopes[None, :, None, None, None] * dist_sel
        s_sel = jnp.where(sel_valid[..., None], s_sel, -jnp.inf)
        k_own = lax.dynamic_slice_in_dim(kp, cur * MOBA_BLOCK, MOBA_BLOCK, axis=2)
        v_own = lax.dynamic_slice_in_dim(vp, cur * MOBA_BLOCK, MOBA_BLOCK, axis=2)
        s_own = jnp.einsum('bhqd,bhkd->bhqk', q_c, k_own).astype(jnp.float32) * scale
        dist_own = t[:, None] - (cur * MOBA_BLOCK + key_off)[None, :]
        s_own = s_own - slopes[None, :, None, None] * dist_own
        s_own = jnp.where(dist_own >= 0, s_own, -jnp.inf)
        n_sel_keys = k_sel * MOBA_BLOCK
        scores = jnp.concatenate(
            [s_sel.reshape(B, H, MOBA_Q_CHUNK, n_sel_keys), s_own], axis=-1)
        p = jax.nn.softmax(scores, axis=-1).astype(v.dtype)
        p_sel = p[..., :n_sel_keys].reshape(B, H, MOBA_Q_CHUNK, k_sel, MOBA_BLOCK)
        p_own = p[..., n_sel_keys:]
        return (jnp.einsum('bhqjk,bhqjkd->bhqd', p_sel, v_g)
                + jnp.einsum('bhqk,bhkd->bhqd', p_own, v_own))

    out = lax.map(chunk, jnp.arange(S // MOBA_Q_CHUNK))
    return out.transpose(1, 2, 0, 3, 4).reshape(B, H, S, Dh)


def mla_attention(q_nope, q_rope, k_nope, k_rope, v):
    B, H, S, _ = q_nope.shape
    scale = (MLA_NOPE_DIM + MLA_ROPE_DIM) ** -0.5
    key_pos = jnp.arange(S)

    def chunk(c):
        t0 = c * MLA_Q_CHUNK
        qn = lax.dynamic_slice_in_dim(q_nope, t0, MLA_Q_CHUNK, axis=2)
        qr = lax.dynamic_slice_in_dim(q_rope, t0, MLA_Q_CHUNK, axis=2)
        s = (jnp.einsum('bhqd,bhkd->bhqk', qn, k_nope)
             + jnp.einsum('bhqd,bkd->bhqk', qr, k_rope)).astype(jnp.float32) * scale
        t = t0 + jnp.arange(MLA_Q_CHUNK)
        s = jnp.where(key_pos[None, :] <= t[:, None], s, -jnp.inf)
        p = jax.nn.softmax(s, axis=-1).astype(v.dtype)
        return jnp.einsum('bhqk,bhkd->bhqd', p, v)

    out = lax.map(chunk, jnp.arange(S // MLA_Q_CHUNK))
    return out.transpose(1, 2, 0, 3, 4).reshape(B, H, S, MLA_V_DIM)


def token_mixer(h, w_in, moba_out_g, q_a_norm_g, kv_a_norm_g, w_uq, w_ukv, mla_out_g, w_o):
    B, S, _ = h.shape
    proj = h @ w_in
    cuts = [MOBA_WIDTH, 2 * MOBA_WIDTH, 3 * MOBA_WIDTH,
            3 * MOBA_WIDTH + MLA_Q_RANK, 3 * MOBA_WIDTH + MLA_Q_RANK + MLA_KV_RANK]
    q_m, k_m, v_m, c_q, c_kv, k_r = jnp.split(proj, cuts, axis=-1)

    def heads(t, n):
        return t.reshape(B, S, n, -1).transpose(0, 2, 1, 3)

    o_moba = moba_attention(heads(q_m, MOBA_HEADS), heads(k_m, MOBA_HEADS), heads(v_m, MOBA_HEADS))
    o_moba = o_moba.transpose(0, 2, 1, 3).reshape(B, S, MOBA_WIDTH)

    cos, sin = rope_cos_sin(S)
    q_full = (rms_norm(c_q, q_a_norm_g) @ w_uq).reshape(B, S, MLA_HEADS, MLA_NOPE_DIM + MLA_ROPE_DIM)
    q_nope = q_full[..., :MLA_NOPE_DIM]
    q_rope = apply_rope(q_full[..., MLA_NOPE_DIM:], cos[:, None, :], sin[:, None, :])
    kv = (rms_norm(c_kv, kv_a_norm_g) @ w_ukv).reshape(B, S, MLA_HEADS, MLA_NOPE_DIM + MLA_V_DIM)
    k_nope = kv[..., :MLA_NOPE_DIM]
    v_mla = kv[..., MLA_NOPE_DIM:]
    k_rope = apply_rope(k_r, cos, sin)
    o_mla = mla_attention(q_nope.transpose(0, 2, 1, 3), q_rope.transpose(0, 2, 1, 3),
                          k_nope.transpose(0, 2, 1, 3), k_rope, v_mla.transpose(0, 2, 1, 3))
    o_mla = o_mla.transpose(0, 2, 1, 3).reshape(B, S, MLA_WIDTH)

    o = jnp.concatenate([rms_norm(o_moba, moba_out_g), rms_norm(o_mla, mla_out_g)], axis=-1)
    return o @ w_o


def moe_ffn(h, w_router, b_router, w1, b1, w2, b2):
    B, S, D = h.shape
    T = B * S
    xt = h.reshape(T, D)
    logits = (xt @ w_router + b_router).astype(jnp.float32)
    top_val, top_idx = lax.top_k(logits, TOP_K)
    gates = jax.nn.softmax(top_val, axis=-1).astype(h.dtype)

    A = T * TOP_K
    n_blocks = -(-A // EXPERT_BLOCK) + N_EXPERTS
    e_flat = top_idx.reshape(-1).astype(jnp.int32)
    g_flat = gates.reshape(-1)
    tok_flat = jnp.arange(A, dtype=jnp.int32) // TOP_K
    order = jnp.argsort(e_flat)
    e_sorted = e_flat[order]
    tok_sorted = tok_flat[order]
    g_sorted = g_flat[order]
    counts = jnp.zeros((N_EXPERTS,), jnp.int32).at[e_flat].add(1)
    start = jnp.cumsum(counts) - counts
    padded = ((counts + EXPERT_BLOCK - 1) // EXPERT_BLOCK) * EXPERT_BLOCK
    pad_end = jnp.cumsum(padded)
    pad_start = pad_end - padded
    dest = pad_start[e_sorted] + (jnp.arange(A, dtype=jnp.int32) - start[e_sorted])
    n_rows = n_blocks * EXPERT_BLOCK
    src = jnp.full((n_rows,), T, jnp.int32).at[dest].set(tok_sorted)
    block_expert = jnp.clip(
        jnp.searchsorted(pad_end, jnp.arange(n_blocks, dtype=jnp.int32) * EXPERT_BLOCK, side='right'),
        0, N_EXPERTS - 1).astype(jnp.int32)
    x_pad = jnp.concatenate([xt, jnp.zeros((1, D), xt.dtype)], axis=0)
    x_buf = x_pad[src].reshape(n_blocks, EXPERT_BLOCK, D)

    def expert_block(args):
        xb, e = args
        hb = xb @ w1[e] + b1[e]
        glu = jnp.minimum(hb[:, :D_FF], SWIGLU_LIMIT)
        lin = jnp.clip(hb[:, D_FF:], -SWIGLU_LIMIT, SWIGLU_LIMIT)
        act = glu * jax.nn.sigmoid(SWIGLU_ALPHA * glu) * (lin + 1.0)
        return act @ w2[e] + b2[e]

    y_buf = lax.map(expert_block, (x_buf, block_expert)).reshape(n_rows, D)
    y_assign = y_buf[dest] * g_sorted[:, None]
    out = jax.ops.segment_sum(y_assign, tok_sorted, num_segments=T)
    return out.reshape(B, S, D).astype(h.dtype)


def setup_inputs(seed: int = 0) -> dict:
    key = jax.random.key(seed)
    ks = jax.random.split(key, 20)
    f32 = jnp.float32

    def nrm(k, shape, fan_in):
        return jax.random.normal(k, shape, f32) * (fan_in ** -0.5)

    def gain(k, shape):
        return 1.0 + 0.05 * jax.random.normal(k, shape, f32)

    L = DEPTH
    return {
        "x": jax.random.normal(ks[0], (BATCH, SEQ, D_MODEL), f32),
        "norm_attn_g": gain(ks[1], (L, D_MODEL)),
        "w_in": nrm(ks[2], (L, D_MODEL, IN_COLS), D_MODEL),
        "moba_out_g": gain(ks[3], (L, MOBA_WIDTH)),
        "q_a_norm_g": gain(ks[4], (L, MLA_Q_RANK)),
        "kv_a_norm_g": gain(ks[5], (L, MLA_KV_RANK)),
        "w_uq": nrm(ks[6], (L, MLA_Q_RANK, MLA_HEADS * (MLA_NOPE_DIM + MLA_ROPE_DIM)), MLA_Q_RANK),
        "w_ukv": nrm(ks[7], (L, MLA_KV_RANK, MLA_HEADS * (MLA_NOPE_DIM + MLA_V_DIM)), MLA_KV_RANK),
        "mla_out_g": gain(ks[8], (L, MLA_WIDTH)),
        "w_o": nrm(ks[9], (L, MIX_WIDTH, D_MODEL), MIX_WIDTH),
        "norm_ffn_g": gain(ks[10], (L, D_MODEL)),
        "w_router": nrm(ks[11], (L, D_MODEL, N_EXPERTS), D_MODEL),
        "b_router": 0.01 * jax.random.normal(ks[12], (L, N_EXPERTS), f32),
        "w1": nrm(ks[13], (L, N_EXPERTS, D_MODEL, 2 * D_FF), D_MODEL),
        "b1": 0.01 * jax.random.normal(ks[14], (L, N_EXPERTS, 2 * D_FF), f32),
        "w2": nrm(ks[15], (L, N_EXPERTS, D_FF, D_MODEL), D_FF),
        "b2": 0.01 * jax.random.normal(ks[16], (L, N_EXPERTS, D_MODEL), f32),
        "norm_final_g": gain(ks[17], (D_MODEL,)),
    }


def reference(x, norm_attn_g, w_in, moba_out_g, q_a_norm_g, kv_a_norm_g, w_uq, w_ukv, mla_out_g,
              w_o, norm_ffn_g, w_router, b_router, w1, b1, w2, b2, norm_final_g):
    for l in range(DEPTH):
        h = rms_norm(x, norm_attn_g[l])
        x = x + token_mixer(h, w_in[l], moba_out_g[l], q_a_norm_g[l], kv_a_norm_g[l],
                            w_uq[l], w_ukv[l], mla_out_g[l], w_o[l])
        h = rms_norm(x, norm_ffn_g[l])
        x = x + moe_ffn(h, w_router[l], b_router[l], w1[l], b1[l], w2[l], b2[l])
    return rms_norm(x, norm_final_g)
```

```python
import functools

import jax
import jax.numpy as jnp
from jax import lax
from jax.experimental import pallas as pl
from jax.experimental.pallas import tpu as pltpu

MOBA_HEADS = 8
MOBA_HEAD_DIM = 128
MOBA_BLOCK = 256
MOBA_TOPK = 3
MLA_HEADS = 8
MLA_NOPE_DIM = 128
MLA_ROPE_DIM = 64
MLA_V_DIM = 128
MLA_Q_RANK = 512
MLA_KV_RANK = 512
MLA_QK_PAD = 256
ROPE_THETA = 10000.0
N_EXPERTS = 32
TOP_K = 4
SWIGLU_LIMIT = 7.0
SWIGLU_ALPHA = 1.702
EXPERT_BLOCK = 256
NORM_EPS = 1e-5

LOG2E = 1.4426950408889634
MASKED = -1e30
LANES = 128
V7X_VMEM_BUDGET = 56 * 1024 * 1024

F32 = jnp.float32
BF16 = jnp.bfloat16
U32 = jnp.uint32
I32 = jnp.int32
NT_DIMS = (((1,), (1,)), ((), ()))


def _params(semantics, vmem_bytes):
    return pltpu.CompilerParams(dimension_semantics=semantics,
                                vmem_limit_bytes=min(int(vmem_bytes), V7X_VMEM_BUDGET))


def _rms(xf, g):
    ms = jnp.mean(xf * xf, axis=-1, keepdims=True)
    return xf * lax.rsqrt(ms + NORM_EPS) * g


def _dot(a, b):
    return jnp.dot(a, b, preferred_element_type=F32)


def _dot_nt(a, b):
    return lax.dot_general(a, b, NT_DIMS, preferred_element_type=F32)


def _pack_halves(a, b):
    ai = lax.bitcast_convert_type(a.astype(BF16).astype(F32), U32)
    bi = lax.bitcast_convert_type(b.astype(BF16).astype(F32), U32)
    return ai | (bi >> 16)


def _unpack_halves(w):
    hi = lax.bitcast_convert_type(w & jnp.uint32(0xFFFF0000), F32)
    lo = lax.bitcast_convert_type(w << 16, F32)
    return hi, lo


def _inproj_kernel(x_ref, g_ref, w_ref, cos_ref, sin_ref,
                   qm_ref, km_ref, vm_ref, cq_ref, ckv_ref, kr_ref, *, q_scale):
    h = _rms(x_ref[...], g_ref[...]).astype(BF16)
    w = MOBA_HEADS * MOBA_HEAD_DIM

    def mm(lo, hi):
        return _dot(h, w_ref[:, lo:hi])

    qm_ref[...] = (mm(0, w) * q_scale).astype(BF16)
    km_ref[...] = mm(w, 2 * w).astype(BF16)
    vm_ref[...] = mm(2 * w, 3 * w).astype(BF16)
    c0 = 3 * w
    cq_ref[...] = mm(c0, c0 + MLA_Q_RANK)
    c1 = c0 + MLA_Q_RANK
    ckv_ref[...] = mm(c1, c1 + MLA_KV_RANK)
    c2 = c1 + MLA_KV_RANK
    kr = mm(c2, c2 + MLA_QK_PAD) * cos_ref[...] + mm(c2 + MLA_QK_PAD, c2 + 2 * MLA_QK_PAD) * sin_ref[...]
    kr_ref[...] = kr.astype(BF16)


def _in_proj(x2, g, w_all, cos_t, sin_t, seq, tm):
    t, d = x2.shape
    w = MOBA_HEADS * MOBA_HEAD_DIM
    nc = w_all.shape[1]
    n_pos = seq // tm
    row = lambda i: (i, 0)
    const = lambda i: (0, 0)
    pos = lambda i: (i % n_pos, 0)
    out_shape = [jax.ShapeDtypeStruct((t, w), BF16)] * 3 + [
        jax.ShapeDtypeStruct((t, MLA_Q_RANK), F32), jax.ShapeDtypeStruct((t, MLA_KV_RANK), F32),
        jax.ShapeDtypeStruct((t, MLA_QK_PAD), BF16)]
    out_specs = [pl.BlockSpec((tm, w), row)] * 3 + [
        pl.BlockSpec((tm, MLA_Q_RANK), row), pl.BlockSpec((tm, MLA_KV_RANK), row),
        pl.BlockSpec((tm, MLA_QK_PAD), row)]
    vmem = (2 * tm * d * 4 + d * nc * 2 + 2 * tm * (3 * w * 2 + 2 * 512 * 4 + 256 * 2)
            + 4 * tm * MLA_QK_PAD * 4 + tm * d * 2 + 3 * tm * w * 4 + (4 << 20))
    return pl.pallas_call(
        functools.partial(_inproj_kernel, q_scale=MOBA_HEAD_DIM ** -0.5 * LOG2E),
        grid=(t // tm,),
        in_specs=[pl.BlockSpec((tm, d), row), pl.BlockSpec((1, d), const),
                  pl.BlockSpec((d, nc), const, pipeline_mode=pl.Buffered(1)),
                  pl.BlockSpec((tm, MLA_QK_PAD), pos), pl.BlockSpec((tm, MLA_QK_PAD), pos)],
        out_specs=out_specs, out_shape=out_shape,
        compiler_params=_params(("parallel",), vmem), name="in_proj",
    )(x2, g, w_all, cos_t, sin_t)


def _mlaup_kernel(cq_ref, ckv_ref, gq_ref, gkv_ref, wqa_ref, wqb_ref, wk_ref, wv_ref,
                  kr_ref, cos_ref, sin_ref, q_ref, k_ref, v_ref, *, q_scale):
    nq = _rms(cq_ref[...], gq_ref[...]).astype(BF16)
    nkv = _rms(ckv_ref[...], gkv_ref[...]).astype(BF16)
    cos = cos_ref[...]
    sin = sin_ref[...]
    kr = kr_ref[...].astype(F32)
    for hh in range(MLA_HEADS):
        cols = slice(hh * MLA_QK_PAD, (hh + 1) * MLA_QK_PAD)
        q = _dot(nq, wqa_ref[:, cols]) * cos + _dot(nq, wqb_ref[:, cols]) * sin
        q_ref[:, cols] = (q * q_scale).astype(BF16)
        k_ref[:, cols] = (_dot(nkv, wk_ref[:, cols]) + kr).astype(BF16)
    v_ref[...] = _dot(nkv, wv_ref[...]).astype(BF16)


def _mla_up(cq, ckv, gq, gkv, wqa, wqb, wk, wv, kr, cos_t, sin_t, seq, tm):
    t = cq.shape[0]
    n_pos = seq // tm
    row = lambda i: (i, 0)
    const = lambda i: (0, 0)
    pos = lambda i: (i % n_pos, 0)
    hq = MLA_HEADS * MLA_QK_PAD
    hv = MLA_HEADS * MLA_V_DIM
    scale = (MLA_NOPE_DIM + MLA_ROPE_DIM) ** -0.5 * LOG2E
    vmem = (4 * tm * 512 * 4 + 2 * (3 * 512 * hq * 2 + 512 * hv * 2) + 6 * tm * MLA_QK_PAD * 4
            + 2 * tm * (2 * hq + hv) * 2 + (8 << 20))
    return pl.pallas_call(
        functools.partial(_mlaup_kernel, q_scale=scale),
        grid=(t // tm,),
        in_specs=[pl.BlockSpec((tm, MLA_Q_RANK), row), pl.BlockSpec((tm, MLA_KV_RANK), row),
                  pl.BlockSpec((1, MLA_Q_RANK), const), pl.BlockSpec((1, MLA_KV_RANK), const),
                  pl.BlockSpec((MLA_Q_RANK, hq), const), pl.BlockSpec((MLA_Q_RANK, hq), const),
                  pl.BlockSpec((MLA_KV_RANK, hq), const), pl.BlockSpec((MLA_KV_RANK, hv), const),
                  pl.BlockSpec((tm, MLA_QK_PAD), row),
                  pl.BlockSpec((tm, MLA_QK_PAD), pos), pl.BlockSpec((tm, MLA_QK_PAD), pos)],
        out_specs=[pl.BlockSpec((tm, hq), row), pl.BlockSpec((tm, hq), row), pl.BlockSpec((tm, hv), row)],
        out_shape=[jax.ShapeDtypeStruct((t, hq), BF16), jax.ShapeDtypeStruct((t, hq), BF16),
                   jax.ShapeDtypeStruct((t, hv), BF16)],
        compiler_params=_params(("parallel",), vmem), name="mla_up",
    )(cq, ckv, gq, gkv, wqa, wqb, wk, wv, kr, cos_t, sin_t)


def _softmax_first(s, v, m_sc, l_sc, acc_sc):
    m = jnp.max(s, axis=-1, keepdims=True)
    p = jnp.exp2(s - m)
    m_sc[...] = m
    l_sc[...] = jnp.sum(p, axis=-1, keepdims=True)
    acc_sc[...] = _dot(p.astype(BF16), v)


def _softmax_next(s, v, m_sc, l_sc, acc_sc):
    m_old = m_sc[...]
    m_new = jnp.maximum(m_old, jnp.max(s, axis=-1, keepdims=True))
    alpha = jnp.exp2(m_old - m_new)
    p = jnp.exp2(s - m_new)
    l_sc[...] = alpha * l_sc[...] + jnp.sum(p, axis=-1, keepdims=True)
    acc_sc[...] = alpha * acc_sc[...] + _dot(p.astype(BF16), v)
    m_sc[...] = m_new


def _mla_attn_kernel(q_ref, k_ref, v_ref, o_ref, m_sc, l_sc, acc_sc, *, tq):
    i = pl.program_id(2)
    q = q_ref[...]
    d0 = pl.multiple_of(i * tq, tq)
    row = lax.broadcasted_iota(I32, (tq, tq), 0)
    col = lax.broadcasted_iota(I32, (tq, tq), 1)
    s = jnp.where(row >= col, _dot_nt(q, k_ref[pl.ds(d0, tq), :]), MASKED)
    _softmax_first(s, v_ref[pl.ds(d0, tq), :], m_sc, l_sc, acc_sc)

    def body(j, carry):
        st = pl.multiple_of(j * tq, tq)
        _softmax_next(_dot_nt(q, k_ref[pl.ds(st, tq), :]), v_ref[pl.ds(st, tq), :], m_sc, l_sc, acc_sc)
        return carry

    lax.fori_loop(0, i, body, 0)
    o_ref[...] = acc_sc[...] / l_sc[...]


def _mla_attn(qc, kc, v, batch, seq, tq):
    t = qc.shape[0]
    nq = seq // tq
    vmem = 2 * seq * (MLA_QK_PAD + MLA_V_DIM) * 2 + 8 * tq * tq * 4 + 8 * tq * LANES * 4 + (4 << 20)
    return pl.pallas_call(
        functools.partial(_mla_attn_kernel, tq=tq),
        grid=(batch, MLA_HEADS, nq),
        in_specs=[pl.BlockSpec((tq, MLA_QK_PAD), lambda b, h, i: (b * nq + i, h)),
                  pl.BlockSpec((seq, MLA_QK_PAD), lambda b, h, i: (b, h)),
                  pl.BlockSpec((seq, MLA_V_DIM), lambda b, h, i: (b, h))],
        out_specs=pl.BlockSpec((tq, MLA_V_DIM), lambda b, h, i: (b * nq + i, h)),
        out_shape=jax.ShapeDtypeStruct((t, MLA_HEADS * MLA_V_DIM), F32),
        scratch_shapes=[pltpu.VMEM((tq, 1), F32), pltpu.VMEM((tq, 1), F32), pltpu.VMEM((tq, MLA_V_DIM), F32)],
        compiler_params=_params(("parallel", "parallel", "arbitrary"), vmem), name="mla_attn",
    )(qc, kc, v)


def _moba_attn_kernel(q_ref, k_ref, v_ref, o_ref, kmean_sc, rel_sc, sel_sc, m_sc, l_sc, acc_sc, *, n_blk):
    blk = MOBA_BLOCK
    h = pl.program_id(1)
    c = pl.program_id(2)
    slope2 = jnp.exp2(-(jnp.zeros((1, 1), F32) + (h + 1).astype(F32))) * LOG2E
    row = lax.broadcasted_iota(I32, (blk, blk), 0)
    col = lax.broadcasted_iota(I32, (blk, blk), 1)

    @pl.when(c == 0)
    def _():
        kmean_sc[...] = jnp.zeros_like(kmean_sc)
        for n in range(n_blk):
            kmean_sc[n:n + 1, :] = jnp.mean(k_ref[n * blk:(n + 1) * blk, :].astype(F32), axis=0, keepdims=True)
        rel_sc[...] = (row - col).astype(F32) * slope2

    q = q_ref[...]
    km = kmean_sc[...]
    km_hi = km.astype(BF16)
    km_lo = (km - km_hi.astype(F32)).astype(BF16)
    gate = _dot_nt(q, km_hi) + _dot_nt(q, km_lo)
    lane = lax.broadcasted_iota(I32, (blk, LANES), 1)
    g = jnp.where(lane < c, gate, -jnp.inf)
    sel = jnp.zeros((blk, LANES), F32)
    for _ in range(MOBA_TOPK):
        mx = jnp.max(g, axis=-1, keepdims=True)
        cand = jnp.where((g == mx) & (mx > -jnp.inf), lane, LANES)
        first = jnp.min(cand, axis=-1, keepdims=True)
        pick = lane == first
        sel = jnp.where(pick, 1.0, sel)
        g = jnp.where(pick, -jnp.inf, g)
    sel_sc[...] = sel

    d0 = pl.multiple_of(c * blk, blk)
    s = _dot_nt(q, k_ref[pl.ds(d0, blk), :]) - rel_sc[...]
    s = jnp.where(row >= col, s, MASKED)
    _softmax_first(s, v_ref[pl.ds(d0, blk), :], m_sc, l_sc, acc_sc)

    def body(n, carry):
        st = pl.multiple_of(n * blk, blk)
        chosen = jnp.sum(jnp.where(lane == n, sel_sc[...], 0.0), axis=-1, keepdims=True)
        far = slope2 * ((c - n) * blk).astype(F32)
        s = _dot_nt(q, k_ref[pl.ds(st, blk), :]) - rel_sc[...] - far
        s = jnp.where(chosen > 0.5, s, MASKED)
        _softmax_next(s, v_ref[pl.ds(st, blk), :], m_sc, l_sc, acc_sc)
        return carry

    lax.fori_loop(0, c, body, 0)
    o_ref[...] = acc_sc[...] / l_sc[...]


def _moba_attn(qm, km, vm, batch, seq):
    t = qm.shape[0]
    blk = MOBA_BLOCK
    n_blk = seq // blk
    dh = MOBA_HEAD_DIM
    vmem = 2 * 2 * seq * dh * 2 + 12 * blk * blk * 4 + 8 * blk * LANES * 4 + (4 << 20)
    return pl.pallas_call(
        functools.partial(_moba_attn_kernel, n_blk=n_blk),
        grid=(batch, MOBA_HEADS, n_blk),
        in_specs=[pl.BlockSpec((blk, dh), lambda b, h, c: (b * n_blk + c, h)),
                  pl.BlockSpec((seq, dh), lambda b, h, c: (b, h)),
                  pl.BlockSpec((seq, dh), lambda b, h, c: (b, h))],
        out_specs=pl.BlockSpec((blk, dh), lambda b, h, c: (b * n_blk + c, h)),
        out_shape=jax.ShapeDtypeStruct((t, MOBA_HEADS * dh), F32),
        scratch_shapes=[pltpu.VMEM((LANES, dh), F32), pltpu.VMEM((blk, blk), F32), pltpu.VMEM((blk, LANES), F32),
                        pltpu.VMEM((blk, 1), F32), pltpu.VMEM((blk, 1), F32), pltpu.VMEM((blk, dh), F32)],
        compiler_params=_params(("parallel", "parallel", "arbitrary"), vmem), name="moba_attn",
    )(qm, km, vm)


def _outproj_kernel(om_ref, ol_ref, gm_ref, gl_ref, wo_ref, x_ref, gf_ref, wrh_ref, wrl_ref, br_ref,
                    x1_ref, hp_ref, lg_ref):
    wm = om_ref.shape[1]
    a = _rms(om_ref[...], gm_ref[...]).astype(BF16)
    b = _rms(ol_ref[...], gl_ref[...]).astype(BF16)
    x1 = x_ref[...] + _dot(a, wo_ref[:wm, :]) + _dot(b, wo_ref[wm:, :])
    x1_ref[...] = x1
    h2 = _rms(x1, gf_ref[...])
    half = h2.shape[1] // 2
    hp_ref[...] = _pack_halves(h2[:, :half], h2[:, half:])
    h_hi = h2.astype(BF16)
    h_lo = (h2 - h_hi.astype(F32)).astype(BF16)
    lg_ref[...] = (_dot(h_hi, wrh_ref[...]) + _dot(h_lo, wrh_ref[...]) + _dot(h_hi, wrl_ref[...])
                   + br_ref[...])


def _out_proj(om, ol, gm, gl, wo, x2, gf, wr_hi, wr_lo, br, tm):
    t, d = x2.shape
    wm = om.shape[1]
    wl = ol.shape[1]
    row = lambda i: (i, 0)
    const = lambda i: (0, 0)
    vmem = 2 * (wm + wl) * d * 2 + 2 * tm * (wm + wl + 2 * d) * 4 + 2 * tm * d * 2 + 6 * tm * d * 4 + (4 << 20)
    return pl.pallas_call(
        _outproj_kernel,
        grid=(t // tm,),
        in_specs=[pl.BlockSpec((tm, wm), row), pl.BlockSpec((tm, wl), row),
                  pl.BlockSpec((1, wm), const), pl.BlockSpec((1, wl), const),
                  pl.BlockSpec((wm + wl, d), const), pl.BlockSpec((tm, d), row), pl.BlockSpec((1, d), const),
                  pl.BlockSpec((d, LANES), const), pl.BlockSpec((d, LANES), const), pl.BlockSpec((1, LANES), const)],
        out_specs=[pl.BlockSpec((tm, d), row), pl.BlockSpec((tm, d // 2), row), pl.BlockSpec((tm, LANES), row)],
        out_shape=[jax.ShapeDtypeStruct((t, d), F32), jax.ShapeDtypeStruct((t, d // 2), U32),
                   jax.ShapeDtypeStruct((t, LANES), F32)],
        compiler_params=_params(("parallel",), vmem), name="out_proj",
    )(om, ol, gm, gl, wo, x2, gf, wr_hi, wr_lo, br)


def _router_kernel(lg_ref, idx_ref, gate_ref, pos_ref, cnt_ref, *, tm):
    i = pl.program_id(0)

    @pl.when(i == 0)
    def _():
        cnt_ref[...] = jnp.zeros_like(cnt_ref)

    lane = lax.broadcasted_iota(I32, (tm, LANES), 1)
    work = jnp.where(lane < N_EXPERTS, lg_ref[...], -jnp.inf)
    vals, firsts, picks = [], [], []
    for _ in range(TOP_K):
        mx = jnp.max(work, axis=-1, keepdims=True)
        first = jnp.min(jnp.where(work == mx, lane, LANES), axis=-1, keepdims=True)
        pick = lane == first
        work = jnp.where(pick, -jnp.inf, work)
        vals.append(mx)
        firsts.append(first)
        picks.append(pick)
    es = [jnp.exp(v - vals[0]) for v in vals]
    denom = es[0] + es[1] + es[2] + es[3]
    chosen = jnp.zeros((tm, LANES), F32)
    for p in picks:
        chosen = jnp.where(p, 1.0, chosen)
    r = lax.broadcasted_iota(I32, (tm, tm), 0)
    cc = lax.broadcasted_iota(I32, (tm, tm), 1)
    before = (r > cc).astype(BF16)
    prior = _dot(before, chosen.astype(BF16)) + cnt_ref[...]
    lane4 = lax.broadcasted_iota(I32, (tm, TOP_K), 1)
    idx = jnp.zeros((tm, TOP_K), I32)
    gate = jnp.zeros((tm, TOP_K), F32)
    pos = jnp.zeros((tm, TOP_K), I32)
    for j in range(TOP_K):
        pj = jnp.sum(jnp.where(picks[j], prior, 0.0), axis=-1, keepdims=True)
        idx = jnp.where(lane4 == j, firsts[j], idx)
        gate = jnp.where(lane4 == j, es[j] / denom, gate)
        pos = jnp.where(lane4 == j, pj.astype(I32), pos)
    idx_ref[...] = idx
    gate_ref[...] = gate
    pos_ref[...] = pos
    cnt_ref[...] += jnp.sum(chosen, axis=0, keepdims=True)


def _router(logits, tm):
    t = logits.shape[0]
    row = lambda i: (i, 0)
    vmem = 4 * tm * tm * 4 + 24 * tm * LANES * 4 + (4 << 20)
    return pl.pallas_call(
        functools.partial(_router_kernel, tm=tm),
        grid=(t // tm,),
        in_specs=[pl.BlockSpec((tm, LANES), row)],
        out_specs=[pl.BlockSpec((tm, TOP_K), row)] * 3 + [pl.BlockSpec((1, LANES), lambda i: (0, 0))],
        out_shape=[jax.ShapeDtypeStruct((t, TOP_K), I32), jax.ShapeDtypeStruct((t, TOP_K), F32),
                   jax.ShapeDtypeStruct((t, TOP_K), I32), jax.ShapeDtypeStruct((1, LANES), F32)],
        compiler_params=_params(("arbitrary",), vmem), name="router",
    )(logits)


DISPATCH_WINDOW = 32


def _dispatch_kernel(dest_ref, pad_end_ref, padded_ref, hp_ref, xb_ref, zero_sc, zsem, sem, *, n_tok):
    blk = EXPERT_BLOCK
    zero_sc[...] = jnp.zeros_like(zero_sc)

    def zero_copy(e):
        start = pl.multiple_of(pad_end_ref[e] - blk, blk)
        return pltpu.make_async_copy(zero_sc, xb_ref.at[pl.ds(start, blk)], zsem)

    def zstart(e, carry):
        @pl.when(padded_ref[e] > 0)
        def _():
            zero_copy(e).start()
        return carry

    def zwait(e, carry):
        @pl.when(padded_ref[e] > 0)
        def _():
            zero_copy(e).wait()
        return carry

    lax.fori_loop(0, N_EXPERTS, zstart, 0)
    lax.fori_loop(0, N_EXPERTS, zwait, 0)

    def row_copy(tok, slot):
        return pltpu.make_async_copy(hp_ref.at[pl.ds(tok, 1)], xb_ref.at[pl.ds(dest_ref[tok * TOP_K + slot], 1)], sem)

    def body(tok, carry):
        for j in range(TOP_K):
            row_copy(tok, j).start()

        @pl.when(tok >= DISPATCH_WINDOW)
        def _():
            for j in range(TOP_K):
                row_copy(tok - DISPATCH_WINDOW, j).wait()
        return carry

    lax.fori_loop(0, n_tok, body, 0)

    def drain(tok, carry):
        for j in range(TOP_K):
            row_copy(tok, j).wait()
        return carry

    lax.fori_loop(n_tok - DISPATCH_WINDOW, n_tok, drain, 0)


def _dispatch(dest, pad_end, padded, hp, n_rows):
    t, half = hp.shape
    grid_spec = pltpu.PrefetchScalarGridSpec(
        num_scalar_prefetch=3, grid=(1,),
        in_specs=[pl.BlockSpec(memory_space=pl.ANY)],
        out_specs=pl.BlockSpec(memory_space=pl.ANY),
        scratch_shapes=[pltpu.VMEM((EXPERT_BLOCK, half), U32), pltpu.SemaphoreType.DMA, pltpu.SemaphoreType.DMA])
    return pl.pallas_call(
        functools.partial(_dispatch_kernel, n_tok=t),
        grid_spec=grid_spec,
        out_shape=jax.ShapeDtypeStruct((n_rows, half), U32),
        compiler_params=pltpu.CompilerParams(dimension_semantics=("arbitrary",), has_side_effects=True),
        name="dispatch",
    )(dest, pad_end, padded, hp)


def _moe_up_kernel(be_ref, nu_ref, x_ref, w1_ref, b1_ref, act_ref, *, fc):
    i = pl.program_id(0)

    @pl.when(i < nu_ref[0])
    def _():
        hi, lo = _unpack_halves(x_ref[...])
        a = hi.astype(BF16)
        b = lo.astype(BF16)
        half = a.shape[1]
        d_ff = act_ref.shape[1]

        def pre(c0):
            return (_dot(a, w1_ref[0, :half, c0:c0 + fc]) + _dot(b, w1_ref[0, half:, c0:c0 + fc])
                    + b1_ref[0, :, c0:c0 + fc])

        for c in range(d_ff // fc):
            glu = jnp.minimum(pre(c * fc), SWIGLU_LIMIT)
            lin = jnp.clip(pre(d_ff + c * fc), -SWIGLU_LIMIT, SWIGLU_LIMIT)
            act = glu * jax.nn.sigmoid(SWIGLU_ALPHA * glu) * (lin + 1.0)
            act_ref[:, c * fc:(c + 1) * fc] = act.astype(BF16)


def _moe_up(block_expert, n_used, xb, w1, b1):
    n_rows, half = xb.shape
    e, d, f2 = w1.shape
    d_ff = f2 // 2
    blk = EXPERT_BLOCK
    n_blocks = n_rows // blk
    fc = 512
    live = lambda i, be, nu: jnp.minimum(i, nu[0] - 1)
    grid_spec = pltpu.PrefetchScalarGridSpec(
        num_scalar_prefetch=2, grid=(n_blocks,),
        in_specs=[pl.BlockSpec((blk, half), lambda i, be, nu: (live(i, be, nu), 0)),
                  pl.BlockSpec((1, d, f2), lambda i, be, nu: (be[live(i, be, nu)], 0, 0)),
                  pl.BlockSpec((1, 1, f2), lambda i, be, nu: (be[live(i, be, nu)], 0, 0))],
        out_specs=pl.BlockSpec((blk, d_ff), lambda i, be, nu: (live(i, be, nu), 0)))
    vmem = 2 * d * f2 * 2 + 2 * blk * half * 4 + 2 * blk * d_ff * 2 + 8 * blk * fc * 4 + 2 * blk * d * 2 + (4 << 20)
    return pl.pallas_call(
        functools.partial(_moe_up_kernel, fc=fc),
        grid_spec=grid_spec,
        out_shape=jax.ShapeDtypeStruct((n_rows, d_ff), BF16),
        compiler_params=_params(("arbitrary",), vmem), name="moe_up",
    )(block_expert, n_used, xb, w1, b1)


def _moe_down_kernel(be_ref, nu_ref, a_ref, w2_ref, b2_ref, y_ref):
    i = pl.program_id(0)

    @pl.when(i < nu_ref[0])
    def _():
        y = _dot(a_ref[...], w2_ref[0]) + b2_ref[0]
        half = y.shape[1] // 2
        y_ref[...] = _pack_halves(y[:, :half], y[:, half:])


def _moe_down(block_expert, n_used, act, w2, b2):
    n_rows, d_ff = act.shape
    e, _, d = w2.shape
    blk = EXPERT_BLOCK
    n_blocks = n_rows // blk
    live = lambda i, be, nu: jnp.minimum(i, nu[0] - 1)
    grid_spec = pltpu.PrefetchScalarGridSpec(
        num_scalar_prefetch=2, grid=(n_blocks,),
        in_specs=[pl.BlockSpec((blk, d_ff), lambda i, be, nu: (live(i, be, nu), 0)),
                  pl.BlockSpec((1, d_ff, d), lambda i, be, nu: (be[live(i, be, nu)], 0, 0)),
                  pl.BlockSpec((1, 1, d), lambda i, be, nu: (be[live(i, be, nu)], 0, 0))],
        out_specs=pl.BlockSpec((blk, d // 2), lambda i, be, nu: (live(i, be, nu), 0)))
    vmem = 2 * d_ff * d * 2 + 2 * blk * d_ff * 2 + 2 * blk * d * 2 + 4 * blk * d * 4 + (4 << 20)
    return pl.pallas_call(
        _moe_down_kernel,
        grid_spec=grid_spec,
        out_shape=jax.ShapeDtypeStruct((n_rows, d // 2), U32),
        compiler_params=_params(("arbitrary",), vmem), name="moe_down",
    )(block_expert, n_used, act, w2, b2)


def _combine_kernel(dest_ref, gate_ref, x1_ref, gfin_ref, yb_ref, o_ref, rows_sc, sem, *, tm):
    i = pl.program_id(0)
    base = i * tm

    def row_copy(r, j):
        return pltpu.make_async_copy(yb_ref.at[pl.ds(dest_ref[(base + r) * TOP_K + j], 1)],
                                     rows_sc.at[j, pl.ds(r, 1)], sem)

    def start(r, carry):
        for j in range(TOP_K):
            row_copy(r, j).start()
        return carry

    def wait(r, carry):
        for j in range(TOP_K):
            row_copy(r, j).wait()
        return carry

    lax.fori_loop(0, tm, start, 0)
    lax.fori_loop(0, tm, wait, 0)

    x1 = x1_ref[...]
    half = x1.shape[1] // 2
    gates = gate_ref[...]
    left = x1[:, :half]
    right = x1[:, half:]
    for j in range(TOP_K):
        hi, lo = _unpack_halves(rows_sc[j])
        gj = gates[:, j:j + 1]
        left = left + gj * hi
        right = right + gj * lo
    ms = (jnp.sum(left * left, axis=-1, keepdims=True) + jnp.sum(right * right, axis=-1, keepdims=True)) / (2 * half)
    inv = lax.rsqrt(ms + NORM_EPS)
    g = gfin_ref[...]
    o_ref[:, :half] = left * inv * g[:, :half]
    o_ref[:, half:] = right * inv * g[:, half:]


def _combine(dest, gates, x1, gfin, yb, tm):
    t, d = x1.shape
    half = d // 2
    grid_spec = pltpu.PrefetchScalarGridSpec(
        num_scalar_prefetch=1, grid=(t // tm,),
        in_specs=[pl.BlockSpec((tm, TOP_K), lambda i, dst: (i, 0)),
                  pl.BlockSpec((tm, d), lambda i, dst: (i, 0)),
                  pl.BlockSpec((1, d), lambda i, dst: (0, 0)),
                  pl.BlockSpec(memory_space=pl.ANY)],
        out_specs=pl.BlockSpec((tm, d), lambda i, dst: (i, 0)),
        scratch_shapes=[pltpu.VMEM((TOP_K, tm, half), U32), pltpu.SemaphoreType.DMA])
    vmem = TOP_K * tm * half * 4 + 4 * tm * d * 4 + 6 * tm * d * 4 + 2 * tm * LANES * 4 + (4 << 20)
    return pl.pallas_call(
        functools.partial(_combine_kernel, tm=tm),
        grid_spec=grid_spec,
        out_shape=jax.ShapeDtypeStruct((t, d), F32),
        compiler_params=_params(("arbitrary",), vmem), name="combine",
    )(dest, gates, x1, gfin, yb)


def _rope_tables(seq):
    inv = 1.0 / (ROPE_THETA ** (jnp.arange(0, MLA_ROPE_DIM, 2, dtype=F32) / MLA_ROPE_DIM))
    ang = jnp.arange(seq, dtype=F32)[:, None] * inv[None, :]
    ang = jnp.concatenate([ang, ang], axis=-1)
    ones = jnp.ones((seq, MLA_NOPE_DIM), F32)
    zeros = jnp.zeros((seq, MLA_QK_PAD - MLA_NOPE_DIM - MLA_ROPE_DIM), F32)
    cos_t = jnp.concatenate([ones, jnp.cos(ang), zeros], axis=-1)
    sin_t = jnp.concatenate([0.0 * ones, jnp.sin(ang), zeros], axis=-1)
    return cos_t, sin_t


def _rot_cols(w):
    half = w.shape[-1] // 2
    return jnp.concatenate([-w[..., half:], w[..., :half]], axis=-1)


def _rope_slot(w):
    k = w.shape[0]
    return jnp.concatenate([jnp.zeros((k, MLA_NOPE_DIM), w.dtype), w,
                            jnp.zeros((k, MLA_QK_PAD - MLA_NOPE_DIM - MLA_ROPE_DIM), w.dtype)], axis=-1)


def _layer(x2, batch, seq, norm_attn_g, w_in, moba_out_g, q_a_norm_g, kv_a_norm_g, w_uq, w_ukv, mla_out_g,
           w_o, norm_ffn_g, w_router, b_router, w1, b1, w2, b2, final_g):
    t, d = x2.shape
    wm = MOBA_HEADS * MOBA_HEAD_DIM
    cos_t, sin_t = _rope_tables(seq)

    c_kr = 3 * wm + MLA_Q_RANK + MLA_KV_RANK
    w_kr = w_in[:, c_kr:c_kr + MLA_ROPE_DIM]
    w_all = jnp.concatenate([w_in[:, :c_kr], _rope_slot(w_kr), _rope_slot(_rot_cols(w_kr))], axis=1).astype(BF16)
    wq = w_uq.reshape(MLA_Q_RANK, MLA_HEADS, MLA_NOPE_DIM + MLA_ROPE_DIM)
    zq = jnp.zeros((MLA_Q_RANK, MLA_HEADS, MLA_QK_PAD - MLA_NOPE_DIM - MLA_ROPE_DIM), F32)
    wqa = jnp.concatenate([wq, zq], axis=-1).reshape(MLA_Q_RANK, -1).astype(BF16)
    wqb = jnp.concatenate([jnp.zeros((MLA_Q_RANK, MLA_HEADS, MLA_NOPE_DIM), F32),
                           _rot_cols(wq[..., MLA_NOPE_DIM:]), zq], axis=-1).reshape(MLA_Q_RANK, -1).astype(BF16)
    wkv = w_ukv.reshape(MLA_KV_RANK, MLA_HEADS, MLA_NOPE_DIM + MLA_V_DIM)
    wk = jnp.concatenate([wkv[..., :MLA_NOPE_DIM],
                          jnp.zeros((MLA_KV_RANK, MLA_HEADS, MLA_QK_PAD - MLA_NOPE_DIM), F32)],
                         axis=-1).reshape(MLA_KV_RANK, -1).astype(BF16)
    wv = wkv[..., MLA_NOPE_DIM:].reshape(MLA_KV_RANK, -1).astype(BF16)
    wr = jnp.pad(w_router, ((0, 0), (0, LANES - N_EXPERTS)))
    wr_hi = wr.astype(BF16)
    wr_lo = (wr - wr_hi.astype(F32)).astype(BF16)
    br = jnp.pad(b_router, (0, LANES - N_EXPERTS))[None, :]

    qm, km, vm, cq, ckv, kr = _in_proj(x2, norm_attn_g[None, :], w_all, cos_t, sin_t, seq, tm=min(512, seq))
    qc, kc, vl = _mla_up(cq, ckv, q_a_norm_g[None, :], kv_a_norm_g[None, :], wqa, wqb, wk, wv, kr,
                         cos_t, sin_t, seq, tm=min(512, seq))
    o_moba = _moba_attn(qm, km, vm, batch, seq)
    o_mla = _mla_attn(qc, kc, vl, batch, seq, tq=min(512, seq))
    x1, hp, logits = _out_proj(o_moba, o_mla, moba_out_g[None, :], mla_out_g[None, :], w_o.astype(BF16), x2,
                               norm_ffn_g[None, :], wr_hi, wr_lo, br, tm=256)

    idx, gates, pos, counts = _router(logits, tm=min(512, t))
    blk = EXPERT_BLOCK
    n_blocks = -(-(t * TOP_K) // blk) + N_EXPERTS
    counts = counts[0, :N_EXPERTS].astype(I32)
    padded = ((counts + blk - 1) // blk) * blk
    pad_end = jnp.cumsum(padded)
    pad_start = pad_end - padded
    dest = (pad_start[idx] + pos).reshape(-1)
    block_expert = jnp.minimum(
        jnp.sum((jnp.arange(n_blocks, dtype=I32) * blk)[:, None] >= pad_end[None, :], axis=1), N_EXPERTS - 1).astype(I32)
    n_used = (pad_end[-1:] // blk).astype(I32)

    xb = _dispatch(dest, pad_end.astype(I32), padded.astype(I32), hp, n_blocks * blk)
    act = _moe_up(block_expert, n_used, xb, w1.astype(BF16), b1[:, None, :])
    yb = _moe_down(block_expert, n_used, act, w2.astype(BF16), b2[:, None, :])
    return _combine(dest, gates, x1, final_g[None, :], yb, tm=256)


def kernel(x, norm_attn_g, w_in, moba_out_g, q_a_norm_g, kv_a_norm_g, w_uq, w_ukv, mla_out_g, w_o, norm_ffn_g,
           w_router, b_router, w1, b1, w2, b2, norm_final_g):
    batch, seq, d = x.shape
    depth = w_in.shape[0]
    assert depth == 1, "the final norm is fused into the last layer's combine step"
    x2 = x.reshape(batch * seq, d)
    out = _layer(x2, batch, seq, norm_attn_g[0], w_in[0], moba_out_g[0], q_a_norm_g[0], kv_a_norm_g[0], w_uq[0],
                 w_ukv[0], mla_out_g[0], w_o[0], norm_ffn_g[0], w_router[0], b_router[0], w1[0], b1[0], w2[0],
                 b2[0], norm_final_g)
    return out.reshape(batch, seq, d)
```

```python
import functools

import numpy as np
import jax
import jax.numpy as jnp
from jax import lax
from jax.experimental import pallas as pl
from jax.experimental.pallas import tpu as pltpu

MOBA_HEADS = 8
MOBA_HEAD_DIM = 128
MOBA_BLOCK = 256
MOBA_TOPK = 3
MLA_HEADS = 8
MLA_NOPE_DIM = 128
MLA_ROPE_DIM = 64
MLA_V_DIM = 128
MLA_Q_RANK = 512
MLA_KV_RANK = 512
ROPE_THETA = 10000.0
N_EXPERTS = 32
TOP_K = 4
SWIGLU_LIMIT = 7.0
SWIGLU_ALPHA = 1.702
EXPERT_BLOCK = 256
NORM_EPS = 1e-5

LANES = 128
QK_WIDTH = 256
ATT_TILE = 512
FLASH_STRIPS = 4
LOG2E = 1.4426950408889634
MASKED = -2.0 ** 100
M_INIT = -2.0 ** 98
FEAT_BLOCKS = 64
FEAT_ALIBI = FEAT_BLOCKS
V7X_VMEM_BUDGET = 56 * 1024 * 1024

F32 = jnp.float32
BF16 = jnp.bfloat16
U32 = jnp.uint32
I32 = jnp.int32


def _params(semantics, vmem_bytes):
    return pltpu.CompilerParams(dimension_semantics=semantics,
                                vmem_limit_bytes=min(int(vmem_bytes), V7X_VMEM_BUDGET))


def _rms(xf, g):
    ms = jnp.mean(xf * xf, axis=-1, keepdims=True)
    return xf * lax.rsqrt(ms + NORM_EPS) * g


def _dot(a, b):
    return jnp.dot(a, b, preferred_element_type=F32)


def _pack_halves(a, b):
    ai = lax.bitcast_convert_type(a.astype(BF16).astype(F32), U32)
    bi = lax.bitcast_convert_type(b.astype(BF16).astype(F32), U32)
    return ai | (bi >> 16)


def _unpack_halves(w):
    hi = lax.bitcast_convert_type(w & jnp.uint32(0xFFFF0000), F32)
    lo = lax.bitcast_convert_type(w << 16, F32)
    return hi, lo


def _inproj_kernel(x_ref, g_ref, w_ref, cos_ref, sin_ref, feat_ref,
                   qm_ref, kt_ref, vm_ref, kmean_ref, cq_ref, ckv_ref, kr_ref, *, q_scale):
    h = _rms(x_ref[...], g_ref[...]).astype(BF16)
    w = MOBA_HEADS * MOBA_HEAD_DIM
    dh = MOBA_HEAD_DIM
    tm = h.shape[0]

    def mm(lo, hi):
        return _dot(h, w_ref[:, lo:hi])

    qm_ref[...] = (mm(0, w) * q_scale).astype(BF16)
    k = mm(w, 2 * w)
    for b in range(tm // MOBA_BLOCK):
        kmean_ref[b] = jnp.mean(k[b * MOBA_BLOCK:(b + 1) * MOBA_BLOCK, :], axis=0, keepdims=True)
    v = mm(2 * w, 3 * w).astype(BF16)
    ones = jnp.ones((tm, QK_WIDTH - dh), BF16)
    for hh in range(MOBA_HEADS):
        kt_ref[hh, 0, :dh, :] = k[:, hh * dh:(hh + 1) * dh].T.astype(BF16)
        kt_ref[hh, 0, dh:, :] = feat_ref[0]
        vm_ref[:, hh * QK_WIDTH:hh * QK_WIDTH + dh] = v[:, hh * dh:(hh + 1) * dh]
        vm_ref[:, hh * QK_WIDTH + dh:(hh + 1) * QK_WIDTH] = ones
    c0 = 3 * w
    cq_ref[...] = mm(c0, c0 + MLA_Q_RANK)
    c1 = c0 + MLA_Q_RANK
    ckv_ref[...] = mm(c1, c1 + MLA_KV_RANK)
    c2 = c1 + MLA_KV_RANK
    kr = mm(c2, c2 + QK_WIDTH) * cos_ref[...] + mm(c2 + QK_WIDTH, c2 + 2 * QK_WIDTH) * sin_ref[...]
    kr_ref[...] = kr.astype(BF16)


def _in_proj(x2, g, w_all, cos_t, sin_t, feat, seq, tm):
    t, d = x2.shape
    w = MOBA_HEADS * MOBA_HEAD_DIM
    hw = MOBA_HEADS * QK_WIDTH
    nc = w_all.shape[1]
    n_pos = seq // tm
    nb = tm // MOBA_BLOCK
    row = lambda i: (i, 0)
    const = lambda i: (0, 0)
    pos = lambda i: (i % n_pos, 0)
    out_shape = [jax.ShapeDtypeStruct((t, w), BF16),
                 jax.ShapeDtypeStruct((MOBA_HEADS, t // tm, QK_WIDTH, tm), BF16),
                 jax.ShapeDtypeStruct((t, hw), BF16),
                 jax.ShapeDtypeStruct((t // MOBA_BLOCK, 1, w), F32),
                 jax.ShapeDtypeStruct((t, MLA_Q_RANK), F32), jax.ShapeDtypeStruct((t, MLA_KV_RANK), F32),
                 jax.ShapeDtypeStruct((t, QK_WIDTH), BF16)]
    out_specs = [pl.BlockSpec((tm, w), row),
                 pl.BlockSpec((MOBA_HEADS, 1, QK_WIDTH, tm), lambda i: (0, i, 0, 0)),
                 pl.BlockSpec((tm, hw), row),
                 pl.BlockSpec((nb, 1, w), lambda i: (i, 0, 0)),
                 pl.BlockSpec((tm, MLA_Q_RANK), row), pl.BlockSpec((tm, MLA_KV_RANK), row),
                 pl.BlockSpec((tm, QK_WIDTH), row)]
    vmem = (2 * tm * d * 4 + d * nc * 2 + 2 * tm * (w * 2 + 2 * hw * 2 + 2 * 512 * 4 + 256 * 2)
            + 4 * tm * QK_WIDTH * 4 + tm * d * 2 + 3 * tm * w * 4 + (4 << 20))
    return pl.pallas_call(
        functools.partial(_inproj_kernel, q_scale=MOBA_HEAD_DIM ** -0.5 * LOG2E),
        grid=(t // tm,),
        in_specs=[pl.BlockSpec((tm, d), row), pl.BlockSpec((1, d), const),
                  pl.BlockSpec((d, nc), const, pipeline_mode=pl.Buffered(1)),
                  pl.BlockSpec((tm, QK_WIDTH), pos), pl.BlockSpec((tm, QK_WIDTH), pos),
                  pl.BlockSpec((1, QK_WIDTH - MOBA_HEAD_DIM, tm), lambda i: (i % n_pos, 0, 0))],
        out_specs=out_specs, out_shape=out_shape,
        compiler_params=_params(("parallel",), vmem), name="in_proj",
    )(x2, g, w_all, cos_t, sin_t, feat)


def _mlaup_kernel(cq_ref, ckv_ref, gq_ref, gkv_ref, wqa_ref, wqb_ref, wk_ref, wv_ref,
                  kr_ref, cos_ref, sin_ref, q_ref, kt_ref, v_ref, *, q_scale):
    nq = _rms(cq_ref[...], gq_ref[...]).astype(BF16)
    nkv = _rms(ckv_ref[...], gkv_ref[...]).astype(BF16)
    cos = cos_ref[...]
    sin = sin_ref[...]
    kr = kr_ref[...].astype(F32)
    ones = jnp.ones((nq.shape[0], QK_WIDTH - MLA_V_DIM), BF16)
    for hh in range(MLA_HEADS):
        cols = slice(hh * QK_WIDTH, (hh + 1) * QK_WIDTH)
        q = _dot(nq, wqa_ref[:, cols]) * cos + _dot(nq, wqb_ref[:, cols]) * sin
        q_ref[:, cols] = (q * q_scale).astype(BF16)
        kt_ref[hh, 0] = (_dot(nkv, wk_ref[:, cols]) + kr).T.astype(BF16)
        v_ref[:, hh * QK_WIDTH:hh * QK_WIDTH + MLA_V_DIM] = _dot(
            nkv, wv_ref[:, hh * MLA_V_DIM:(hh + 1) * MLA_V_DIM]).astype(BF16)
        v_ref[:, hh * QK_WIDTH + MLA_V_DIM:(hh + 1) * QK_WIDTH] = ones


def _mla_up(cq, ckv, gq, gkv, wqa, wqb, wk, wv, kr, cos_t, sin_t, seq, tm):
    t = cq.shape[0]
    n_pos = seq // tm
    row = lambda i: (i, 0)
    const = lambda i: (0, 0)
    pos = lambda i: (i % n_pos, 0)
    hq = MLA_HEADS * QK_WIDTH
    hv = MLA_HEADS * MLA_V_DIM
    scale = (MLA_NOPE_DIM + MLA_ROPE_DIM) ** -0.5 * LOG2E
    vmem = (4 * tm * 512 * 4 + 2 * (3 * 512 * hq * 2 + 512 * hv * 2) + 8 * tm * QK_WIDTH * 4
            + 2 * tm * 3 * hq * 2 + (8 << 20))
    return pl.pallas_call(
        functools.partial(_mlaup_kernel, q_scale=scale),
        grid=(t // tm,),
        in_specs=[pl.BlockSpec((tm, MLA_Q_RANK), row), pl.BlockSpec((tm, MLA_KV_RANK), row),
                  pl.BlockSpec((1, MLA_Q_RANK), const), pl.BlockSpec((1, MLA_KV_RANK), const),
                  pl.BlockSpec((MLA_Q_RANK, hq), const), pl.BlockSpec((MLA_Q_RANK, hq), const),
                  pl.BlockSpec((MLA_KV_RANK, hq), const), pl.BlockSpec((MLA_KV_RANK, hv), const),
                  pl.BlockSpec((tm, QK_WIDTH), row),
                  pl.BlockSpec((tm, QK_WIDTH), pos), pl.BlockSpec((tm, QK_WIDTH), pos)],
        out_specs=[pl.BlockSpec((tm, hq), row),
                   pl.BlockSpec((MLA_HEADS, 1, QK_WIDTH, tm), lambda i: (0, i, 0, 0)),
                   pl.BlockSpec((tm, hq), row)],
        out_shape=[jax.ShapeDtypeStruct((t, hq), BF16),
                   jax.ShapeDtypeStruct((MLA_HEADS, t // tm, QK_WIDTH, tm), BF16),
                   jax.ShapeDtypeStruct((t, hq), BF16)],
        compiler_params=_params(("parallel",), vmem), name="mla_up",
    )(cq, ckv, gq, gkv, wqa, wqb, wk, wv, kr, cos_t, sin_t)


def _flash_step(q_ref, s_sc, kt_next, v, m_sc, acc_sc, mask=None):
    rows = s_sc.shape[0] // FLASH_STRIPS
    for r in range(FLASH_STRIPS):
        sl = slice(r * rows, (r + 1) * rows)
        s = s_sc[sl]
        if kt_next is not None:
            s_next = _dot(q_ref[sl], kt_next)
        if mask is not None:
            s = jnp.where(mask[sl], s, MASKED)
        m_old = m_sc[sl]
        m_new = jnp.maximum(m_old, jnp.max(s, axis=-1, keepdims=True))
        p = jnp.exp2(s - m_new)
        acc_sc[sl] = jnp.exp2(m_old - m_new) * acc_sc[sl] + _dot(p.astype(BF16), v)
        m_sc[sl] = m_new
        if kt_next is not None:
            s_sc[sl] = s_next


def _flash_loop(i, q_ref, kt_ref, v_ref, o_ref, s_sc, m_sc, acc_sc, diag_mask, *, tile, dv):
    m_sc[...] = jnp.full_like(m_sc, M_INIT)
    acc_sc[...] = jnp.zeros_like(acc_sc)
    s_sc[...] = _dot(q_ref[...], kt_ref[0, 0])

    def v_tile(j):
        return v_ref[pl.ds(pl.multiple_of(j * tile, tile), tile), :]

    def past(j, carry):
        _flash_step(q_ref, s_sc, kt_ref[0, j + 1], v_tile(j), m_sc, acc_sc)
        return carry

    lax.fori_loop(0, i, past, 0)
    _flash_step(q_ref, s_sc, None, v_tile(i), m_sc, acc_sc, mask=diag_mask)
    acc = acc_sc[...]
    o_ref[...] = acc[:, :dv] / acc[:, dv:2 * dv]


def _mla_attn_kernel(q_ref, kt_ref, v_ref, o_ref, s_sc, m_sc, acc_sc, *, tile):
    row = lax.broadcasted_iota(I32, (tile, tile), 0)
    col = lax.broadcasted_iota(I32, (tile, tile), 1)
    _flash_loop(pl.program_id(2), q_ref, kt_ref, v_ref, o_ref, s_sc, m_sc, acc_sc, row >= col,
                tile=tile, dv=MLA_V_DIM)


def _attn_call(kernel, name, q_spec, extra_specs, scratch, args, batch, heads, seq, tile, dv):
    t = batch * seq
    nq = seq // tile
    vmem = 2 * 2 * seq * QK_WIDTH * 2 + 10 * tile * tile * 4 + 10 * tile * QK_WIDTH * 4 + (4 << 20)
    return pl.pallas_call(
        kernel,
        grid=(batch, heads, nq),
        in_specs=[q_spec,
                  pl.BlockSpec((1, nq, QK_WIDTH, tile), lambda b, h, i: (h, b, 0, 0)),
                  pl.BlockSpec((seq, QK_WIDTH), lambda b, h, i: (b, h))] + extra_specs,
        out_specs=pl.BlockSpec((tile, dv), lambda b, h, i: (b * nq + i, h)),
        out_shape=jax.ShapeDtypeStruct((t, heads * dv), F32),
        scratch_shapes=[pltpu.VMEM((tile, tile), F32), pltpu.VMEM((tile, 1), F32),
                        pltpu.VMEM((tile, QK_WIDTH), F32)] + scratch,
        compiler_params=_params(("parallel", "parallel", "arbitrary"), vmem), name=name,
    )(*args)


def _mla_attn(qc, kt, v, batch, seq, tile):
    nq = seq // tile
    return _attn_call(functools.partial(_mla_attn_kernel, tile=tile), "mla_attn",
                      pl.BlockSpec((tile, QK_WIDTH), lambda b, h, i: (b * nq + i, h)), [], [],
                      (qc, kt, v), batch, MLA_HEADS, seq, tile, MLA_V_DIM)


def _moba_attn_kernel(q_ref, kt_ref, v_ref, kmt_ref, aq_ref, o_ref, s_sc, m_sc, acc_sc, qa_sc, *, tile):
    u = pl.program_id(2)
    blk = MOBA_BLOCK
    q = q_ref[...]
    km = kmt_ref[0]
    km_hi = km.astype(BF16)
    km_lo = (km - km_hi.astype(F32)).astype(BF16)
    gate = _dot(q, km_hi) + _dot(q, km_lo)
    lane = lax.broadcasted_iota(I32, (tile, LANES), 1)
    own = u * (tile // blk) + lax.broadcasted_iota(I32, (tile, LANES), 0) // blk
    g = jnp.where(lane < own, gate, -jnp.inf)
    keep = lane == own
    for _ in range(MOBA_TOPK):
        mx = jnp.max(g, axis=-1, keepdims=True)
        cand = jnp.where((g == mx) & (mx > -jnp.inf), lane, LANES)
        pick = lane == jnp.min(cand, axis=-1, keepdims=True)
        keep = keep | pick
        g = jnp.where(pick, -jnp.inf, g)
    feat = jnp.where(lane < FEAT_BLOCKS, jnp.where(keep, 0.0, 1.0), aq_ref[0])
    qa_sc[:, :MOBA_HEAD_DIM] = q
    qa_sc[:, MOBA_HEAD_DIM:] = feat.astype(BF16)

    row = lax.broadcasted_iota(I32, (tile, tile), 0)
    col = lax.broadcasted_iota(I32, (tile, tile), 1)
    causal_in_own_block = (row // blk != col // blk) | (col <= row)
    _flash_loop(u, qa_sc, kt_ref, v_ref, o_ref, s_sc, m_sc, acc_sc, causal_in_own_block,
                tile=tile, dv=MOBA_HEAD_DIM)


def _moba_attn(qm, kt, vm, kmt, aq, batch, seq, tile):
    nq = seq // tile
    dh = MOBA_HEAD_DIM
    return _attn_call(functools.partial(_moba_attn_kernel, tile=tile), "moba_attn",
                      pl.BlockSpec((tile, dh), lambda b, h, i: (b * nq + i, h)),
                      [pl.BlockSpec((1, dh, LANES), lambda b, h, i: (b * MOBA_HEADS + h, 0, 0)),
                       pl.BlockSpec((1, 1, LANES), lambda b, h, i: (h, 0, 0))],
                      [pltpu.VMEM((tile, QK_WIDTH), BF16)],
                      (qm, kt, vm, kmt, aq), batch, MOBA_HEADS, seq, tile, dh)


def _outproj_kernel(om_ref, ol_ref, gm_ref, gl_ref, wo_ref, x_ref, gf_ref, wrh_ref, wrl_ref, br_ref,
                    x1_ref, hp_ref, lg_ref):
    wm = om_ref.shape[1]
    a = _rms(om_ref[...], gm_ref[...]).astype(BF16)
    b = _rms(ol_ref[...], gl_ref[...]).astype(BF16)
    x1 = x_ref[...] + _dot(a, wo_ref[:wm, :]) + _dot(b, wo_ref[wm:, :])
    x1_ref[...] = x1
    h2 = _rms(x1, gf_ref[...])
    half = h2.shape[1] // 2
    hp_ref[...] = _pack_halves(h2[:, :half], h2[:, half:])
    h_hi = h2.astype(BF16)
    h_lo = (h2 - h_hi.astype(F32)).astype(BF16)
    lg_ref[...] = (_dot(h_hi, wrh_ref[...]) + _dot(h_lo, wrh_ref[...]) + _dot(h_hi, wrl_ref[...])
                   + br_ref[...])


def _out_proj(om, ol, gm, gl, wo, x2, gf, wr_hi, wr_lo, br, tm):
    t, d = x2.shape
    wm = om.shape[1]
    wl = ol.shape[1]
    row = lambda i: (i, 0)
    const = lambda i: (0, 0)
    vmem = 2 * (wm + wl) * d * 2 + 2 * tm * (wm + wl + 2 * d) * 4 + 2 * tm * d * 2 + 6 * tm * d * 4 + (4 << 20)
    return pl.pallas_call(
        _outproj_kernel,
        grid=(t // tm,),
        in_specs=[pl.BlockSpec((tm, wm), row), pl.BlockSpec((tm, wl), row),
                  pl.BlockSpec((1, wm), const), pl.BlockSpec((1, wl), const),
                  pl.BlockSpec((wm + wl, d), const), pl.BlockSpec((tm, d), row), pl.BlockSpec((1, d), const),
                  pl.BlockSpec((d, LANES), const), pl.BlockSpec((d, LANES), const), pl.BlockSpec((1, LANES), const)],
        out_specs=[pl.BlockSpec((tm, d), row), pl.BlockSpec((tm, d // 2), row), pl.BlockSpec((tm, LANES), row)],
        out_shape=[jax.ShapeDtypeStruct((t, d), F32), jax.ShapeDtypeStruct((t, d // 2), U32),
                   jax.ShapeDtypeStruct((t, LANES), F32)],
        compiler_params=_params(("parallel",), vmem), name="out_proj",
    )(om, ol, gm, gl, wo, x2, gf, wr_hi, wr_lo, br)


def _router_kernel(lg_ref, idx_ref, gate_ref, pos_ref, cnt_ref, *, tm):
    i = pl.program_id(0)

    @pl.when(i == 0)
    def _():
        cnt_ref[...] = jnp.zeros_like(cnt_ref)

    lane = lax.broadcasted_iota(I32, (tm, LANES), 1)
    work = jnp.where(lane < N_EXPERTS, lg_ref[...], -jnp.inf)
    vals, firsts, picks = [], [], []
    for _ in range(TOP_K):
        mx = jnp.max(work, axis=-1, keepdims=True)
        first = jnp.min(jnp.where(work == mx, lane, LANES), axis=-1, keepdims=True)
        pick = lane == first
        work = jnp.where(pick, -jnp.inf, work)
        vals.append(mx)
        firsts.append(first)
        picks.append(pick)
    es = [jnp.exp(v - vals[0]) for v in vals]
    denom = es[0] + es[1] + es[2] + es[3]
    chosen = jnp.zeros((tm, LANES), F32)
    for p in picks:
        chosen = jnp.where(p, 1.0, chosen)
    r = lax.broadcasted_iota(I32, (tm, tm), 0)
    cc = lax.broadcasted_iota(I32, (tm, tm), 1)
    before = (r > cc).astype(BF16)
    prior = _dot(before, chosen.astype(BF16)) + cnt_ref[...]
    lane4 = lax.broadcasted_iota(I32, (tm, TOP_K), 1)
    idx = jnp.zeros((tm, TOP_K), I32)
    gate = jnp.zeros((tm, TOP_K), F32)
    pos = jnp.zeros((tm, TOP_K), I32)
    for j in range(TOP_K):
        pj = jnp.sum(jnp.where(picks[j], prior, 0.0), axis=-1, keepdims=True)
        idx = jnp.where(lane4 == j, firsts[j], idx)
        gate = jnp.where(lane4 == j, es[j] / denom, gate)
        pos = jnp.where(lane4 == j, pj.astype(I32), pos)
    idx_ref[...] = idx
    gate_ref[...] = gate
    pos_ref[...] = pos
    cnt_ref[...] += jnp.sum(chosen, axis=0, keepdims=True)


def _router(logits, tm):
    t = logits.shape[0]
    row = lambda i: (i, 0)
    vmem = 4 * tm * tm * 4 + 24 * tm * LANES * 4 + (4 << 20)
    return pl.pallas_call(
        functools.partial(_router_kernel, tm=tm),
        grid=(t // tm,),
        in_specs=[pl.BlockSpec((tm, LANES), row)],
        out_specs=[pl.BlockSpec((tm, TOP_K), row)] * 3 + [pl.BlockSpec((1, LANES), lambda i: (0, 0))],
        out_shape=[jax.ShapeDtypeStruct((t, TOP_K), I32), jax.ShapeDtypeStruct((t, TOP_K), F32),
                   jax.ShapeDtypeStruct((t, TOP_K), I32), jax.ShapeDtypeStruct((1, LANES), F32)],
        compiler_params=_params(("arbitrary",), vmem), name="router",
    )(logits)


DISPATCH_GROUP = 4


def _dispatch_kernel(src_ref, nu_ref, hp_ref, xb_ref, sem):
    i = pl.program_id(0)
    rows = xb_ref.shape[0]
    used = i * DISPATCH_GROUP < nu_ref[0]

    def row_copy(r):
        return pltpu.make_async_copy(hp_ref.at[pl.ds(src_ref[i * rows + r], 1)], xb_ref.at[pl.ds(r, 1)], sem)

    @pl.when(used)
    def _():
        def start(r, carry):
            row_copy(r).start()
            return carry

        def wait(r, carry):
            row_copy(r).wait()
            return carry

        lax.fori_loop(0, rows, start, 0, unroll=8)
        lax.fori_loop(0, rows, wait, 0, unroll=8)

    @pl.when(jnp.logical_not(used))
    def _():
        xb_ref[...] = jnp.zeros_like(xb_ref)


def _dispatch(src, n_used, hp, n_rows):
    t, half = hp.shape
    rows = EXPERT_BLOCK * DISPATCH_GROUP
    assert n_rows % rows == 0
    grid_spec = pltpu.PrefetchScalarGridSpec(
        num_scalar_prefetch=2, grid=(n_rows // rows,),
        in_specs=[pl.BlockSpec(memory_space=pl.ANY)],
        out_specs=pl.BlockSpec((rows, half), lambda i, src, nu: (i, 0)),
        scratch_shapes=[pltpu.SemaphoreType.DMA])
    return pl.pallas_call(
        _dispatch_kernel,
        grid_spec=grid_spec,
        out_shape=jax.ShapeDtypeStruct((n_rows, half), U32),
        compiler_params=_params(("arbitrary",), 3 * rows * half * 4 + (4 << 20)),
        name="dispatch",
    )(src, n_used, hp)


def _moe_up_kernel(be_ref, nu_ref, x_ref, w1_ref, b1_ref, act_ref, *, fc):
    i = pl.program_id(0)

    @pl.when(i < nu_ref[0])
    def _():
        hi, lo = _unpack_halves(x_ref[...])
        a = hi.astype(BF16)
        b = lo.astype(BF16)
        half = a.shape[1]
        d_ff = act_ref.shape[1]

        def pre(c0):
            return (_dot(a, w1_ref[0, :half, c0:c0 + fc]) + _dot(b, w1_ref[0, half:, c0:c0 + fc])
                    + b1_ref[0, :, c0:c0 + fc])

        for c in range(d_ff // fc):
            glu = jnp.minimum(pre(c * fc), SWIGLU_LIMIT)
            lin = jnp.clip(pre(d_ff + c * fc), -SWIGLU_LIMIT, SWIGLU_LIMIT)
            act = glu * jax.nn.sigmoid(SWIGLU_ALPHA * glu) * (lin + 1.0)
            act_ref[:, c * fc:(c + 1) * fc] = act.astype(BF16)

    @pl.when(i >= nu_ref[0])
    def _():
        act_ref[...] = jnp.zeros_like(act_ref)


def _moe_up(block_expert, n_used, xb, w1, b1):
    n_rows, half = xb.shape
    e, d, f2 = w1.shape
    d_ff = f2 // 2
    blk = EXPERT_BLOCK
    n_blocks = n_rows // blk
    fc = 512
    live = lambda i, be, nu: jnp.minimum(i, nu[0] - 1)
    grid_spec = pltpu.PrefetchScalarGridSpec(
        num_scalar_prefetch=2, grid=(n_blocks,),
        in_specs=[pl.BlockSpec((blk, half), lambda i, be, nu: (live(i, be, nu), 0)),
                  pl.BlockSpec((1, d, f2), lambda i, be, nu: (be[live(i, be, nu)], 0, 0)),
                  pl.BlockSpec((1, 1, f2), lambda i, be, nu: (be[live(i, be, nu)], 0, 0))],
        out_specs=pl.BlockSpec((blk, d_ff), lambda i, be, nu: (i, 0)))
    vmem = 2 * d * f2 * 2 + 2 * blk * half * 4 + 2 * blk * d_ff * 2 + 8 * blk * fc * 4 + 2 * blk * d * 2 + (4 << 20)
    return pl.pallas_call(
        functools.partial(_moe_up_kernel, fc=fc),
        grid_spec=grid_spec,
        out_shape=jax.ShapeDtypeStruct((n_rows, d_ff), BF16),
        compiler_params=_params(("arbitrary",), vmem), name="moe_up",
    )(block_expert, n_used, xb, w1, b1)


def _moe_down_kernel(be_ref, nu_ref, a_ref, w2_ref, b2_ref, y_ref):
    i = pl.program_id(0)

    @pl.when(i < nu_ref[0])
    def _():
        y = _dot(a_ref[...], w2_ref[0]) + b2_ref[0]
        half = y.shape[1] // 2
        y_ref[...] = _pack_halves(y[:, :half], y[:, half:])

    @pl.when(i >= nu_ref[0])
    def _():
        y_ref[...] = jnp.zeros_like(y_ref)


def _moe_down(block_expert, n_used, act, w2, b2):
    n_rows, d_ff = act.shape
    e, _, d = w2.shape
    blk = EXPERT_BLOCK
    n_blocks = n_rows // blk
    live = lambda i, be, nu: jnp.minimum(i, nu[0] - 1)
    grid_spec = pltpu.PrefetchScalarGridSpec(
        num_scalar_prefetch=2, grid=(n_blocks,),
        in_specs=[pl.BlockSpec((blk, d_ff), lambda i, be, nu: (live(i, be, nu), 0)),
                  pl.BlockSpec((1, d_ff, d), lambda i, be, nu: (be[live(i, be, nu)], 0, 0)),
                  pl.BlockSpec((1, 1, d), lambda i, be, nu: (be[live(i, be, nu)], 0, 0))],
        out_specs=pl.BlockSpec((blk, d // 2), lambda i, be, nu: (i, 0)))
    vmem = 2 * d_ff * d * 2 + 2 * blk * d_ff * 2 + 2 * blk * d * 2 + 4 * blk * d * 4 + (4 << 20)
    return pl.pallas_call(
        _moe_down_kernel,
        grid_spec=grid_spec,
        out_shape=jax.ShapeDtypeStruct((n_rows, d // 2), U32),
        compiler_params=_params(("arbitrary",), vmem), name="moe_down",
    )(block_expert, n_used, act, w2, b2)


def _combine_kernel(dest_ref, gate_ref, x1_ref, gfin_ref, yb_ref, o_ref, rows_sc, sem, *, tm):
    i = pl.program_id(0)
    base = i * tm

    def row_copy(r, j):
        return pltpu.make_async_copy(yb_ref.at[pl.ds(dest_ref[(base + r) * TOP_K + j], 1)],
                                     rows_sc.at[j, pl.ds(r, 1)], sem)

    def start(r, carry):
        for j in range(TOP_K):
            row_copy(r, j).start()
        return carry

    def wait(r, carry):
        for j in range(TOP_K):
            row_copy(r, j).wait()
        return carry

    lax.fori_loop(0, tm, start, 0, unroll=4)
    lax.fori_loop(0, tm, wait, 0, unroll=4)

    x1 = x1_ref[...]
    half = x1.shape[1] // 2
    gates = gate_ref[...]
    left = x1[:, :half]
    right = x1[:, half:]
    for j in range(TOP_K):
        hi, lo = _unpack_halves(rows_sc[j])
        gj = gates[:, j:j + 1]
        left = left + gj * hi
        right = right + gj * lo
    ms = (jnp.sum(left * left, axis=-1, keepdims=True) + jnp.sum(right * right, axis=-1, keepdims=True)) / (2 * half)
    inv = lax.rsqrt(ms + NORM_EPS)
    g = gfin_ref[...]
    o_ref[:, :half] = left * inv * g[:, :half]
    o_ref[:, half:] = right * inv * g[:, half:]


def _combine(dest, gates, x1, gfin, yb, tm):
    t, d = x1.shape
    half = d // 2
    grid_spec = pltpu.PrefetchScalarGridSpec(
        num_scalar_prefetch=1, grid=(t // tm,),
        in_specs=[pl.BlockSpec((tm, TOP_K), lambda i, dst: (i, 0)),
                  pl.BlockSpec((tm, d), lambda i, dst: (i, 0)),
                  pl.BlockSpec((1, d), lambda i, dst: (0, 0)),
                  pl.BlockSpec(memory_space=pl.ANY)],
        out_specs=pl.BlockSpec((tm, d), lambda i, dst: (i, 0)),
        scratch_shapes=[pltpu.VMEM((TOP_K, tm, half), U32), pltpu.SemaphoreType.DMA])
    vmem = TOP_K * tm * half * 4 + 4 * tm * d * 4 + 6 * tm * d * 4 + 2 * tm * LANES * 4 + (4 << 20)
    return pl.pallas_call(
        functools.partial(_combine_kernel, tm=tm),
        grid_spec=grid_spec,
        out_shape=jax.ShapeDtypeStruct((t, d), F32),
        compiler_params=_params(("arbitrary",), vmem), name="combine",
    )(dest, gates, x1, gfin, yb)


def _rope_tables(seq):
    inv = 1.0 / (ROPE_THETA ** (jnp.arange(0, MLA_ROPE_DIM, 2, dtype=F32) / MLA_ROPE_DIM))
    ang = jnp.arange(seq, dtype=F32)[:, None] * inv[None, :]
    ang = jnp.concatenate([ang, ang], axis=-1)
    ones = jnp.ones((seq, MLA_NOPE_DIM), F32)
    zeros = jnp.zeros((seq, QK_WIDTH - MLA_NOPE_DIM - MLA_ROPE_DIM), F32)
    cos_t = jnp.concatenate([ones, jnp.cos(ang), zeros], axis=-1)
    sin_t = jnp.concatenate([0.0 * ones, jnp.sin(ang), zeros], axis=-1)
    return cos_t, sin_t


def _bf16_pieces(x, n=3):
    out = []
    for _ in range(n):
        p = float(np.asarray(x, dtype=np.float32).astype(jnp.bfloat16).astype(np.float32))
        out.append(p)
        x = x - p
    return out


def _moba_features(seq, tile):
    pos = np.arange(seq)
    blk, off = pos // MOBA_BLOCK, pos % MOBA_BLOCK
    assert seq // MOBA_BLOCK <= FEAT_BLOCKS
    feat = np.zeros((QK_WIDTH - MOBA_HEAD_DIM, seq), np.float32)
    feat[:FEAT_BLOCKS] = np.where(blk[None, :] == np.arange(FEAT_BLOCKS)[:, None], MASKED, 0.0)
    feat[FEAT_ALIBI:FEAT_ALIBI + 3] = blk[None, :]
    feat[FEAT_ALIBI + 3:FEAT_ALIBI + 6] = off[None, :]
    feat = feat.reshape(feat.shape[0], seq // tile, tile).transpose(1, 0, 2)
    aq = np.zeros((MOBA_HEADS, 1, LANES), np.float32)
    for h in range(MOBA_HEADS):
        slope = 2.0 ** (-8.0 * (h + 1) / MOBA_HEADS)
        aq[h, 0, FEAT_ALIBI:FEAT_ALIBI + 3] = _bf16_pieces(slope * LOG2E * MOBA_BLOCK)
        aq[h, 0, FEAT_ALIBI + 3:FEAT_ALIBI + 6] = _bf16_pieces(slope * LOG2E)
    return jnp.asarray(feat, BF16), jnp.asarray(aq, F32)


def _rot_cols(w):
    half = w.shape[-1] // 2
    return jnp.concatenate([-w[..., half:], w[..., :half]], axis=-1)


def _rope_slot(w):
    k = w.shape[0]
    return jnp.concatenate([jnp.zeros((k, MLA_NOPE_DIM), w.dtype), w,
                            jnp.zeros((k, QK_WIDTH - MLA_NOPE_DIM - MLA_ROPE_DIM), w.dtype)], axis=-1)


def _layer(x2, batch, seq, norm_attn_g, w_in, moba_out_g, q_a_norm_g, kv_a_norm_g, w_uq, w_ukv, mla_out_g,
           w_o, norm_ffn_g, w_router, b_router, w1, b1, w2, b2, final_g):
    t, d = x2.shape
    wm = MOBA_HEADS * MOBA_HEAD_DIM
    tile = ATT_TILE
    assert seq % tile == 0 and tile % MOBA_BLOCK == 0
    cos_t, sin_t = _rope_tables(seq)
    feat, aq = _moba_features(seq, tile)

    c_kr = 3 * wm + MLA_Q_RANK + MLA_KV_RANK
    w_kr = w_in[:, c_kr:c_kr + MLA_ROPE_DIM]
    w_all = jnp.concatenate([w_in[:, :c_kr], _rope_slot(w_kr), _rope_slot(_rot_cols(w_kr))], axis=1).astype(BF16)
    wq = w_uq.reshape(MLA_Q_RANK, MLA_HEADS, MLA_NOPE_DIM + MLA_ROPE_DIM)
    zq = jnp.zeros((MLA_Q_RANK, MLA_HEADS, QK_WIDTH - MLA_NOPE_DIM - MLA_ROPE_DIM), F32)
    wqa = jnp.concatenate([wq, zq], axis=-1).reshape(MLA_Q_RANK, -1).astype(BF16)
    wqb = jnp.concatenate([jnp.zeros((MLA_Q_RANK, MLA_HEADS, MLA_NOPE_DIM), F32),
                           _rot_cols(wq[..., MLA_NOPE_DIM:]), zq], axis=-1).reshape(MLA_Q_RANK, -1).astype(BF16)
    wkv = w_ukv.reshape(MLA_KV_RANK, MLA_HEADS, MLA_NOPE_DIM + MLA_V_DIM)
    wk = jnp.concatenate([wkv[..., :MLA_NOPE_DIM],
                          jnp.zeros((MLA_KV_RANK, MLA_HEADS, QK_WIDTH - MLA_NOPE_DIM), F32)],
                         axis=-1).reshape(MLA_KV_RANK, -1).astype(BF16)
    wv = wkv[..., MLA_NOPE_DIM:].reshape(MLA_KV_RANK, -1).astype(BF16)
    wr = jnp.pad(w_router, ((0, 0), (0, LANES - N_EXPERTS)))
    wr_hi = wr.astype(BF16)
    wr_lo = (wr - wr_hi.astype(F32)).astype(BF16)
    br = jnp.pad(b_router, (0, LANES - N_EXPERTS))[None, :]

    qm, ktm, vm, kmean, cq, ckv, kr = _in_proj(x2, norm_attn_g[None, :], w_all, cos_t, sin_t, feat, seq, tm=tile)
    qc, ktl, vl = _mla_up(cq, ckv, q_a_norm_g[None, :], kv_a_norm_g[None, :], wqa, wqb, wk, wv, kr,
                          cos_t, sin_t, seq, tm=tile)
    n_blk = seq // MOBA_BLOCK
    kmt = kmean.reshape(batch, n_blk, MOBA_HEADS, MOBA_HEAD_DIM).transpose(0, 2, 3, 1)
    kmt = jnp.pad(kmt, ((0, 0), (0, 0), (0, 0), (0, LANES - n_blk))).reshape(batch * MOBA_HEADS, MOBA_HEAD_DIM, LANES)
    o_moba = _moba_attn(qm, ktm, vm, kmt, aq, batch, seq, tile)
    o_mla = _mla_attn(qc, ktl, vl, batch, seq, tile)
    x1, hp, logits = _out_proj(o_moba, o_mla, moba_out_g[None, :], mla_out_g[None, :], w_o.astype(BF16), x2,
                               norm_ffn_g[None, :], wr_hi, wr_lo, br, tm=256)

    idx, gates, pos, counts = _router(logits, tm=min(512, t))
    blk = EXPERT_BLOCK
    n_blocks = -(-(t * TOP_K) // blk) + N_EXPERTS
    counts = counts[0, :N_EXPERTS].astype(I32)
    padded = ((counts + blk - 1) // blk) * blk
    pad_end = jnp.cumsum(padded)
    pad_start = pad_end - padded
    dest = (pad_start[idx] + pos).reshape(-1)
    block_expert = jnp.minimum(
        jnp.sum((jnp.arange(n_blocks, dtype=I32) * blk)[:, None] >= pad_end[None, :], axis=1), N_EXPERTS - 1).astype(I32)
    n_used = (pad_end[-1:] // blk).astype(I32)
    src = jnp.zeros((n_blocks * blk,), I32).at[dest].set(jnp.arange(t * TOP_K, dtype=I32) // TOP_K,
                                                          unique_indices=True)

    xb = _dispatch(src, n_used, hp, n_blocks * blk)
    act = _moe_up(block_expert, n_used, xb, w1.astype(BF16), b1[:, None, :])
    yb = _moe_down(block_expert, n_used, act, w2.astype(BF16), b2[:, None, :])
    return _combine(dest, gates, x1, final_g[None, :], yb, tm=256)


def kernel(x, norm_attn_g, w_in, moba_out_g, q_a_norm_g, kv_a_norm_g, w_uq, w_ukv, mla_out_g, w_o, norm_ffn_g,
           w_router, b_router, w1, b1, w2, b2, norm_final_g):
    batch, seq, d = x.shape
    depth = w_in.shape[0]
    assert depth == 1, "the final norm is fused into the last layer's combine step"
    x2 = x.reshape(batch * seq, d)
    out = _layer(x2, batch, seq, norm_attn_g[0], w_in[0], moba_out_g[0], q_a_norm_g[0], kv_a_norm_g[0], w_uq[0],
                 w_ukv[0], mla_out_g[0], w_o[0], norm_ffn_g[0], w_router[0], b_router[0], w1[0], b1[0], w2[0],
                 b2[0], norm_final_g)
    return out.reshape(batch, seq, d)
```

```python
import functools

import numpy as np
import jax
import jax.numpy as jnp
from jax import lax
from jax.experimental import pallas as pl
from jax.experimental.pallas import tpu as pltpu

MOBA_HEADS = 8
MOBA_HEAD_DIM = 128
MOBA_BLOCK = 256
MOBA_TOPK = 3
MLA_HEADS = 8
MLA_NOPE_DIM = 128
MLA_ROPE_DIM = 64
MLA_V_DIM = 128
MLA_Q_RANK = 512
MLA_KV_RANK = 512
ROPE_THETA = 10000.0
N_EXPERTS = 32
TOP_K = 4
SWIGLU_LIMIT = 7.0
SWIGLU_ALPHA = 1.702
EXPERT_BLOCK = 256
NORM_EPS = 1e-5

LANES = 128
QK_WIDTH = 256
ATT_TILE = 512
FLASH_STRIPS = 2
LOG2E = 1.4426950408889634
MASKED = -2.0 ** 100
M_INIT = -2.0 ** 98
FEAT_BLOCKS = 64
FEAT_ALIBI = FEAT_BLOCKS
V7X_VMEM_BUDGET = 56 * 1024 * 1024

F32 = jnp.float32
BF16 = jnp.bfloat16
U32 = jnp.uint32
I32 = jnp.int32


def _params(semantics, vmem_bytes):
    return pltpu.CompilerParams(dimension_semantics=semantics,
                                vmem_limit_bytes=min(int(vmem_bytes), V7X_VMEM_BUDGET))


def _rms(xf, g):
    ms = jnp.mean(xf * xf, axis=-1, keepdims=True)
    return xf * lax.rsqrt(ms + NORM_EPS) * g


def _dot(a, b):
    return jnp.dot(a, b, preferred_element_type=F32)


def _pack_halves(a, b):
    ai = lax.bitcast_convert_type(a.astype(BF16).astype(F32), U32)
    bi = lax.bitcast_convert_type(b.astype(BF16).astype(F32), U32)
    return ai | (bi >> 16)


def _unpack_halves(w):
    hi = lax.bitcast_convert_type(w & jnp.uint32(0xFFFF0000), F32)
    lo = lax.bitcast_convert_type(w << 16, F32)
    return hi, lo


def _inproj_kernel(x_ref, g_ref, w_ref, cos_ref, sin_ref, feat_ref,
                   qm_ref, kt_ref, vm_ref, kmean_ref, cq_ref, ckv_ref, kr_ref, *, q_scale):
    h = _rms(x_ref[...], g_ref[...]).astype(BF16)
    w = MOBA_HEADS * MOBA_HEAD_DIM
    dh = MOBA_HEAD_DIM
    tm = h.shape[0]

    def mm(lo, hi):
        return _dot(h, w_ref[:, lo:hi])

    qm_ref[...] = (mm(0, w) * q_scale).astype(BF16)
    k = mm(w, 2 * w)
    for b in range(tm // MOBA_BLOCK):
        kmean_ref[b] = jnp.mean(k[b * MOBA_BLOCK:(b + 1) * MOBA_BLOCK, :], axis=0, keepdims=True)
    v = mm(2 * w, 3 * w).astype(BF16)
    ones = jnp.ones((tm, QK_WIDTH - dh), BF16)
    for hh in range(MOBA_HEADS):
        kt_ref[hh, 0, :dh, :] = k[:, hh * dh:(hh + 1) * dh].T.astype(BF16)
        kt_ref[hh, 0, dh:, :] = feat_ref[0]
        vm_ref[:, hh * QK_WIDTH:hh * QK_WIDTH + dh] = v[:, hh * dh:(hh + 1) * dh]
        vm_ref[:, hh * QK_WIDTH + dh:(hh + 1) * QK_WIDTH] = ones
    c0 = 3 * w
    cq_ref[...] = mm(c0, c0 + MLA_Q_RANK)
    c1 = c0 + MLA_Q_RANK
    ckv_ref[...] = mm(c1, c1 + MLA_KV_RANK)
    c2 = c1 + MLA_KV_RANK
    kr = mm(c2, c2 + QK_WIDTH) * cos_ref[...] + mm(c2 + QK_WIDTH, c2 + 2 * QK_WIDTH) * sin_ref[...]
    kr_ref[...] = kr.astype(BF16)


def _in_proj(x2, g, w_all, cos_t, sin_t, feat, seq, tm):
    t, d = x2.shape
    w = MOBA_HEADS * MOBA_HEAD_DIM
    hw = MOBA_HEADS * QK_WIDTH
    nc = w_all.shape[1]
    n_pos = seq // tm
    nb = tm // MOBA_BLOCK
    row = lambda i: (i, 0)
    const = lambda i: (0, 0)
    pos = lambda i: (i % n_pos, 0)
    out_shape = [jax.ShapeDtypeStruct((t, w), BF16),
                 jax.ShapeDtypeStruct((MOBA_HEADS, t // tm, QK_WIDTH, tm), BF16),
                 jax.ShapeDtypeStruct((t, hw), BF16),
                 jax.ShapeDtypeStruct((t // MOBA_BLOCK, 1, w), F32),
                 jax.ShapeDtypeStruct((t, MLA_Q_RANK), F32), jax.ShapeDtypeStruct((t, MLA_KV_RANK), F32),
                 jax.ShapeDtypeStruct((t, QK_WIDTH), BF16)]
    out_specs = [pl.BlockSpec((tm, w), row),
                 pl.BlockSpec((MOBA_HEADS, 1, QK_WIDTH, tm), lambda i: (0, i, 0, 0)),
                 pl.BlockSpec((tm, hw), row),
                 pl.BlockSpec((nb, 1, w), lambda i: (i, 0, 0)),
                 pl.BlockSpec((tm, MLA_Q_RANK), row), pl.BlockSpec((tm, MLA_KV_RANK), row),
                 pl.BlockSpec((tm, QK_WIDTH), row)]
    vmem = (2 * tm * d * 4 + d * nc * 2 + 2 * tm * (w * 2 + 2 * hw * 2 + 2 * 512 * 4 + 256 * 2)
            + 4 * tm * QK_WIDTH * 4 + tm * d * 2 + 3 * tm * w * 4 + (4 << 20))
    return pl.pallas_call(
        functools.partial(_inproj_kernel, q_scale=MOBA_HEAD_DIM ** -0.5 * LOG2E),
        grid=(t // tm,),
        in_specs=[pl.BlockSpec((tm, d), row), pl.BlockSpec((1, d), const),
                  pl.BlockSpec((d, nc), const, pipeline_mode=pl.Buffered(1)),
                  pl.BlockSpec((tm, QK_WIDTH), pos), pl.BlockSpec((tm, QK_WIDTH), pos),
                  pl.BlockSpec((1, QK_WIDTH - MOBA_HEAD_DIM, tm), lambda i: (i % n_pos, 0, 0))],
        out_specs=out_specs, out_shape=out_shape,
        compiler_params=_params(("parallel",), vmem), name="in_proj",
    )(x2, g, w_all, cos_t, sin_t, feat)


def _mlaup_kernel(cq_ref, ckv_ref, gq_ref, gkv_ref, wqa_ref, wqb_ref, wk_ref, wv_ref,
                  kr_ref, cos_ref, sin_ref, q_ref, kt_ref, v_ref, *, q_scale):
    nq = _rms(cq_ref[...], gq_ref[...]).astype(BF16)
    nkv = _rms(ckv_ref[...], gkv_ref[...]).astype(BF16)
    cos = cos_ref[...]
    sin = sin_ref[...]
    kr = kr_ref[...].astype(F32)
    ones = jnp.ones((nq.shape[0], QK_WIDTH - MLA_V_DIM), BF16)
    for hh in range(MLA_HEADS):
        cols = slice(hh * QK_WIDTH, (hh + 1) * QK_WIDTH)
        q = _dot(nq, wqa_ref[:, cols]) * cos + _dot(nq, wqb_ref[:, cols]) * sin
        q_ref[:, cols] = (q * q_scale).astype(BF16)
        kt_ref[hh, 0] = (_dot(nkv, wk_ref[:, cols]) + kr).T.astype(BF16)
        v_ref[:, hh * QK_WIDTH:hh * QK_WIDTH + MLA_V_DIM] = _dot(
            nkv, wv_ref[:, hh * MLA_V_DIM:(hh + 1) * MLA_V_DIM]).astype(BF16)
        v_ref[:, hh * QK_WIDTH + MLA_V_DIM:(hh + 1) * QK_WIDTH] = ones


def _mla_up(cq, ckv, gq, gkv, wqa, wqb, wk, wv, kr, cos_t, sin_t, seq, tm):
    t = cq.shape[0]
    n_pos = seq // tm
    row = lambda i: (i, 0)
    const = lambda i: (0, 0)
    pos = lambda i: (i % n_pos, 0)
    hq = MLA_HEADS * QK_WIDTH
    hv = MLA_HEADS * MLA_V_DIM
    scale = (MLA_NOPE_DIM + MLA_ROPE_DIM) ** -0.5 * LOG2E
    vmem = (4 * tm * 512 * 4 + 2 * (3 * 512 * hq * 2 + 512 * hv * 2) + 8 * tm * QK_WIDTH * 4
            + 2 * tm * 3 * hq * 2 + (8 << 20))
    return pl.pallas_call(
        functools.partial(_mlaup_kernel, q_scale=scale),
        grid=(t // tm,),
        in_specs=[pl.BlockSpec((tm, MLA_Q_RANK), row), pl.BlockSpec((tm, MLA_KV_RANK), row),
                  pl.BlockSpec((1, MLA_Q_RANK), const), pl.BlockSpec((1, MLA_KV_RANK), const),
                  pl.BlockSpec((MLA_Q_RANK, hq), const), pl.BlockSpec((MLA_Q_RANK, hq), const),
                  pl.BlockSpec((MLA_KV_RANK, hq), const), pl.BlockSpec((MLA_KV_RANK, hv), const),
                  pl.BlockSpec((tm, QK_WIDTH), row),
                  pl.BlockSpec((tm, QK_WIDTH), pos), pl.BlockSpec((tm, QK_WIDTH), pos)],
        out_specs=[pl.BlockSpec((tm, hq), row),
                   pl.BlockSpec((MLA_HEADS, 1, QK_WIDTH, tm), lambda i: (0, i, 0, 0)),
                   pl.BlockSpec((tm, hq), row)],
        out_shape=[jax.ShapeDtypeStruct((t, hq), BF16),
                   jax.ShapeDtypeStruct((MLA_HEADS, t // tm, QK_WIDTH, tm), BF16),
                   jax.ShapeDtypeStruct((t, hq), BF16)],
        compiler_params=_params(("parallel",), vmem), name="mla_up",
    )(cq, ckv, gq, gkv, wqa, wqb, wk, wv, kr, cos_t, sin_t)


def _flash_stages(bufs, q_ref, *, pv=None, softmax=None, scores=None):
    s_sc, p_sc, alpha_sc, m_sc, acc_sc = bufs
    rows = s_sc.shape[0] // FLASH_STRIPS
    for r in range(FLASH_STRIPS):
        sl = slice(r * rows, (r + 1) * rows)
        if pv is not None:
            acc_sc[sl] = alpha_sc[sl] * acc_sc[sl] + _dot(p_sc[sl], pv)
        if softmax is not None:
            s = s_sc[sl]
            if softmax is not True:
                s = jnp.where(softmax[sl], s, MASKED)
            m_old = m_sc[sl]
            m_new = jnp.maximum(m_old, jnp.max(s, axis=-1, keepdims=True))
            p_sc[sl] = jnp.exp2(s - m_new).astype(BF16)
            alpha_sc[sl] = jnp.exp2(m_old - m_new)
            m_sc[sl] = m_new
        if scores is not None:
            s_sc[sl] = _dot(q_ref[sl], scores)


def _flash_loop(i, q_ref, kt_ref, v_ref, o_ref, bufs, diag_mask, *, tile, dv):
    s_sc, p_sc, alpha_sc, m_sc, acc_sc = bufs
    m_sc[...] = jnp.full_like(m_sc, M_INIT)
    acc_sc[...] = jnp.zeros_like(acc_sc)

    def v_tile(j):
        return v_ref[pl.ds(pl.multiple_of(j * tile, tile), tile), :]

    _flash_stages(bufs, q_ref, scores=kt_ref[0, 0])

    @pl.when(i >= 1)
    def _():
        _flash_stages(bufs, q_ref, softmax=True, scores=kt_ref[0, 1])

    def steady(k, carry):
        _flash_stages(bufs, q_ref, pv=v_tile(k - 2), softmax=True, scores=kt_ref[0, k])
        return carry

    lax.fori_loop(2, i + 1, steady, 0)

    @pl.when(i >= 1)
    def _():
        _flash_stages(bufs, q_ref, pv=v_tile(i - 1), softmax=diag_mask)

    @pl.when(i == 0)
    def _():
        _flash_stages(bufs, q_ref, softmax=diag_mask)

    _flash_stages(bufs, q_ref, pv=v_tile(i))
    acc = acc_sc[...]
    o_ref[...] = acc[:, :dv] / acc[:, dv:2 * dv]


def _mla_attn_kernel(q_ref, kt_ref, v_ref, o_ref, *bufs, tile):
    row = lax.broadcasted_iota(I32, (tile, tile), 0)
    col = lax.broadcasted_iota(I32, (tile, tile), 1)
    _flash_loop(pl.program_id(2), q_ref, kt_ref, v_ref, o_ref, bufs, row >= col, tile=tile, dv=MLA_V_DIM)


def _attn_call(kernel, name, q_spec, extra_specs, scratch, args, batch, heads, seq, tile, dv):
    t = batch * seq
    nq = seq // tile
    vmem = 2 * 2 * seq * QK_WIDTH * 2 + 10 * tile * tile * 4 + 10 * tile * QK_WIDTH * 4 + (4 << 20)
    return pl.pallas_call(
        kernel,
        grid=(batch, heads, nq),
        in_specs=[q_spec,
                  pl.BlockSpec((1, nq, QK_WIDTH, tile), lambda b, h, i: (h, b, 0, 0)),
                  pl.BlockSpec((seq, QK_WIDTH), lambda b, h, i: (b, h))] + extra_specs,
        out_specs=pl.BlockSpec((tile, dv), lambda b, h, i: (b * nq + i, h)),
        out_shape=jax.ShapeDtypeStruct((t, heads * dv), F32),
        scratch_shapes=scratch + [pltpu.VMEM((tile, tile), F32), pltpu.VMEM((tile, tile), BF16),
                                  pltpu.VMEM((tile, 1), F32), pltpu.VMEM((tile, 1), F32),
                                  pltpu.VMEM((tile, QK_WIDTH), F32)],
        compiler_params=_params(("parallel", "parallel", "arbitrary"), vmem), name=name,
    )(*args)


def _mla_attn(qc, kt, v, batch, seq, tile):
    nq = seq // tile
    return _attn_call(functools.partial(_mla_attn_kernel, tile=tile), "mla_attn",
                      pl.BlockSpec((tile, QK_WIDTH), lambda b, h, i: (b * nq + i, h)), [], [],
                      (qc, kt, v), batch, MLA_HEADS, seq, tile, MLA_V_DIM)


def _moba_attn_kernel(q_ref, kt_ref, v_ref, kmt_ref, aq_ref, o_ref, qa_sc, *bufs, tile):
    u = pl.program_id(2)
    blk = MOBA_BLOCK
    q = q_ref[...]
    km = kmt_ref[0]
    km_hi = km.astype(BF16)
    km_lo = (km - km_hi.astype(F32)).astype(BF16)
    gate = _dot(q, km_hi) + _dot(q, km_lo)
    lane = lax.broadcasted_iota(I32, (tile, LANES), 1)
    own = u * (tile // blk) + lax.broadcasted_iota(I32, (tile, LANES), 0) // blk
    g = jnp.where(lane < own, gate, -jnp.inf)
    keep = lane == own
    for _ in range(MOBA_TOPK):
        mx = jnp.max(g, axis=-1, keepdims=True)
        cand = jnp.where((g == mx) & (mx > -jnp.inf), lane, LANES)
        pick = lane == jnp.min(cand, axis=-1, keepdims=True)
        keep = keep | pick
        g = jnp.where(pick, -jnp.inf, g)
    feat = jnp.where(lane < FEAT_BLOCKS, jnp.where(keep, 0.0, 1.0), aq_ref[0])
    qa_sc[:, :MOBA_HEAD_DIM] = q
    qa_sc[:, MOBA_HEAD_DIM:] = feat.astype(BF16)

    row = lax.broadcasted_iota(I32, (tile, tile), 0)
    col = lax.broadcasted_iota(I32, (tile, tile), 1)
    causal_in_own_block = (row // blk != col // blk) | (col <= row)
    _flash_loop(u, qa_sc, kt_ref, v_ref, o_ref, bufs, causal_in_own_block, tile=tile, dv=MOBA_HEAD_DIM)


def _moba_attn(qm, kt, vm, kmt, aq, batch, seq, tile):
    nq = seq // tile
    dh = MOBA_HEAD_DIM
    return _attn_call(functools.partial(_moba_attn_kernel, tile=tile), "moba_attn",
                      pl.BlockSpec((tile, dh), lambda b, h, i: (b * nq + i, h)),
                      [pl.BlockSpec((1, dh, LANES), lambda b, h, i: (b * MOBA_HEADS + h, 0, 0)),
                       pl.BlockSpec((1, 1, LANES), lambda b, h, i: (h, 0, 0))],
                      [pltpu.VMEM((tile, QK_WIDTH), BF16)],
                      (qm, kt, vm, kmt, aq), batch, MOBA_HEADS, seq, tile, dh)


def _outproj_kernel(om_ref, ol_ref, gm_ref, gl_ref, wo_ref, x_ref, gf_ref, wrh_ref, wrl_ref, br_ref,
                    x1_ref, hp_ref, lg_ref):
    wm = om_ref.shape[1]
    a = _rms(om_ref[...], gm_ref[...]).astype(BF16)
    b = _rms(ol_ref[...], gl_ref[...]).astype(BF16)
    x1 = x_ref[...] + _dot(a, wo_ref[:wm, :]) + _dot(b, wo_ref[wm:, :])
    x1_ref[...] = x1
    h2 = _rms(x1, gf_ref[...])
    half = h2.shape[1] // 2
    hp_ref[...] = _pack_halves(h2[:, :half], h2[:, half:])
    h_hi = h2.astype(BF16)
    h_lo = (h2 - h_hi.astype(F32)).astype(BF16)
    lg_ref[...] = (_dot(h_hi, wrh_ref[...]) + _dot(h_lo, wrh_ref[...]) + _dot(h_hi, wrl_ref[...])
                   + br_ref[...])


def _out_proj(om, ol, gm, gl, wo, x2, gf, wr_hi, wr_lo, br, tm):
    t, d = x2.shape
    wm = om.shape[1]
    wl = ol.shape[1]
    row = lambda i: (i, 0)
    const = lambda i: (0, 0)
    vmem = 2 * (wm + wl) * d * 2 + 2 * tm * (wm + wl + 2 * d) * 4 + 2 * tm * d * 2 + 6 * tm * d * 4 + (4 << 20)
    return pl.pallas_call(
        _outproj_kernel,
        grid=(t // tm,),
        in_specs=[pl.BlockSpec((tm, wm), row), pl.BlockSpec((tm, wl), row),
                  pl.BlockSpec((1, wm), const), pl.BlockSpec((1, wl), const),
                  pl.BlockSpec((wm + wl, d), const), pl.BlockSpec((tm, d), row), pl.BlockSpec((1, d), const),
                  pl.BlockSpec((d, LANES), const), pl.BlockSpec((d, LANES), const), pl.BlockSpec((1, LANES), const)],
        out_specs=[pl.BlockSpec((tm, d), row), pl.BlockSpec((tm, d // 2), row), pl.BlockSpec((tm, LANES), row)],
        out_shape=[jax.ShapeDtypeStruct((t, d), F32), jax.ShapeDtypeStruct((t, d // 2), U32),
                   jax.ShapeDtypeStruct((t, LANES), F32)],
        compiler_params=_params(("parallel",), vmem), name="out_proj",
    )(om, ol, gm, gl, wo, x2, gf, wr_hi, wr_lo, br)


def _router_kernel(lg_ref, idx_ref, gate_ref, pos_ref, cnt_ref, *, tm):
    i = pl.program_id(0)

    @pl.when(i == 0)
    def _():
        cnt_ref[...] = jnp.zeros_like(cnt_ref)

    lane = lax.broadcasted_iota(I32, (tm, LANES), 1)
    work = jnp.where(lane < N_EXPERTS, lg_ref[...], -jnp.inf)
    vals, firsts, picks = [], [], []
    for _ in range(TOP_K):
        mx = jnp.max(work, axis=-1, keepdims=True)
        first = jnp.min(jnp.where(work == mx, lane, LANES), axis=-1, keepdims=True)
        pick = lane == first
        work = jnp.where(pick, -jnp.inf, work)
        vals.append(mx)
        firsts.append(first)
        picks.append(pick)
    es = [jnp.exp(v - vals[0]) for v in vals]
    denom = es[0] + es[1] + es[2] + es[3]
    chosen = jnp.zeros((tm, LANES), F32)
    for p in picks:
        chosen = jnp.where(p, 1.0, chosen)
    r = lax.broadcasted_iota(I32, (tm, tm), 0)
    cc = lax.broadcasted_iota(I32, (tm, tm), 1)
    before = (r > cc).astype(BF16)
    prior = _dot(before, chosen.astype(BF16)) + cnt_ref[...]
    lane4 = lax.broadcasted_iota(I32, (tm, TOP_K), 1)
    idx = jnp.zeros((tm, TOP_K), I32)
    gate = jnp.zeros((tm, TOP_K), F32)
    pos = jnp.zeros((tm, TOP_K), I32)
    for j in range(TOP_K):
        pj = jnp.sum(jnp.where(picks[j], prior, 0.0), axis=-1, keepdims=True)
        idx = jnp.where(lane4 == j, firsts[j], idx)
        gate = jnp.where(lane4 == j, es[j] / denom, gate)
        pos = jnp.where(lane4 == j, pj.astype(I32), pos)
    idx_ref[...] = idx
    gate_ref[...] = gate
    pos_ref[...] = pos
    cnt_ref[...] += jnp.sum(chosen, axis=0, keepdims=True)


def _router(logits, tm):
    t = logits.shape[0]
    row = lambda i: (i, 0)
    vmem = 4 * tm * tm * 4 + 24 * tm * LANES * 4 + (4 << 20)
    return pl.pallas_call(
        functools.partial(_router_kernel, tm=tm),
        grid=(t // tm,),
        in_specs=[pl.BlockSpec((tm, LANES), row)],
        out_specs=[pl.BlockSpec((tm, TOP_K), row)] * 3 + [pl.BlockSpec((1, LANES), lambda i: (0, 0))],
        out_shape=[jax.ShapeDtypeStruct((t, TOP_K), I32), jax.ShapeDtypeStruct((t, TOP_K), F32),
                   jax.ShapeDtypeStruct((t, TOP_K), I32), jax.ShapeDtypeStruct((1, LANES), F32)],
        compiler_params=_params(("arbitrary",), vmem), name="router",
    )(logits)


DISPATCH_GROUP = 4


def _dispatch_kernel(src_ref, nu_ref, hp_ref, xb_ref, sem):
    i = pl.program_id(0)
    rows = xb_ref.shape[0]
    used = i * DISPATCH_GROUP < nu_ref[0]

    def row_copy(r):
        return pltpu.make_async_copy(hp_ref.at[pl.ds(src_ref[i * rows + r], 1)], xb_ref.at[pl.ds(r, 1)], sem)

    @pl.when(used)
    def _():
        def start(r, carry):
            row_copy(r).start()
            return carry

        def wait(r, carry):
            row_copy(r).wait()
            return carry

        lax.fori_loop(0, rows, start, 0, unroll=8)
        lax.fori_loop(0, rows, wait, 0, unroll=8)

    @pl.when(jnp.logical_not(used))
    def _():
        xb_ref[...] = jnp.zeros_like(xb_ref)


def _dispatch(src, n_used, hp, n_rows):
    t, half = hp.shape
    rows = EXPERT_BLOCK * DISPATCH_GROUP
    assert n_rows % rows == 0
    grid_spec = pltpu.PrefetchScalarGridSpec(
        num_scalar_prefetch=2, grid=(n_rows // rows,),
        in_specs=[pl.BlockSpec(memory_space=pl.ANY)],
        out_specs=pl.BlockSpec((rows, half), lambda i, src, nu: (i, 0)),
        scratch_shapes=[pltpu.SemaphoreType.DMA])
    return pl.pallas_call(
        _dispatch_kernel,
        grid_spec=grid_spec,
        out_shape=jax.ShapeDtypeStruct((n_rows, half), U32),
        compiler_params=_params(("arbitrary",), 3 * rows * half * 4 + (4 << 20)),
        name="dispatch",
    )(src, n_used, hp)


def _moe_up_kernel(be_ref, nu_ref, x_ref, w1_ref, b1_ref, act_ref, *, fc):
    i = pl.program_id(0)

    @pl.when(i < nu_ref[0])
    def _():
        hi, lo = _unpack_halves(x_ref[...])
        a = hi.astype(BF16)
        b = lo.astype(BF16)
        half = a.shape[1]
        d_ff = act_ref.shape[1]

        def pre(c0):
            return (_dot(a, w1_ref[0, :half, c0:c0 + fc]) + _dot(b, w1_ref[0, half:, c0:c0 + fc])
                    + b1_ref[0, :, c0:c0 + fc])

        for c in range(d_ff // fc):
            glu = jnp.minimum(pre(c * fc), SWIGLU_LIMIT)
            lin = jnp.clip(pre(d_ff + c * fc), -SWIGLU_LIMIT, SWIGLU_LIMIT)
            act = glu * jax.nn.sigmoid(SWIGLU_ALPHA * glu) * (lin + 1.0)
            act_ref[:, c * fc:(c + 1) * fc] = act.astype(BF16)

    @pl.when(i >= nu_ref[0])
    def _():
        act_ref[...] = jnp.zeros_like(act_ref)


def _moe_up(block_expert, n_used, xb, w1, b1):
    n_rows, half = xb.shape
    e, d, f2 = w1.shape
    d_ff = f2 // 2
    blk = EXPERT_BLOCK
    n_blocks = n_rows // blk
    fc = 512
    live = lambda i, be, nu: jnp.minimum(i, nu[0] - 1)
    grid_spec = pltpu.PrefetchScalarGridSpec(
        num_scalar_prefetch=2, grid=(n_blocks,),
        in_specs=[pl.BlockSpec((blk, half), lambda i, be, nu: (live(i, be, nu), 0)),
                  pl.BlockSpec((1, d, f2), lambda i, be, nu: (be[live(i, be, nu)], 0, 0)),
                  pl.BlockSpec((1, 1, f2), lambda i, be, nu: (be[live(i, be, nu)], 0, 0))],
        out_specs=pl.BlockSpec((blk, d_ff), lambda i, be, nu: (i, 0)))
    vmem = 2 * d * f2 * 2 + 2 * blk * half * 4 + 2 * blk * d_ff * 2 + 8 * blk * fc * 4 + 2 * blk * d * 2 + (4 << 20)
    return pl.pallas_call(
        functools.partial(_moe_up_kernel, fc=fc),
        grid_spec=grid_spec,
        out_shape=jax.ShapeDtypeStruct((n_rows, d_ff), BF16),
        compiler_params=_params(("arbitrary",), vmem), name="moe_up",
    )(block_expert, n_used, xb, w1, b1)


def _moe_down_kernel(be_ref, nu_ref, a_ref, w2_ref, b2_ref, y_ref):
    i = pl.program_id(0)

    @pl.when(i < nu_ref[0])
    def _():
        y = _dot(a_ref[...], w2_ref[0]) + b2_ref[0]
        half = y.shape[1] // 2
        y_ref[...] = _pack_halves(y[:, :half], y[:, half:])

    @pl.when(i >= nu_ref[0])
    def _():
        y_ref[...] = jnp.zeros_like(y_ref)


def _moe_down(block_expert, n_used, act, w2, b2):
    n_rows, d_ff = act.shape
    e, _, d = w2.shape
    blk = EXPERT_BLOCK
    n_blocks = n_rows // blk
    live = lambda i, be, nu: jnp.minimum(i, nu[0] - 1)
    grid_spec = pltpu.PrefetchScalarGridSpec(
        num_scalar_prefetch=2, grid=(n_blocks,),
        in_specs=[pl.BlockSpec((blk, d_ff), lambda i, be, nu: (live(i, be, nu), 0)),
                  pl.BlockSpec((1, d_ff, d), lambda i, be, nu: (be[live(i, be, nu)], 0, 0)),
                  pl.BlockSpec((1, 1, d), lambda i, be, nu: (be[live(i, be, nu)], 0, 0))],
        out_specs=pl.BlockSpec((blk, d // 2), lambda i, be, nu: (i, 0)))
    vmem = 2 * d_ff * d * 2 + 2 * blk * d_ff * 2 + 2 * blk * d * 2 + 4 * blk * d * 4 + (4 << 20)
    return pl.pallas_call(
        _moe_down_kernel,
        grid_spec=grid_spec,
        out_shape=jax.ShapeDtypeStruct((n_rows, d // 2), U32),
        compiler_params=_params(("arbitrary",), vmem), name="moe_down",
    )(block_expert, n_used, act, w2, b2)


def _combine_kernel(dest_ref, gate_ref, x1_ref, gfin_ref, yb_ref, o_ref, rows_sc, sem, *, tm):
    i = pl.program_id(0)
    base = i * tm

    def row_copy(r, j):
        return pltpu.make_async_copy(yb_ref.at[pl.ds(dest_ref[(base + r) * TOP_K + j], 1)],
                                     rows_sc.at[j, pl.ds(r, 1)], sem)

    def start(r, carry):
        for j in range(TOP_K):
            row_copy(r, j).start()
        return carry

    def wait(r, carry):
        for j in range(TOP_K):
            row_copy(r, j).wait()
        return carry

    lax.fori_loop(0, tm, start, 0, unroll=4)
    lax.fori_loop(0, tm, wait, 0, unroll=4)

    x1 = x1_ref[...]
    half = x1.shape[1] // 2
    gates = gate_ref[...]
    left = x1[:, :half]
    right = x1[:, half:]
    for j in range(TOP_K):
        hi, lo = _unpack_halves(rows_sc[j])
        gj = gates[:, j:j + 1]
        left = left + gj * hi
        right = right + gj * lo
    ms = (jnp.sum(left * left, axis=-1, keepdims=True) + jnp.sum(right * right, axis=-1, keepdims=True)) / (2 * half)
    inv = lax.rsqrt(ms + NORM_EPS)
    g = gfin_ref[...]
    o_ref[:, :half] = left * inv * g[:, :half]
    o_ref[:, half:] = right * inv * g[:, half:]


def _combine(dest, gates, x1, gfin, yb, tm):
    t, d = x1.shape
    half = d // 2
    grid_spec = pltpu.PrefetchScalarGridSpec(
        num_scalar_prefetch=1, grid=(t // tm,),
        in_specs=[pl.BlockSpec((tm, TOP_K), lambda i, dst: (i, 0)),
                  pl.BlockSpec((tm, d), lambda i, dst: (i, 0)),
                  pl.BlockSpec((1, d), lambda i, dst: (0, 0)),
                  pl.BlockSpec(memory_space=pl.ANY)],
        out_specs=pl.BlockSpec((tm, d), lambda i, dst: (i, 0)),
        scratch_shapes=[pltpu.VMEM((TOP_K, tm, half), U32), pltpu.SemaphoreType.DMA])
    vmem = TOP_K * tm * half * 4 + 4 * tm * d * 4 + 6 * tm * d * 4 + 2 * tm * LANES * 4 + (4 << 20)
    return pl.pallas_call(
        functools.partial(_combine_kernel, tm=tm),
        grid_spec=grid_spec,
        out_shape=jax.ShapeDtypeStruct((t, d), F32),
        compiler_params=_params(("arbitrary",), vmem), name="combine",
    )(dest, gates, x1, gfin, yb)


def _rope_tables(seq):
    inv = 1.0 / (ROPE_THETA ** (jnp.arange(0, MLA_ROPE_DIM, 2, dtype=F32) / MLA_ROPE_DIM))
    ang = jnp.arange(seq, dtype=F32)[:, None] * inv[None, :]
    ang = jnp.concatenate([ang, ang], axis=-1)
    ones = jnp.ones((seq, MLA_NOPE_DIM), F32)
    zeros = jnp.zeros((seq, QK_WIDTH - MLA_NOPE_DIM - MLA_ROPE_DIM), F32)
    cos_t = jnp.concatenate([ones, jnp.cos(ang), zeros], axis=-1)
    sin_t = jnp.concatenate([0.0 * ones, jnp.sin(ang), zeros], axis=-1)
    return cos_t, sin_t


def _bf16_pieces(x, n=3):
    out = []
    for _ in range(n):
        p = float(np.asarray(x, dtype=np.float32).astype(jnp.bfloat16).astype(np.float32))
        out.append(p)
        x = x - p
    return out


def _moba_features(seq, tile):
    pos = np.arange(seq)
    blk, off = pos // MOBA_BLOCK, pos % MOBA_BLOCK
    assert seq // MOBA_BLOCK <= FEAT_BLOCKS
    feat = np.zeros((QK_WIDTH - MOBA_HEAD_DIM, seq), np.float32)
    feat[:FEAT_BLOCKS] = np.where(blk[None, :] == np.arange(FEAT_BLOCKS)[:, None], MASKED, 0.0)
    feat[FEAT_ALIBI:FEAT_ALIBI + 3] = blk[None, :]
    feat[FEAT_ALIBI + 3:FEAT_ALIBI + 6] = off[None, :]
    feat = feat.reshape(feat.shape[0], seq // tile, tile).transpose(1, 0, 2)
    aq = np.zeros((MOBA_HEADS, 1, LANES), np.float32)
    for h in range(MOBA_HEADS):
        slope = 2.0 ** (-8.0 * (h + 1) / MOBA_HEADS)
        aq[h, 0, FEAT_ALIBI:FEAT_ALIBI + 3] = _bf16_pieces(slope * LOG2E * MOBA_BLOCK)
        aq[h, 0, FEAT_ALIBI + 3:FEAT_ALIBI + 6] = _bf16_pieces(slope * LOG2E)
    return jnp.asarray(feat, BF16), jnp.asarray(aq, F32)


def _rot_cols(w):
    half = w.shape[-1] // 2
    return jnp.concatenate([-w[..., half:], w[..., :half]], axis=-1)


def _rope_slot(w):
    k = w.shape[0]
    return jnp.concatenate([jnp.zeros((k, MLA_NOPE_DIM), w.dtype), w,
                            jnp.zeros((k, QK_WIDTH - MLA_NOPE_DIM - MLA_ROPE_DIM), w.dtype)], axis=-1)


def _layer(x2, batch, seq, norm_attn_g, w_in, moba_out_g, q_a_norm_g, kv_a_norm_g, w_uq, w_ukv, mla_out_g,
           w_o, norm_ffn_g, w_router, b_router, w1, b1, w2, b2, final_g):
    t, d = x2.shape
    wm = MOBA_HEADS * MOBA_HEAD_DIM
    tile = ATT_TILE
    assert seq % tile == 0 and tile % MOBA_BLOCK == 0
    cos_t, sin_t = _rope_tables(seq)
    feat, aq = _moba_features(seq, tile)

    c_kr = 3 * wm + MLA_Q_RANK + MLA_KV_RANK
    w_kr = w_in[:, c_kr:c_kr + MLA_ROPE_DIM]
    w_all = jnp.concatenate([w_in[:, :c_kr], _rope_slot(w_kr), _rope_slot(_rot_cols(w_kr))], axis=1).astype(BF16)
    wq = w_uq.reshape(MLA_Q_RANK, MLA_HEADS, MLA_NOPE_DIM + MLA_ROPE_DIM)
    zq = jnp.zeros((MLA_Q_RANK, MLA_HEADS, QK_WIDTH - MLA_NOPE_DIM - MLA_ROPE_DIM), F32)
    wqa = jnp.concatenate([wq, zq], axis=-1).reshape(MLA_Q_RANK, -1).astype(BF16)
    wqb = jnp.concatenate([jnp.zeros((MLA_Q_RANK, MLA_HEADS, MLA_NOPE_DIM), F32),
                           _rot_cols(wq[..., MLA_NOPE_DIM:]), zq], axis=-1).reshape(MLA_Q_RANK, -1).astype(BF16)
    wkv = w_ukv.reshape(MLA_KV_RANK, MLA_HEADS, MLA_NOPE_DIM + MLA_V_DIM)
    wk = jnp.concatenate([wkv[..., :MLA_NOPE_DIM],
                          jnp.zeros((MLA_KV_RANK, MLA_HEADS, QK_WIDTH - MLA_NOPE_DIM), F32)],
                         axis=-1).reshape(MLA_KV_RANK, -1).astype(BF16)
    wv = wkv[..., MLA_NOPE_DIM:].reshape(MLA_KV_RANK, -1).astype(BF16)
    wr = jnp.pad(w_router, ((0, 0), (0, LANES - N_EXPERTS)))
    wr_hi = wr.astype(BF16)
    wr_lo = (wr - wr_hi.astype(F32)).astype(BF16)
    br = jnp.pad(b_router, (0, LANES - N_EXPERTS))[None, :]

    qm, ktm, vm, kmean, cq, ckv, kr = _in_proj(x2, norm_attn_g[None, :], w_all, cos_t, sin_t, feat, seq, tm=tile)
    qc, ktl, vl = _mla_up(cq, ckv, q_a_norm_g[None, :], kv_a_norm_g[None, :], wqa, wqb, wk, wv, kr,
                          cos_t, sin_t, seq, tm=tile)
    n_blk = seq // MOBA_BLOCK
    kmt = kmean.reshape(batch, n_blk, MOBA_HEADS, MOBA_HEAD_DIM).transpose(0, 2, 3, 1)
    kmt = jnp.pad(kmt, ((0, 0), (0, 0), (0, 0), (0, LANES - n_blk))).reshape(batch * MOBA_HEADS, MOBA_HEAD_DIM, LANES)
    o_moba = _moba_attn(qm, ktm, vm, kmt, aq, batch, seq, tile)
    o_mla = _mla_attn(qc, ktl, vl, batch, seq, tile)
    x1, hp, logits = _out_proj(o_moba, o_mla, moba_out_g[None, :], mla_out_g[None, :], w_o.astype(BF16), x2,
                               norm_ffn_g[None, :], wr_hi, wr_lo, br, tm=256)

    idx, gates, pos, counts = _router(logits, tm=min(512, t))
    blk = EXPERT_BLOCK
    n_blocks = -(-(t * TOP_K) // blk) + N_EXPERTS
    counts = counts[0, :N_EXPERTS].astype(I32)
    padded = ((counts + blk - 1) // blk) * blk
    pad_end = jnp.cumsum(padded)
    pad_start = pad_end - padded
    dest = (pad_start[idx] + pos).reshape(-1)
    block_expert = jnp.minimum(
        jnp.sum((jnp.arange(n_blocks, dtype=I32) * blk)[:, None] >= pad_end[None, :], axis=1), N_EXPERTS - 1).astype(I32)
    n_used = (pad_end[-1:] // blk).astype(I32)
    src = jnp.zeros((n_blocks * blk,), I32).at[dest].set(jnp.arange(t * TOP_K, dtype=I32) // TOP_K,
                                                          unique_indices=True)

    xb = _dispatch(src, n_used, hp, n_blocks * blk)
    act = _moe_up(block_expert, n_used, xb, w1.astype(BF16), b1[:, None, :])
    yb = _moe_down(block_expert, n_used, act, w2.astype(BF16), b2[:, None, :])
    return _combine(dest, gates, x1, final_g[None, :], yb, tm=256)


def kernel(x, norm_attn_g, w_in, moba_out_g, q_a_norm_g, kv_a_norm_g, w_uq, w_ukv, mla_out_g, w_o, norm_ffn_g,
           w_router, b_router, w1, b1, w2, b2, norm_final_g):
    batch, seq, d = x.shape
    depth = w_in.shape[0]
    assert depth == 1, "the final norm is fused into the last layer's combine step"
    x2 = x.reshape(batch * seq, d)
    out = _layer(x2, batch, seq, norm_attn_g[0], w_in[0], moba_out_g[0], q_a_norm_g[0], kv_a_norm_g[0], w_uq[0],
                 w_ukv[0], mla_out_g[0], w_o[0], norm_ffn_g[0], w_router[0], b_router[0], w1[0], b1[0], w2[0],
                 b2[0], norm_final_g)
    return out.reshape(batch, seq, d)
```

```python
import functools

import numpy as np
import jax
import jax.numpy as jnp
from jax import lax
from jax.experimental import pallas as pl
from jax.experimental.pallas import tpu as pltpu

MOBA_HEADS = 8
MOBA_HEAD_DIM = 128
MOBA_BLOCK = 256
MOBA_TOPK = 3
MLA_HEADS = 8
MLA_NOPE_DIM = 128
MLA_ROPE_DIM = 64
MLA_V_DIM = 128
MLA_Q_RANK = 512
MLA_KV_RANK = 512
ROPE_THETA = 10000.0
N_EXPERTS = 32
TOP_K = 4
SWIGLU_LIMIT = 7.0
SWIGLU_ALPHA = 1.702
EXPERT_BLOCK = 256
NORM_EPS = 1e-5

LANES = 128
QK_WIDTH = 256
ATT_TILE = 512
FLASH_STRIPS = 2
LOG2E = 1.4426950408889634
MASKED = -2.0 ** 100
M_INIT = -2.0 ** 98
FEAT_BLOCKS = 64
FEAT_ALIBI = FEAT_BLOCKS
V7X_VMEM_BUDGET = 56 * 1024 * 1024

F32 = jnp.float32
BF16 = jnp.bfloat16
U32 = jnp.uint32
I32 = jnp.int32


def _params(semantics, vmem_bytes):
    return pltpu.CompilerParams(dimension_semantics=semantics,
                                vmem_limit_bytes=min(int(vmem_bytes), V7X_VMEM_BUDGET))


def _rms(xf, g):
    ms = jnp.mean(xf * xf, axis=-1, keepdims=True)
    return xf * lax.rsqrt(ms + NORM_EPS) * g


def _dot(a, b):
    return jnp.dot(a, b, preferred_element_type=F32)


def _pack_halves(a, b):
    ai = lax.bitcast_convert_type(a.astype(BF16).astype(F32), U32)
    bi = lax.bitcast_convert_type(b.astype(BF16).astype(F32), U32)
    return ai | (bi >> 16)


def _unpack_halves(w):
    hi = lax.bitcast_convert_type(w & jnp.uint32(0xFFFF0000), F32)
    lo = lax.bitcast_convert_type(w << 16, F32)
    return hi, lo


def _inproj_kernel(x_ref, g_ref, w_ref, cos_ref, sin_ref, feat_ref,
                   qm_ref, kt_ref, vm_ref, kmean_ref, cq_ref, ckv_ref, kr_ref, *, q_scale):
    h = _rms(x_ref[...], g_ref[...]).astype(BF16)
    w = MOBA_HEADS * MOBA_HEAD_DIM
    dh = MOBA_HEAD_DIM
    tm = h.shape[0]

    def mm(lo, hi):
        return _dot(h, w_ref[:, lo:hi])

    qm_ref[...] = (mm(0, w) * q_scale).astype(BF16)
    k = mm(w, 2 * w)
    for b in range(tm // MOBA_BLOCK):
        kmean_ref[b] = jnp.mean(k[b * MOBA_BLOCK:(b + 1) * MOBA_BLOCK, :], axis=0, keepdims=True)
    v = mm(2 * w, 3 * w).astype(BF16)
    ones = jnp.ones((tm, QK_WIDTH - dh), BF16)
    for hh in range(MOBA_HEADS):
        kt_ref[hh, 0, :dh, :] = k[:, hh * dh:(hh + 1) * dh].T.astype(BF16)
        kt_ref[hh, 0, dh:, :] = feat_ref[0]
        vm_ref[:, hh * QK_WIDTH:hh * QK_WIDTH + dh] = v[:, hh * dh:(hh + 1) * dh]
        vm_ref[:, hh * QK_WIDTH + dh:(hh + 1) * QK_WIDTH] = ones
    c0 = 3 * w
    cq_ref[...] = mm(c0, c0 + MLA_Q_RANK)
    c1 = c0 + MLA_Q_RANK
    ckv_ref[...] = mm(c1, c1 + MLA_KV_RANK)
    c2 = c1 + MLA_KV_RANK
    kr = mm(c2, c2 + QK_WIDTH) * cos_ref[...] + mm(c2 + QK_WIDTH, c2 + 2 * QK_WIDTH) * sin_ref[...]
    kr_ref[...] = kr.astype(BF16)


def _in_proj(x2, g, w_all, cos_t, sin_t, feat, seq, tm):
    t, d = x2.shape
    w = MOBA_HEADS * MOBA_HEAD_DIM
    hw = MOBA_HEADS * QK_WIDTH
    nc = w_all.shape[1]
    n_pos = seq // tm
    nb = tm // MOBA_BLOCK
    row = lambda i: (i, 0)
    const = lambda i: (0, 0)
    pos = lambda i: (i % n_pos, 0)
    out_shape = [jax.ShapeDtypeStruct((t, w), BF16),
                 jax.ShapeDtypeStruct((MOBA_HEADS, t // tm, QK_WIDTH, tm), BF16),
                 jax.ShapeDtypeStruct((t, hw), BF16),
                 jax.ShapeDtypeStruct((t // MOBA_BLOCK, 1, w), F32),
                 jax.ShapeDtypeStruct((t, MLA_Q_RANK), F32), jax.ShapeDtypeStruct((t, MLA_KV_RANK), F32),
                 jax.ShapeDtypeStruct((t, QK_WIDTH), BF16)]
    out_specs = [pl.BlockSpec((tm, w), row),
                 pl.BlockSpec((MOBA_HEADS, 1, QK_WIDTH, tm), lambda i: (0, i, 0, 0)),
                 pl.BlockSpec((tm, hw), row),
                 pl.BlockSpec((nb, 1, w), lambda i: (i, 0, 0)),
                 pl.BlockSpec((tm, MLA_Q_RANK), row), pl.BlockSpec((tm, MLA_KV_RANK), row),
                 pl.BlockSpec((tm, QK_WIDTH), row)]
    vmem = (2 * tm * d * 4 + d * nc * 2 + 2 * tm * (w * 2 + 2 * hw * 2 + 2 * 512 * 4 + 256 * 2)
            + 4 * tm * QK_WIDTH * 4 + tm * d * 2 + 3 * tm * w * 4 + (4 << 20))
    return pl.pallas_call(
        functools.partial(_inproj_kernel, q_scale=MOBA_HEAD_DIM ** -0.5 * LOG2E),
        grid=(t // tm,),
        in_specs=[pl.BlockSpec((tm, d), row), pl.BlockSpec((1, d), const),
                  pl.BlockSpec((d, nc), const, pipeline_mode=pl.Buffered(1)),
                  pl.BlockSpec((tm, QK_WIDTH), pos), pl.BlockSpec((tm, QK_WIDTH), pos),
                  pl.BlockSpec((1, QK_WIDTH - MOBA_HEAD_DIM, tm), lambda i: (i % n_pos, 0, 0))],
        out_specs=out_specs, out_shape=out_shape,
        compiler_params=_params(("parallel",), vmem), name="in_proj",
    )(x2, g, w_all, cos_t, sin_t, feat)


def _mlaup_kernel(cq_ref, ckv_ref, gq_ref, gkv_ref, wqa_ref, wqb_ref, wk_ref, wv_ref,
                  kr_ref, cos_ref, sin_ref, q_ref, kt_ref, v_ref, *, q_scale):
    nq = _rms(cq_ref[...], gq_ref[...]).astype(BF16)
    nkv = _rms(ckv_ref[...], gkv_ref[...]).astype(BF16)
    cos = cos_ref[...]
    sin = sin_ref[...]
    kr = kr_ref[...].astype(F32)
    ones = jnp.ones((nq.shape[0], QK_WIDTH - MLA_V_DIM), BF16)
    for hh in range(MLA_HEADS):
        cols = slice(hh * QK_WIDTH, (hh + 1) * QK_WIDTH)
        q = _dot(nq, wqa_ref[:, cols]) * cos + _dot(nq, wqb_ref[:, cols]) * sin
        q_ref[:, cols] = (q * q_scale).astype(BF16)
        kt_ref[hh, 0] = (_dot(nkv, wk_ref[:, cols]) + kr).T.astype(BF16)
        v_ref[:, hh * QK_WIDTH:hh * QK_WIDTH + MLA_V_DIM] = _dot(
            nkv, wv_ref[:, hh * MLA_V_DIM:(hh + 1) * MLA_V_DIM]).astype(BF16)
        v_ref[:, hh * QK_WIDTH + MLA_V_DIM:(hh + 1) * QK_WIDTH] = ones


def _mla_up(cq, ckv, gq, gkv, wqa, wqb, wk, wv, kr, cos_t, sin_t, seq, tm):
    t = cq.shape[0]
    n_pos = seq // tm
    row = lambda i: (i, 0)
    const = lambda i: (0, 0)
    pos = lambda i: (i % n_pos, 0)
    hq = MLA_HEADS * QK_WIDTH
    hv = MLA_HEADS * MLA_V_DIM
    scale = (MLA_NOPE_DIM + MLA_ROPE_DIM) ** -0.5 * LOG2E
    vmem = (4 * tm * 512 * 4 + 2 * (3 * 512 * hq * 2 + 512 * hv * 2) + 8 * tm * QK_WIDTH * 4
            + 2 * tm * 3 * hq * 2 + (8 << 20))
    return pl.pallas_call(
        functools.partial(_mlaup_kernel, q_scale=scale),
        grid=(t // tm,),
        in_specs=[pl.BlockSpec((tm, MLA_Q_RANK), row), pl.BlockSpec((tm, MLA_KV_RANK), row),
                  pl.BlockSpec((1, MLA_Q_RANK), const), pl.BlockSpec((1, MLA_KV_RANK), const),
                  pl.BlockSpec((MLA_Q_RANK, hq), const), pl.BlockSpec((MLA_Q_RANK, hq), const),
                  pl.BlockSpec((MLA_KV_RANK, hq), const), pl.BlockSpec((MLA_KV_RANK, hv), const),
                  pl.BlockSpec((tm, QK_WIDTH), row),
                  pl.BlockSpec((tm, QK_WIDTH), pos), pl.BlockSpec((tm, QK_WIDTH), pos)],
        out_specs=[pl.BlockSpec((tm, hq), row),
                   pl.BlockSpec((MLA_HEADS, 1, QK_WIDTH, tm), lambda i: (0, i, 0, 0)),
                   pl.BlockSpec((tm, hq), row)],
        out_shape=[jax.ShapeDtypeStruct((t, hq), BF16),
                   jax.ShapeDtypeStruct((MLA_HEADS, t // tm, QK_WIDTH, tm), BF16),
                   jax.ShapeDtypeStruct((t, hq), BF16)],
        compiler_params=_params(("parallel",), vmem), name="mla_up",
    )(cq, ckv, gq, gkv, wqa, wqb, wk, wv, kr, cos_t, sin_t)


def _flash_stages(bufs, q_ref, *, pv=None, softmax=None, scores=None):
    s_sc, p_sc, alpha_sc, m_sc, acc_sc = bufs
    rows = s_sc.shape[0] // FLASH_STRIPS
    for r in range(FLASH_STRIPS):
        sl = slice(r * rows, (r + 1) * rows)
        if pv is not None:
            acc_sc[sl] = alpha_sc[sl] * acc_sc[sl] + _dot(p_sc[sl], pv)
        if softmax is not None:
            s = s_sc[sl]
            if softmax is not True:
                s = jnp.where(softmax[sl], s, MASKED)
            m_old = m_sc[sl]
            m_new = jnp.maximum(m_old, jnp.max(s, axis=-1, keepdims=True))
            p_sc[sl] = jnp.exp2(s - m_new).astype(BF16)
            alpha_sc[sl] = jnp.exp2(m_old - m_new)
            m_sc[sl] = m_new
        if scores is not None:
            s_sc[sl] = _dot(q_ref[sl], scores)


def _flash_loop(i, q_ref, kt_ref, v_ref, o_ref, bufs, diag_mask, *, tile, dv):
    s_sc, p_sc, alpha_sc, m_sc, acc_sc = bufs
    m_sc[...] = jnp.full_like(m_sc, M_INIT)
    acc_sc[...] = jnp.zeros_like(acc_sc)

    def v_tile(j):
        return v_ref[pl.ds(pl.multiple_of(j * tile, tile), tile), :]

    _flash_stages(bufs, q_ref, scores=kt_ref[0, 0])

    @pl.when(i >= 1)
    def _():
        _flash_stages(bufs, q_ref, softmax=True, scores=kt_ref[0, 1])

    def steady(k, carry):
        _flash_stages(bufs, q_ref, pv=v_tile(k - 2), softmax=True, scores=kt_ref[0, k])
        return carry

    lax.fori_loop(2, i + 1, steady, 0)

    @pl.when(i >= 1)
    def _():
        _flash_stages(bufs, q_ref, pv=v_tile(i - 1), softmax=diag_mask)

    @pl.when(i == 0)
    def _():
        _flash_stages(bufs, q_ref, softmax=diag_mask)

    _flash_stages(bufs, q_ref, pv=v_tile(i))
    acc = acc_sc[...]
    o_ref[...] = acc[:, :dv] / acc[:, dv:2 * dv]


def _mla_attn_kernel(q_ref, kt_ref, v_ref, w_ref, o_ref, wb_ref, *bufs, tile):
    wb_ref[...] = w_ref[...].astype(BF16)
    row = lax.broadcasted_iota(I32, (tile, tile), 0)
    col = lax.broadcasted_iota(I32, (tile, tile), 1)
    _flash_loop(pl.program_id(2), q_ref, kt_ref, v_ref, o_ref, bufs, row >= col, tile=tile, dv=MLA_V_DIM)


def _attn_call(kernel, name, q_spec, extra_specs, scratch, args, w_f32, batch, heads, seq, tile, dv):
    t = batch * seq
    nq = seq // tile
    steps = batch * heads * nq
    w2d = w_f32.reshape(-1, w_f32.shape[-1])
    assert w2d.shape[0] % steps == 0
    w_rows = w2d.shape[0] // steps
    w_spec = pl.BlockSpec((w_rows, w2d.shape[1]), lambda b, h, i: ((b * heads + h) * nq + i, 0))
    vmem = (2 * 2 * seq * QK_WIDTH * 2 + 10 * tile * tile * 4 + 10 * tile * QK_WIDTH * 4
            + 3 * w_rows * w2d.shape[1] * 6 + (4 << 20))
    out, wb = pl.pallas_call(
        kernel,
        grid=(batch, heads, nq),
        in_specs=[q_spec,
                  pl.BlockSpec((1, nq, QK_WIDTH, tile), lambda b, h, i: (h, b, 0, 0)),
                  pl.BlockSpec((seq, QK_WIDTH), lambda b, h, i: (b, h))] + extra_specs + [w_spec],
        out_specs=[pl.BlockSpec((tile, dv), lambda b, h, i: (b * nq + i, h)), w_spec],
        out_shape=[jax.ShapeDtypeStruct((t, heads * dv), F32), jax.ShapeDtypeStruct(w2d.shape, BF16)],
        scratch_shapes=scratch + [pltpu.VMEM((tile, tile), F32), pltpu.VMEM((tile, tile), BF16),
                                  pltpu.VMEM((tile, 1), F32), pltpu.VMEM((tile, 1), F32),
                                  pltpu.VMEM((tile, QK_WIDTH), F32)],
        compiler_params=_params(("parallel", "parallel", "arbitrary"), vmem), name=name,
    )(*args, w2d)
    return out, wb.reshape(w_f32.shape)


def _mla_attn(qc, kt, v, w_f32, batch, seq, tile):
    nq = seq // tile
    return _attn_call(functools.partial(_mla_attn_kernel, tile=tile), "mla_attn",
                      pl.BlockSpec((tile, QK_WIDTH), lambda b, h, i: (b * nq + i, h)), [], [],
                      (qc, kt, v), w_f32, batch, MLA_HEADS, seq, tile, MLA_V_DIM)


def _moba_attn_kernel(q_ref, kt_ref, v_ref, kmt_ref, aq_ref, w_ref, o_ref, wb_ref, qa_sc, *bufs, tile):
    wb_ref[...] = w_ref[...].astype(BF16)
    u = pl.program_id(2)
    blk = MOBA_BLOCK
    q = q_ref[...]
    km = kmt_ref[0]
    km_hi = km.astype(BF16)
    km_lo = (km - km_hi.astype(F32)).astype(BF16)
    gate = _dot(q, km_hi) + _dot(q, km_lo)
    lane = lax.broadcasted_iota(I32, (tile, LANES), 1)
    own = u * (tile // blk) + lax.broadcasted_iota(I32, (tile, LANES), 0) // blk
    g = jnp.where(lane < own, gate, -jnp.inf)
    keep = lane == own
    for _ in range(MOBA_TOPK):
        mx = jnp.max(g, axis=-1, keepdims=True)
        cand = jnp.where((g == mx) & (mx > -jnp.inf), lane, LANES)
        pick = lane == jnp.min(cand, axis=-1, keepdims=True)
        keep = keep | pick
        g = jnp.where(pick, -jnp.inf, g)
    feat = jnp.where(lane < FEAT_BLOCKS, jnp.where(keep, 0.0, 1.0), aq_ref[0])
    qa_sc[:, :MOBA_HEAD_DIM] = q
    qa_sc[:, MOBA_HEAD_DIM:] = feat.astype(BF16)

    row = lax.broadcasted_iota(I32, (tile, tile), 0)
    col = lax.broadcasted_iota(I32, (tile, tile), 1)
    causal_in_own_block = (row // blk != col // blk) | (col <= row)
    _flash_loop(u, qa_sc, kt_ref, v_ref, o_ref, bufs, causal_in_own_block, tile=tile, dv=MOBA_HEAD_DIM)


def _moba_attn(qm, kt, vm, kmt, aq, w_f32, batch, seq, tile):
    nq = seq // tile
    dh = MOBA_HEAD_DIM
    return _attn_call(functools.partial(_moba_attn_kernel, tile=tile), "moba_attn",
                      pl.BlockSpec((tile, dh), lambda b, h, i: (b * nq + i, h)),
                      [pl.BlockSpec((1, dh, LANES), lambda b, h, i: (b * MOBA_HEADS + h, 0, 0)),
                       pl.BlockSpec((1, 1, LANES), lambda b, h, i: (h, 0, 0))],
                      [pltpu.VMEM((tile, QK_WIDTH), BF16)],
                      (qm, kt, vm, kmt, aq), w_f32, batch, MOBA_HEADS, seq, tile, dh)


def _outproj_kernel(om_ref, ol_ref, gm_ref, gl_ref, wo_ref, x_ref, gf_ref, wrh_ref, wrl_ref, br_ref,
                    x1_ref, hp_ref, lg_ref):
    wm = om_ref.shape[1]
    a = _rms(om_ref[...], gm_ref[...]).astype(BF16)
    b = _rms(ol_ref[...], gl_ref[...]).astype(BF16)
    x1 = x_ref[...] + _dot(a, wo_ref[:wm, :]) + _dot(b, wo_ref[wm:, :])
    x1_ref[...] = x1
    h2 = _rms(x1, gf_ref[...])
    half = h2.shape[1] // 2
    hp_ref[...] = _pack_halves(h2[:, :half], h2[:, half:])
    h_hi = h2.astype(BF16)
    h_lo = (h2 - h_hi.astype(F32)).astype(BF16)
    lg_ref[...] = (_dot(h_hi, wrh_ref[...]) + _dot(h_lo, wrh_ref[...]) + _dot(h_hi, wrl_ref[...])
                   + br_ref[...])


def _out_proj(om, ol, gm, gl, wo, x2, gf, wr_hi, wr_lo, br, tm):
    t, d = x2.shape
    wm = om.shape[1]
    wl = ol.shape[1]
    row = lambda i: (i, 0)
    const = lambda i: (0, 0)
    vmem = 2 * (wm + wl) * d * 2 + 2 * tm * (wm + wl + 2 * d) * 4 + 2 * tm * d * 2 + 6 * tm * d * 4 + (4 << 20)
    return pl.pallas_call(
        _outproj_kernel,
        grid=(t // tm,),
        in_specs=[pl.BlockSpec((tm, wm), row), pl.BlockSpec((tm, wl), row),
                  pl.BlockSpec((1, wm), const), pl.BlockSpec((1, wl), const),
                  pl.BlockSpec((wm + wl, d), const), pl.BlockSpec((tm, d), row), pl.BlockSpec((1, d), const),
                  pl.BlockSpec((d, LANES), const), pl.BlockSpec((d, LANES), const), pl.BlockSpec((1, LANES), const)],
        out_specs=[pl.BlockSpec((tm, d), row), pl.BlockSpec((tm, d // 2), row), pl.BlockSpec((tm, LANES), row)],
        out_shape=[jax.ShapeDtypeStruct((t, d), F32), jax.ShapeDtypeStruct((t, d // 2), U32),
                   jax.ShapeDtypeStruct((t, LANES), F32)],
        compiler_params=_params(("parallel",), vmem), name="out_proj",
    )(om, ol, gm, gl, wo, x2, gf, wr_hi, wr_lo, br)


def _router_kernel(lg_ref, idx_ref, gate_ref, pos_ref, cnt_ref, *, tm):
    i = pl.program_id(0)

    @pl.when(i == 0)
    def _():
        cnt_ref[...] = jnp.zeros_like(cnt_ref)

    lane = lax.broadcasted_iota(I32, (tm, LANES), 1)
    work = jnp.where(lane < N_EXPERTS, lg_ref[...], -jnp.inf)
    vals, firsts, picks = [], [], []
    for _ in range(TOP_K):
        mx = jnp.max(work, axis=-1, keepdims=True)
        first = jnp.min(jnp.where(work == mx, lane, LANES), axis=-1, keepdims=True)
        pick = lane == first
        work = jnp.where(pick, -jnp.inf, work)
        vals.append(mx)
        firsts.append(first)
        picks.append(pick)
    es = [jnp.exp(v - vals[0]) for v in vals]
    denom = es[0] + es[1] + es[2] + es[3]
    chosen = jnp.zeros((tm, LANES), F32)
    for p in picks:
        chosen = jnp.where(p, 1.0, chosen)
    r = lax.broadcasted_iota(I32, (tm, tm), 0)
    cc = lax.broadcasted_iota(I32, (tm, tm), 1)
    before = (r > cc).astype(BF16)
    prior = _dot(before, chosen.astype(BF16)) + cnt_ref[...]
    lane4 = lax.broadcasted_iota(I32, (tm, TOP_K), 1)
    idx = jnp.zeros((tm, TOP_K), I32)
    gate = jnp.zeros((tm, TOP_K), F32)
    pos = jnp.zeros((tm, TOP_K), I32)
    for j in range(TOP_K):
        pj = jnp.sum(jnp.where(picks[j], prior, 0.0), axis=-1, keepdims=True)
        idx = jnp.where(lane4 == j, firsts[j], idx)
        gate = jnp.where(lane4 == j, es[j] / denom, gate)
        pos = jnp.where(lane4 == j, pj.astype(I32), pos)
    idx_ref[...] = idx
    gate_ref[...] = gate
    pos_ref[...] = pos
    cnt_ref[...] += jnp.sum(chosen, axis=0, keepdims=True)


def _router(logits, tm):
    t = logits.shape[0]
    row = lambda i: (i, 0)
    vmem = 4 * tm * tm * 4 + 24 * tm * LANES * 4 + (4 << 20)
    return pl.pallas_call(
        functools.partial(_router_kernel, tm=tm),
        grid=(t // tm,),
        in_specs=[pl.BlockSpec((tm, LANES), row)],
        out_specs=[pl.BlockSpec((tm, TOP_K), row)] * 3 + [pl.BlockSpec((1, LANES), lambda i: (0, 0))],
        out_shape=[jax.ShapeDtypeStruct((t, TOP_K), I32), jax.ShapeDtypeStruct((t, TOP_K), F32),
                   jax.ShapeDtypeStruct((t, TOP_K), I32), jax.ShapeDtypeStruct((1, LANES), F32)],
        compiler_params=_params(("arbitrary",), vmem), name="router",
    )(logits)


DISPATCH_GROUP = 4


def _dispatch_kernel(src_ref, nu_ref, hp_ref, xb_ref, sem):
    i = pl.program_id(0)
    rows = xb_ref.shape[0]
    used = i * DISPATCH_GROUP < nu_ref[0]

    def row_copy(r):
        return pltpu.make_async_copy(hp_ref.at[pl.ds(src_ref[i * rows + r], 1)], xb_ref.at[pl.ds(r, 1)], sem)

    @pl.when(used)
    def _():
        def start(r, carry):
            row_copy(r).start()
            return carry

        def wait(r, carry):
            row_copy(r).wait()
            return carry

        lax.fori_loop(0, rows, start, 0, unroll=8)
        lax.fori_loop(0, rows, wait, 0, unroll=8)

    @pl.when(jnp.logical_not(used))
    def _():
        xb_ref[...] = jnp.zeros_like(xb_ref)


def _dispatch(src, n_used, hp, n_rows):
    t, half = hp.shape
    rows = EXPERT_BLOCK * DISPATCH_GROUP
    assert n_rows % rows == 0
    grid_spec = pltpu.PrefetchScalarGridSpec(
        num_scalar_prefetch=2, grid=(n_rows // rows,),
        in_specs=[pl.BlockSpec(memory_space=pl.ANY)],
        out_specs=pl.BlockSpec((rows, half), lambda i, src, nu: (i, 0)),
        scratch_shapes=[pltpu.SemaphoreType.DMA])
    return pl.pallas_call(
        _dispatch_kernel,
        grid_spec=grid_spec,
        out_shape=jax.ShapeDtypeStruct((n_rows, half), U32),
        compiler_params=_params(("arbitrary",), 3 * rows * half * 4 + (4 << 20)),
        name="dispatch",
    )(src, n_used, hp)


def _moe_up_kernel(be_ref, nu_ref, x_ref, w1_ref, b1_ref, act_ref, *, fc):
    i = pl.program_id(0)

    @pl.when(i < nu_ref[0])
    def _():
        hi, lo = _unpack_halves(x_ref[...])
        a = hi.astype(BF16)
        b = lo.astype(BF16)
        half = a.shape[1]
        d_ff = act_ref.shape[1]

        def pre(c0):
            return (_dot(a, w1_ref[0, :half, c0:c0 + fc]) + _dot(b, w1_ref[0, half:, c0:c0 + fc])
                    + b1_ref[0, :, c0:c0 + fc])

        for c in range(d_ff // fc):
            glu = jnp.minimum(pre(c * fc), SWIGLU_LIMIT)
            lin = jnp.clip(pre(d_ff + c * fc), -SWIGLU_LIMIT, SWIGLU_LIMIT)
            act = glu * jax.nn.sigmoid(SWIGLU_ALPHA * glu) * (lin + 1.0)
            act_ref[:, c * fc:(c + 1) * fc] = act.astype(BF16)

    @pl.when(i >= nu_ref[0])
    def _():
        act_ref[...] = jnp.zeros_like(act_ref)


def _moe_up(block_expert, n_used, xb, w1, b1):
    n_rows, half = xb.shape
    e, d, f2 = w1.shape
    d_ff = f2 // 2
    blk = EXPERT_BLOCK
    n_blocks = n_rows // blk
    fc = 512
    live = lambda i, be, nu: jnp.minimum(i, nu[0] - 1)
    grid_spec = pltpu.PrefetchScalarGridSpec(
        num_scalar_prefetch=2, grid=(n_blocks,),
        in_specs=[pl.BlockSpec((blk, half), lambda i, be, nu: (live(i, be, nu), 0)),
                  pl.BlockSpec((1, d, f2), lambda i, be, nu: (be[live(i, be, nu)], 0, 0)),
                  pl.BlockSpec((1, 1, f2), lambda i, be, nu: (be[live(i, be, nu)], 0, 0))],
        out_specs=pl.BlockSpec((blk, d_ff), lambda i, be, nu: (i, 0)))
    vmem = 2 * d * f2 * 2 + 2 * blk * half * 4 + 2 * blk * d_ff * 2 + 8 * blk * fc * 4 + 2 * blk * d * 2 + (4 << 20)
    return pl.pallas_call(
        functools.partial(_moe_up_kernel, fc=fc),
        grid_spec=grid_spec,
        out_shape=jax.ShapeDtypeStruct((n_rows, d_ff), BF16),
        compiler_params=_params(("arbitrary",), vmem), name="moe_up",
    )(block_expert, n_used, xb, w1, b1)


def _moe_down_kernel(be_ref, nu_ref, a_ref, w2_ref, b2_ref, y_ref):
    i = pl.program_id(0)

    @pl.when(i < nu_ref[0])
    def _():
        y = _dot(a_ref[...], w2_ref[0]) + b2_ref[0]
        half = y.shape[1] // 2
        y_ref[...] = _pack_halves(y[:, :half], y[:, half:])

    @pl.when(i >= nu_ref[0])
    def _():
        y_ref[...] = jnp.zeros_like(y_ref)


def _moe_down(block_expert, n_used, act, w2, b2):
    n_rows, d_ff = act.shape
    e, _, d = w2.shape
    blk = EXPERT_BLOCK
    n_blocks = n_rows // blk
    live = lambda i, be, nu: jnp.minimum(i, nu[0] - 1)
    grid_spec = pltpu.PrefetchScalarGridSpec(
        num_scalar_prefetch=2, grid=(n_blocks,),
        in_specs=[pl.BlockSpec((blk, d_ff), lambda i, be, nu: (live(i, be, nu), 0)),
                  pl.BlockSpec((1, d_ff, d), lambda i, be, nu: (be[live(i, be, nu)], 0, 0)),
                  pl.BlockSpec((1, 1, d), lambda i, be, nu: (be[live(i, be, nu)], 0, 0))],
        out_specs=pl.BlockSpec((blk, d // 2), lambda i, be, nu: (i, 0)))
    vmem = 2 * d_ff * d * 2 + 2 * blk * d_ff * 2 + 2 * blk * d * 2 + 4 * blk * d * 4 + (4 << 20)
    return pl.pallas_call(
        _moe_down_kernel,
        grid_spec=grid_spec,
        out_shape=jax.ShapeDtypeStruct((n_rows, d // 2), U32),
        compiler_params=_params(("arbitrary",), vmem), name="moe_down",
    )(block_expert, n_used, act, w2, b2)


def _combine_kernel(dest_ref, gate_ref, x1_ref, gfin_ref, yb_ref, o_ref, rows_sc, sem, *, tm):
    i = pl.program_id(0)
    base = i * tm

    def row_copy(r, j):
        return pltpu.make_async_copy(yb_ref.at[pl.ds(dest_ref[(base + r) * TOP_K + j], 1)],
                                     rows_sc.at[j, pl.ds(r, 1)], sem)

    def start(r, carry):
        for j in range(TOP_K):
            row_copy(r, j).start()
        return carry

    def wait(r, carry):
        for j in range(TOP_K):
            row_copy(r, j).wait()
        return carry

    lax.fori_loop(0, tm, start, 0, unroll=4)
    lax.fori_loop(0, tm, wait, 0, unroll=4)

    x1 = x1_ref[...]
    half = x1.shape[1] // 2
    gates = gate_ref[...]
    left = x1[:, :half]
    right = x1[:, half:]
    for j in range(TOP_K):
        hi, lo = _unpack_halves(rows_sc[j])
        gj = gates[:, j:j + 1]
        left = left + gj * hi
        right = right + gj * lo
    ms = (jnp.sum(left * left, axis=-1, keepdims=True) + jnp.sum(right * right, axis=-1, keepdims=True)) / (2 * half)
    inv = lax.rsqrt(ms + NORM_EPS)
    g = gfin_ref[...]
    o_ref[:, :half] = left * inv * g[:, :half]
    o_ref[:, half:] = right * inv * g[:, half:]


def _combine(dest, gates, x1, gfin, yb, tm):
    t, d = x1.shape
    half = d // 2
    grid_spec = pltpu.PrefetchScalarGridSpec(
        num_scalar_prefetch=1, grid=(t // tm,),
        in_specs=[pl.BlockSpec((tm, TOP_K), lambda i, dst: (i, 0)),
                  pl.BlockSpec((tm, d), lambda i, dst: (i, 0)),
                  pl.BlockSpec((1, d), lambda i, dst: (0, 0)),
                  pl.BlockSpec(memory_space=pl.ANY)],
        out_specs=pl.BlockSpec((tm, d), lambda i, dst: (i, 0)),
        scratch_shapes=[pltpu.VMEM((TOP_K, tm, half), U32), pltpu.SemaphoreType.DMA])
    vmem = TOP_K * tm * half * 4 + 4 * tm * d * 4 + 6 * tm * d * 4 + 2 * tm * LANES * 4 + (4 << 20)
    return pl.pallas_call(
        functools.partial(_combine_kernel, tm=tm),
        grid_spec=grid_spec,
        out_shape=jax.ShapeDtypeStruct((t, d), F32),
        compiler_params=_params(("arbitrary",), vmem), name="combine",
    )(dest, gates, x1, gfin, yb)


def _rope_tables(seq):
    inv = 1.0 / (ROPE_THETA ** (jnp.arange(0, MLA_ROPE_DIM, 2, dtype=F32) / MLA_ROPE_DIM))
    ang = jnp.arange(seq, dtype=F32)[:, None] * inv[None, :]
    ang = jnp.concatenate([ang, ang], axis=-1)
    ones = jnp.ones((seq, MLA_NOPE_DIM), F32)
    zeros = jnp.zeros((seq, QK_WIDTH - MLA_NOPE_DIM - MLA_ROPE_DIM), F32)
    cos_t = jnp.concatenate([ones, jnp.cos(ang), zeros], axis=-1)
    sin_t = jnp.concatenate([0.0 * ones, jnp.sin(ang), zeros], axis=-1)
    return cos_t, sin_t


def _bf16_pieces(x, n=3):
    out = []
    for _ in range(n):
        p = float(np.asarray(x, dtype=np.float32).astype(jnp.bfloat16).astype(np.float32))
        out.append(p)
        x = x - p
    return out


def _moba_features(seq, tile):
    pos = np.arange(seq)
    blk, off = pos // MOBA_BLOCK, pos % MOBA_BLOCK
    assert seq // MOBA_BLOCK <= FEAT_BLOCKS
    feat = np.zeros((QK_WIDTH - MOBA_HEAD_DIM, seq), np.float32)
    feat[:FEAT_BLOCKS] = np.where(blk[None, :] == np.arange(FEAT_BLOCKS)[:, None], MASKED, 0.0)
    feat[FEAT_ALIBI:FEAT_ALIBI + 3] = blk[None, :]
    feat[FEAT_ALIBI + 3:FEAT_ALIBI + 6] = off[None, :]
    feat = feat.reshape(feat.shape[0], seq // tile, tile).transpose(1, 0, 2)
    aq = np.zeros((MOBA_HEADS, 1, LANES), np.float32)
    for h in range(MOBA_HEADS):
        slope = 2.0 ** (-8.0 * (h + 1) / MOBA_HEADS)
        aq[h, 0, FEAT_ALIBI:FEAT_ALIBI + 3] = _bf16_pieces(slope * LOG2E * MOBA_BLOCK)
        aq[h, 0, FEAT_ALIBI + 3:FEAT_ALIBI + 6] = _bf16_pieces(slope * LOG2E)
    return jnp.asarray(feat, BF16), jnp.asarray(aq, F32)


def _rot_cols(w):
    half = w.shape[-1] // 2
    return jnp.concatenate([-w[..., half:], w[..., :half]], axis=-1)


def _rope_slot(w):
    k = w.shape[0]
    return jnp.concatenate([jnp.zeros((k, MLA_NOPE_DIM), w.dtype), w,
                            jnp.zeros((k, QK_WIDTH - MLA_NOPE_DIM - MLA_ROPE_DIM), w.dtype)], axis=-1)


def _layer(x2, batch, seq, norm_attn_g, w_in, moba_out_g, q_a_norm_g, kv_a_norm_g, w_uq, w_ukv, mla_out_g,
           w_o, norm_ffn_g, w_router, b_router, w1, b1, w2, b2, final_g):
    t, d = x2.shape
    wm = MOBA_HEADS * MOBA_HEAD_DIM
    tile = ATT_TILE
    assert seq % tile == 0 and tile % MOBA_BLOCK == 0
    cos_t, sin_t = _rope_tables(seq)
    feat, aq = _moba_features(seq, tile)

    c_kr = 3 * wm + MLA_Q_RANK + MLA_KV_RANK
    w_kr = w_in[:, c_kr:c_kr + MLA_ROPE_DIM]
    w_all = jnp.concatenate([w_in[:, :c_kr], _rope_slot(w_kr), _rope_slot(_rot_cols(w_kr))], axis=1).astype(BF16)
    wq = w_uq.reshape(MLA_Q_RANK, MLA_HEADS, MLA_NOPE_DIM + MLA_ROPE_DIM)
    zq = jnp.zeros((MLA_Q_RANK, MLA_HEADS, QK_WIDTH - MLA_NOPE_DIM - MLA_ROPE_DIM), F32)
    wqa = jnp.concatenate([wq, zq], axis=-1).reshape(MLA_Q_RANK, -1).astype(BF16)
    wqb = jnp.concatenate([jnp.zeros((MLA_Q_RANK, MLA_HEADS, MLA_NOPE_DIM), F32),
                           _rot_cols(wq[..., MLA_NOPE_DIM:]), zq], axis=-1).reshape(MLA_Q_RANK, -1).astype(BF16)
    wkv = w_ukv.reshape(MLA_KV_RANK, MLA_HEADS, MLA_NOPE_DIM + MLA_V_DIM)
    wk = jnp.concatenate([wkv[..., :MLA_NOPE_DIM],
                          jnp.zeros((MLA_KV_RANK, MLA_HEADS, QK_WIDTH - MLA_NOPE_DIM), F32)],
                         axis=-1).reshape(MLA_KV_RANK, -1).astype(BF16)
    wv = wkv[..., MLA_NOPE_DIM:].reshape(MLA_KV_RANK, -1).astype(BF16)
    wr = jnp.pad(w_router, ((0, 0), (0, LANES - N_EXPERTS)))
    wr_hi = wr.astype(BF16)
    wr_lo = (wr - wr_hi.astype(F32)).astype(BF16)
    br = jnp.pad(b_router, (0, LANES - N_EXPERTS))[None, :]

    qm, ktm, vm, kmean, cq, ckv, kr = _in_proj(x2, norm_attn_g[None, :], w_all, cos_t, sin_t, feat, seq, tm=tile)
    qc, ktl, vl = _mla_up(cq, ckv, q_a_norm_g[None, :], kv_a_norm_g[None, :], wqa, wqb, wk, wv, kr,
                          cos_t, sin_t, seq, tm=tile)
    n_blk = seq // MOBA_BLOCK
    kmt = kmean.reshape(batch, n_blk, MOBA_HEADS, MOBA_HEAD_DIM).transpose(0, 2, 3, 1)
    kmt = jnp.pad(kmt, ((0, 0), (0, 0), (0, 0), (0, LANES - n_blk))).reshape(batch * MOBA_HEADS, MOBA_HEAD_DIM, LANES)
    o_moba, w2b = _moba_attn(qm, ktm, vm, kmt, aq, w2, batch, seq, tile)
    o_mla, w1b = _mla_attn(qc, ktl, vl, w1, batch, seq, tile)
    x1, hp, logits = _out_proj(o_moba, o_mla, moba_out_g[None, :], mla_out_g[None, :], w_o.astype(BF16), x2,
                               norm_ffn_g[None, :], wr_hi, wr_lo, br, tm=256)

    idx, gates, pos, counts = _router(logits, tm=min(512, t))
    blk = EXPERT_BLOCK
    n_blocks = -(-(t * TOP_K) // blk) + N_EXPERTS
    counts = counts[0, :N_EXPERTS].astype(I32)
    padded = ((counts + blk - 1) // blk) * blk
    pad_end = jnp.cumsum(padded)
    pad_start = pad_end - padded
    dest = (pad_start[idx] + pos).reshape(-1)
    block_expert = jnp.minimum(
        jnp.sum((jnp.arange(n_blocks, dtype=I32) * blk)[:, None] >= pad_end[None, :], axis=1), N_EXPERTS - 1).astype(I32)
    n_used = (pad_end[-1:] // blk).astype(I32)
    src = jnp.zeros((n_blocks * blk,), I32).at[dest].set(jnp.arange(t * TOP_K, dtype=I32) // TOP_K,
                                                          unique_indices=True)

    xb = _dispatch(src, n_used, hp, n_blocks * blk)
    act = _moe_up(block_expert, n_used, xb, w1b, b1[:, None, :])
    yb = _moe_down(block_expert, n_used, act, w2b, b2[:, None, :])
    return _combine(dest, gates, x1, final_g[None, :], yb, tm=256)


def kernel(x, norm_attn_g, w_in, moba_out_g, q_a_norm_g, kv_a_norm_g, w_uq, w_ukv, mla_out_g, w_o, norm_ffn_g,
           w_router, b_router, w1, b1, w2, b2, norm_final_g):
    batch, seq, d = x.shape
    depth = w_in.shape[0]
    assert depth == 1, "the final norm is fused into the last layer's combine step"
    x2 = x.reshape(batch * seq, d)
    out = _layer(x2, batch, seq, norm_attn_g[0], w_in[0], moba_out_g[0], q_a_norm_g[0], kv_a_norm_g[0], w_uq[0],
                 w_ukv[0], mla_out_g[0], w_o[0], norm_ffn_g[0], w_router[0], b_router[0], w1[0], b1[0], w2[0],
                 b2[0], norm_final_g)
    return out.reshape(batch, seq, d)
```

```python
import functools

import numpy as np
import jax
import jax.numpy as jnp
from jax import lax
from jax.experimental import pallas as pl
from jax.experimental.pallas import tpu as pltpu

MOBA_HEADS = 8
MOBA_HEAD_DIM = 128
MOBA_BLOCK = 256
MOBA_TOPK = 3
MLA_HEADS = 8
MLA_NOPE_DIM = 128
MLA_ROPE_DIM = 64
MLA_V_DIM = 128
MLA_Q_RANK = 512
MLA_KV_RANK = 512
ROPE_THETA = 10000.0
N_EXPERTS = 32
TOP_K = 4
SWIGLU_LIMIT = 7.0
SWIGLU_ALPHA = 1.702
EXPERT_BLOCK = 256
NORM_EPS = 1e-5

LANES = 128
QK_WIDTH = 256
ATT_TILE = 512
FLASH_STRIPS = 2
LOG2E = 1.4426950408889634
MASKED = -2.0 ** 100
M_INIT = -2.0 ** 98
FEAT_BLOCKS = 64
FEAT_ALIBI = FEAT_BLOCKS
V7X_VMEM_BUDGET = 56 * 1024 * 1024

F32 = jnp.float32
BF16 = jnp.bfloat16
U32 = jnp.uint32
I32 = jnp.int32


def _params(semantics, vmem_bytes):
    return pltpu.CompilerParams(dimension_semantics=semantics,
                                vmem_limit_bytes=min(int(vmem_bytes), V7X_VMEM_BUDGET))


def _rms(xf, g):
    ms = jnp.mean(xf * xf, axis=-1, keepdims=True)
    return xf * lax.rsqrt(ms + NORM_EPS) * g


def _dot(a, b):
    return jnp.dot(a, b, preferred_element_type=F32)


def _pack_halves(a, b):
    ai = lax.bitcast_convert_type(a.astype(BF16).astype(F32), U32)
    bi = lax.bitcast_convert_type(b.astype(BF16).astype(F32), U32)
    return ai | (bi >> 16)


def _unpack_halves(w):
    hi = lax.bitcast_convert_type(w & jnp.uint32(0xFFFF0000), F32)
    lo = lax.bitcast_convert_type(w << 16, F32)
    return hi, lo


def _inproj_kernel(x_ref, g_ref, w_ref, cos_ref, sin_ref, feat_ref,
                   qm_ref, kt_ref, vm_ref, kmean_ref, cq_ref, ckv_ref, kr_ref, *, q_scale):
    h = _rms(x_ref[...], g_ref[...]).astype(BF16)
    w = MOBA_HEADS * MOBA_HEAD_DIM
    dh = MOBA_HEAD_DIM
    tm = h.shape[0]

    def mm(lo, hi):
        return _dot(h, w_ref[:, lo:hi])

    qm_ref[...] = (mm(0, w) * q_scale).astype(BF16)
    k = mm(w, 2 * w)
    for b in range(tm // MOBA_BLOCK):
        kmean_ref[b] = jnp.mean(k[b * MOBA_BLOCK:(b + 1) * MOBA_BLOCK, :], axis=0, keepdims=True)
    v = mm(2 * w, 3 * w).astype(BF16)
    ones = jnp.ones((tm, QK_WIDTH - dh), BF16)
    for hh in range(MOBA_HEADS):
        kt_ref[hh, 0, :dh, :] = k[:, hh * dh:(hh + 1) * dh].T.astype(BF16)
        kt_ref[hh, 0, dh:, :] = feat_ref[0]
        vm_ref[:, hh * QK_WIDTH:hh * QK_WIDTH + dh] = v[:, hh * dh:(hh + 1) * dh]
        vm_ref[:, hh * QK_WIDTH + dh:(hh + 1) * QK_WIDTH] = ones
    c0 = 3 * w
    cq_ref[...] = mm(c0, c0 + MLA_Q_RANK)
    c1 = c0 + MLA_Q_RANK
    ckv_ref[...] = mm(c1, c1 + MLA_KV_RANK)
    c2 = c1 + MLA_KV_RANK
    kr = mm(c2, c2 + QK_WIDTH) * cos_ref[...] + mm(c2 + QK_WIDTH, c2 + 2 * QK_WIDTH) * sin_ref[...]
    kr_ref[...] = kr.astype(BF16)


def _in_proj(x2, g, w_all, cos_t, sin_t, feat, seq, tm):
    t, d = x2.shape
    w = MOBA_HEADS * MOBA_HEAD_DIM
    hw = MOBA_HEADS * QK_WIDTH
    nc = w_all.shape[1]
    n_pos = seq // tm
    nb = tm // MOBA_BLOCK
    row = lambda i: (i, 0)
    const = lambda i: (0, 0)
    pos = lambda i: (i % n_pos, 0)
    out_shape = [jax.ShapeDtypeStruct((t, w), BF16),
                 jax.ShapeDtypeStruct((MOBA_HEADS, t // tm, QK_WIDTH, tm), BF16),
                 jax.ShapeDtypeStruct((t, hw), BF16),
                 jax.ShapeDtypeStruct((t // MOBA_BLOCK, 1, w), F32),
                 jax.ShapeDtypeStruct((t, MLA_Q_RANK), F32), jax.ShapeDtypeStruct((t, MLA_KV_RANK), F32),
                 jax.ShapeDtypeStruct((t, QK_WIDTH), BF16)]
    out_specs = [pl.BlockSpec((tm, w), row),
                 pl.BlockSpec((MOBA_HEADS, 1, QK_WIDTH, tm), lambda i: (0, i, 0, 0)),
                 pl.BlockSpec((tm, hw), row),
                 pl.BlockSpec((nb, 1, w), lambda i: (i, 0, 0)),
                 pl.BlockSpec((tm, MLA_Q_RANK), row), pl.BlockSpec((tm, MLA_KV_RANK), row),
                 pl.BlockSpec((tm, QK_WIDTH), row)]
    vmem = (2 * tm * d * 4 + d * nc * 2 + 2 * tm * (w * 2 + 2 * hw * 2 + 2 * 512 * 4 + 256 * 2)
            + 4 * tm * QK_WIDTH * 4 + tm * d * 2 + 3 * tm * w * 4 + (4 << 20))
    return pl.pallas_call(
        functools.partial(_inproj_kernel, q_scale=MOBA_HEAD_DIM ** -0.5 * LOG2E),
        grid=(t // tm,),
        in_specs=[pl.BlockSpec((tm, d), row), pl.BlockSpec((1, d), const),
                  pl.BlockSpec((d, nc), const, pipeline_mode=pl.Buffered(1)),
                  pl.BlockSpec((tm, QK_WIDTH), pos), pl.BlockSpec((tm, QK_WIDTH), pos),
                  pl.BlockSpec((1, QK_WIDTH - MOBA_HEAD_DIM, tm), lambda i: (i % n_pos, 0, 0))],
        out_specs=out_specs, out_shape=out_shape,
        compiler_params=_params(("parallel",), vmem), name="in_proj",
    )(x2, g, w_all, cos_t, sin_t, feat)


def _mlaup_kernel(cq_ref, ckv_ref, gq_ref, gkv_ref, wqa_ref, wqb_ref, wk_ref, wv_ref,
                  kr_ref, cos_ref, sin_ref, q_ref, kt_ref, v_ref, *, q_scale):
    nq = _rms(cq_ref[...], gq_ref[...]).astype(BF16)
    nkv = _rms(ckv_ref[...], gkv_ref[...]).astype(BF16)
    cos = cos_ref[...]
    sin = sin_ref[...]
    kr = kr_ref[...].astype(F32)
    ones = jnp.ones((nq.shape[0], QK_WIDTH - MLA_V_DIM), BF16)
    for hh in range(MLA_HEADS):
        cols = slice(hh * QK_WIDTH, (hh + 1) * QK_WIDTH)
        q = _dot(nq, wqa_ref[:, cols]) * cos + _dot(nq, wqb_ref[:, cols]) * sin
        q_ref[:, cols] = (q * q_scale).astype(BF16)
        kt_ref[hh, 0] = (_dot(nkv, wk_ref[:, cols]) + kr).T.astype(BF16)
        v_ref[:, hh * QK_WIDTH:hh * QK_WIDTH + MLA_V_DIM] = _dot(
            nkv, wv_ref[:, hh * MLA_V_DIM:(hh + 1) * MLA_V_DIM]).astype(BF16)
        v_ref[:, hh * QK_WIDTH + MLA_V_DIM:(hh + 1) * QK_WIDTH] = ones


def _mla_up(cq, ckv, gq, gkv, wqa, wqb, wk, wv, kr, cos_t, sin_t, seq, tm):
    t = cq.shape[0]
    n_pos = seq // tm
    row = lambda i: (i, 0)
    const = lambda i: (0, 0)
    pos = lambda i: (i % n_pos, 0)
    hq = MLA_HEADS * QK_WIDTH
    hv = MLA_HEADS * MLA_V_DIM
    scale = (MLA_NOPE_DIM + MLA_ROPE_DIM) ** -0.5 * LOG2E
    vmem = (4 * tm * 512 * 4 + 2 * (3 * 512 * hq * 2 + 512 * hv * 2) + 8 * tm * QK_WIDTH * 4
            + 2 * tm * 3 * hq * 2 + (8 << 20))
    return pl.pallas_call(
        functools.partial(_mlaup_kernel, q_scale=scale),
        grid=(t // tm,),
        in_specs=[pl.BlockSpec((tm, MLA_Q_RANK), row), pl.BlockSpec((tm, MLA_KV_RANK), row),
                  pl.BlockSpec((1, MLA_Q_RANK), const), pl.BlockSpec((1, MLA_KV_RANK), const),
                  pl.BlockSpec((MLA_Q_RANK, hq), const), pl.BlockSpec((MLA_Q_RANK, hq), const),
                  pl.BlockSpec((MLA_KV_RANK, hq), const), pl.BlockSpec((MLA_KV_RANK, hv), const),
                  pl.BlockSpec((tm, QK_WIDTH), row),
                  pl.BlockSpec((tm, QK_WIDTH), pos), pl.BlockSpec((tm, QK_WIDTH), pos)],
        out_specs=[pl.BlockSpec((tm, hq), row),
                   pl.BlockSpec((MLA_HEADS, 1, QK_WIDTH, tm), lambda i: (0, i, 0, 0)),
                   pl.BlockSpec((tm, hq), row)],
        out_shape=[jax.ShapeDtypeStruct((t, hq), BF16),
                   jax.ShapeDtypeStruct((MLA_HEADS, t // tm, QK_WIDTH, tm), BF16),
                   jax.ShapeDtypeStruct((t, hq), BF16)],
        compiler_params=_params(("parallel",), vmem), name="mla_up",
    )(cq, ckv, gq, gkv, wqa, wqb, wk, wv, kr, cos_t, sin_t)


def _flash_stages(bufs, q_ref, *, pv=None, softmax=None, scores=None):
    s_sc, p_sc, alpha_sc, m_sc, acc_sc = bufs
    rows = s_sc.shape[0] // FLASH_STRIPS
    for r in range(FLASH_STRIPS):
        sl = slice(r * rows, (r + 1) * rows)
        if pv is not None:
            acc_sc[sl] = alpha_sc[sl] * acc_sc[sl] + _dot(p_sc[sl], pv)
        if softmax is not None:
            s = s_sc[sl]
            if softmax is not True:
                s = jnp.where(softmax[sl], s, MASKED)
            m_old = m_sc[sl]
            m_new = jnp.maximum(m_old, jnp.max(s, axis=-1, keepdims=True))
            p_sc[sl] = jnp.exp2(s - m_new).astype(BF16)
            alpha_sc[sl] = jnp.exp2(m_old - m_new)
            m_sc[sl] = m_new
        if scores is not None:
            s_sc[sl] = _dot(q_ref[sl], scores)


def _flash_loop(i, q_ref, kt_ref, v_ref, o_ref, bufs, diag_mask, *, tile, dv):
    s_sc, p_sc, alpha_sc, m_sc, acc_sc = bufs
    m_sc[...] = jnp.full_like(m_sc, M_INIT)
    acc_sc[...] = jnp.zeros_like(acc_sc)

    def v_tile(j):
        return v_ref[pl.ds(pl.multiple_of(j * tile, tile), tile), :]

    _flash_stages(bufs, q_ref, scores=kt_ref[0, 0])

    @pl.when(i >= 1)
    def _():
        _flash_stages(bufs, q_ref, softmax=True, scores=kt_ref[0, 1])

    def steady(k, carry):
        _flash_stages(bufs, q_ref, pv=v_tile(k - 2), softmax=True, scores=kt_ref[0, k])
        return carry

    lax.fori_loop(2, i + 1, steady, 0)

    @pl.when(i >= 1)
    def _():
        _flash_stages(bufs, q_ref, pv=v_tile(i - 1), softmax=diag_mask)

    @pl.when(i == 0)
    def _():
        _flash_stages(bufs, q_ref, softmax=diag_mask)

    _flash_stages(bufs, q_ref, pv=v_tile(i))
    acc = acc_sc[...]
    o_ref[...] = acc[:, :dv] / acc[:, dv:2 * dv]


def _mla_attn_kernel(q_ref, kt_ref, v_ref, w_ref, o_ref, wb_ref, *bufs, tile):
    wb_ref[...] = w_ref[...].astype(BF16)
    row = lax.broadcasted_iota(I32, (tile, tile), 0)
    col = lax.broadcasted_iota(I32, (tile, tile), 1)
    _flash_loop(pl.program_id(2), q_ref, kt_ref, v_ref, o_ref, bufs, row >= col, tile=tile, dv=MLA_V_DIM)


def _attn_call(kernel, name, q_spec, extra_specs, scratch, args, w_f32, batch, heads, seq, tile, dv):
    t = batch * seq
    nq = seq // tile
    steps = batch * heads * nq
    w2d = w_f32.reshape(-1, w_f32.shape[-1])
    assert w2d.shape[0] % steps == 0
    w_rows = w2d.shape[0] // steps
    w_spec = pl.BlockSpec((w_rows, w2d.shape[1]), lambda b, h, i: ((b * heads + h) * nq + i, 0))
    vmem = (2 * 2 * seq * QK_WIDTH * 2 + 10 * tile * tile * 4 + 10 * tile * QK_WIDTH * 4
            + 3 * w_rows * w2d.shape[1] * 6 + (4 << 20))
    out, wb = pl.pallas_call(
        kernel,
        grid=(batch, heads, nq),
        in_specs=[q_spec,
                  pl.BlockSpec((1, nq, QK_WIDTH, tile), lambda b, h, i: (h, b, 0, 0)),
                  pl.BlockSpec((seq, QK_WIDTH), lambda b, h, i: (b, h))] + extra_specs + [w_spec],
        out_specs=[pl.BlockSpec((tile, dv), lambda b, h, i: (b * nq + i, h)), w_spec],
        out_shape=[jax.ShapeDtypeStruct((t, heads * dv), F32), jax.ShapeDtypeStruct(w2d.shape, BF16)],
        scratch_shapes=scratch + [pltpu.VMEM((tile, tile), F32), pltpu.VMEM((tile, tile), BF16),
                                  pltpu.VMEM((tile, 1), F32), pltpu.VMEM((tile, 1), F32),
                                  pltpu.VMEM((tile, QK_WIDTH), F32)],
        compiler_params=_params(("parallel", "parallel", "arbitrary"), vmem), name=name,
    )(*args, w2d)
    return out, wb.reshape(w_f32.shape)


def _mla_attn(qc, kt, v, w_f32, batch, seq, tile):
    nq = seq // tile
    return _attn_call(functools.partial(_mla_attn_kernel, tile=tile), "mla_attn",
                      pl.BlockSpec((tile, QK_WIDTH), lambda b, h, i: (b * nq + i, h)), [], [],
                      (qc, kt, v), w_f32, batch, MLA_HEADS, seq, tile, MLA_V_DIM)


def _moba_attn_kernel(q_ref, kt_ref, v_ref, kmt_ref, aq_ref, w_ref, o_ref, wb_ref, qa_sc, *bufs, tile):
    wb_ref[...] = w_ref[...].astype(BF16)
    u = pl.program_id(2)
    blk = MOBA_BLOCK
    q = q_ref[...]
    km = kmt_ref[0]
    km_hi = km.astype(BF16)
    km_lo = (km - km_hi.astype(F32)).astype(BF16)
    gate = _dot(q, km_hi) + _dot(q, km_lo)
    lane = lax.broadcasted_iota(I32, (tile, LANES), 1)
    own = u * (tile // blk) + lax.broadcasted_iota(I32, (tile, LANES), 0) // blk
    g = jnp.where(lane < own, gate, -jnp.inf)
    keep = lane == own
    for _ in range(MOBA_TOPK):
        mx = jnp.max(g, axis=-1, keepdims=True)
        cand = jnp.where((g == mx) & (mx > -jnp.inf), lane, LANES)
        pick = lane == jnp.min(cand, axis=-1, keepdims=True)
        keep = keep | pick
        g = jnp.where(pick, -jnp.inf, g)
    feat = jnp.where(lane < FEAT_BLOCKS, jnp.where(keep, 0.0, 1.0), aq_ref[0])
    qa_sc[:, :MOBA_HEAD_DIM] = q
    qa_sc[:, MOBA_HEAD_DIM:] = feat.astype(BF16)

    row = lax.broadcasted_iota(I32, (tile, tile), 0)
    col = lax.broadcasted_iota(I32, (tile, tile), 1)
    causal_in_own_block = (row // blk != col // blk) | (col <= row)
    _flash_loop(u, qa_sc, kt_ref, v_ref, o_ref, bufs, causal_in_own_block, tile=tile, dv=MOBA_HEAD_DIM)


def _moba_attn(qm, kt, vm, kmt, aq, w_f32, batch, seq, tile):
    nq = seq // tile
    dh = MOBA_HEAD_DIM
    return _attn_call(functools.partial(_moba_attn_kernel, tile=tile), "moba_attn",
                      pl.BlockSpec((tile, dh), lambda b, h, i: (b * nq + i, h)),
                      [pl.BlockSpec((1, dh, LANES), lambda b, h, i: (b * MOBA_HEADS + h, 0, 0)),
                       pl.BlockSpec((1, 1, LANES), lambda b, h, i: (h, 0, 0))],
                      [pltpu.VMEM((tile, QK_WIDTH), BF16)],
                      (qm, kt, vm, kmt, aq), w_f32, batch, MOBA_HEADS, seq, tile, dh)


def _outproj_kernel(om_ref, ol_ref, gm_ref, gl_ref, wo_ref, x_ref, gf_ref, wrh_ref, wrl_ref, br_ref,
                    x1_ref, hp_ref, lg_ref):
    wm = om_ref.shape[1]
    a = _rms(om_ref[...], gm_ref[...]).astype(BF16)
    b = _rms(ol_ref[...], gl_ref[...]).astype(BF16)
    x1 = x_ref[...] + _dot(a, wo_ref[:wm, :]) + _dot(b, wo_ref[wm:, :])
    x1_ref[...] = x1
    h2 = _rms(x1, gf_ref[...])
    half = h2.shape[1] // 2
    hp_ref[...] = _pack_halves(h2[:, :half], h2[:, half:])
    h_hi = h2.astype(BF16)
    h_lo = (h2 - h_hi.astype(F32)).astype(BF16)
    lg_ref[...] = (_dot(h_hi, wrh_ref[...]) + _dot(h_lo, wrh_ref[...]) + _dot(h_hi, wrl_ref[...])
                   + br_ref[...])


def _out_proj(om, ol, gm, gl, wo, x2, gf, wr_hi, wr_lo, br, tm):
    t, d = x2.shape
    wm = om.shape[1]
    wl = ol.shape[1]
    row = lambda i: (i, 0)
    const = lambda i: (0, 0)
    vmem = 2 * (wm + wl) * d * 2 + 2 * tm * (wm + wl + 2 * d) * 4 + 2 * tm * d * 2 + 6 * tm * d * 4 + (4 << 20)
    return pl.pallas_call(
        _outproj_kernel,
        grid=(t // tm,),
        in_specs=[pl.BlockSpec((tm, wm), row), pl.BlockSpec((tm, wl), row),
                  pl.BlockSpec((1, wm), const), pl.BlockSpec((1, wl), const),
                  pl.BlockSpec((wm + wl, d), const), pl.BlockSpec((tm, d), row), pl.BlockSpec((1, d), const),
                  pl.BlockSpec((d, LANES), const), pl.BlockSpec((d, LANES), const), pl.BlockSpec((1, LANES), const)],
        out_specs=[pl.BlockSpec((tm, d), row), pl.BlockSpec((tm, d // 2), row), pl.BlockSpec((tm, LANES), row)],
        out_shape=[jax.ShapeDtypeStruct((t, d), F32), jax.ShapeDtypeStruct((t, d // 2), U32),
                   jax.ShapeDtypeStruct((t, LANES), F32)],
        compiler_params=_params(("parallel",), vmem), name="out_proj",
    )(om, ol, gm, gl, wo, x2, gf, wr_hi, wr_lo, br)


def _router_kernel(lg_ref, idx_ref, gate_ref, pos_ref, cnt_ref, *, tm):
    i = pl.program_id(0)

    @pl.when(i == 0)
    def _():
        cnt_ref[...] = jnp.zeros_like(cnt_ref)

    lane = lax.broadcasted_iota(I32, (tm, LANES), 1)
    work = jnp.where(lane < N_EXPERTS, lg_ref[...], -jnp.inf)
    vals, firsts, picks = [], [], []
    for _ in range(TOP_K):
        mx = jnp.max(work, axis=-1, keepdims=True)
        first = jnp.min(jnp.where(work == mx, lane, LANES), axis=-1, keepdims=True)
        pick = lane == first
        work = jnp.where(pick, -jnp.inf, work)
        vals.append(mx)
        firsts.append(first)
        picks.append(pick)
    es = [jnp.exp(v - vals[0]) for v in vals]
    denom = es[0] + es[1] + es[2] + es[3]
    chosen = jnp.zeros((tm, LANES), F32)
    for p in picks:
        chosen = jnp.where(p, 1.0, chosen)
    r = lax.broadcasted_iota(I32, (tm, tm), 0)
    cc = lax.broadcasted_iota(I32, (tm, tm), 1)
    before = (r > cc).astype(BF16)
    prior = _dot(before, chosen.astype(BF16)) + cnt_ref[...]
    lane4 = lax.broadcasted_iota(I32, (tm, TOP_K), 1)
    idx = jnp.zeros((tm, TOP_K), I32)
    gate = jnp.zeros((tm, TOP_K), F32)
    pos = jnp.zeros((tm, TOP_K), I32)
    for j in range(TOP_K):
        pj = jnp.sum(jnp.where(picks[j], prior, 0.0), axis=-1, keepdims=True)
        idx = jnp.where(lane4 == j, firsts[j], idx)
        gate = jnp.where(lane4 == j, es[j] / denom, gate)
        pos = jnp.where(lane4 == j, pj.astype(I32), pos)
    idx_ref[...] = idx
    gate_ref[...] = gate
    pos_ref[...] = pos
    cnt_ref[...] += jnp.sum(chosen, axis=0, keepdims=True)


def _router(logits, tm):
    t = logits.shape[0]
    row = lambda i: (i, 0)
    vmem = 4 * tm * tm * 4 + 24 * tm * LANES * 4 + (4 << 20)
    return pl.pallas_call(
        functools.partial(_router_kernel, tm=tm),
        grid=(t // tm,),
        in_specs=[pl.BlockSpec((tm, LANES), row)],
        out_specs=[pl.BlockSpec((tm, TOP_K), row)] * 3 + [pl.BlockSpec((1, LANES), lambda i: (0, 0))],
        out_shape=[jax.ShapeDtypeStruct((t, TOP_K), I32), jax.ShapeDtypeStruct((t, TOP_K), F32),
                   jax.ShapeDtypeStruct((t, TOP_K), I32), jax.ShapeDtypeStruct((1, LANES), F32)],
        compiler_params=_params(("arbitrary",), vmem), name="router",
    )(logits)


def _moe_up_kernel(be_ref, nu_ref, src_ref, hp_ref, w1_ref, b1_ref, act_ref, xs_sc, sem, *, fc):
    i = pl.program_id(0)
    blk = act_ref.shape[0]
    slot = i % 2

    def row_copy(block, r, s):
        return pltpu.make_async_copy(hp_ref.at[pl.ds(src_ref[block * blk + r], 1)],
                                     xs_sc.at[s, pl.ds(r, 1)], sem.at[s])

    def start_block(block, s):
        def body(r, carry):
            row_copy(block, r, s).start()
            return carry
        lax.fori_loop(0, blk, body, 0, unroll=8)

    def wait_block(block, s):
        def body(r, carry):
            row_copy(block, r, s).wait()
            return carry
        lax.fori_loop(0, blk, body, 0, unroll=8)

    @pl.when(i == 0)
    def _():
        start_block(0, 0)

    @pl.when(i < nu_ref[0])
    def _():
        wait_block(i, slot)
        hi, lo = _unpack_halves(xs_sc[slot])
        a = hi.astype(BF16)
        b = lo.astype(BF16)
        half = a.shape[1]
        d_ff = act_ref.shape[1]
        n_chunks = d_ff // fc
        nxt = jnp.minimum(i + 1, pl.num_programs(0) - 1)

        def pre(c0):
            return (_dot(a, w1_ref[0, :half, c0:c0 + fc]) + _dot(b, w1_ref[0, half:, c0:c0 + fc])
                    + b1_ref[0, :, c0:c0 + fc])

        per = blk // (2 * n_chunks)
        for c in range(n_chunks):
            glu = jnp.minimum(pre(c * fc), SWIGLU_LIMIT)
            for r in range(2 * c * per, (2 * c + 1) * per):
                row_copy(nxt, r, 1 - slot).start()
            lin = jnp.clip(pre(d_ff + c * fc), -SWIGLU_LIMIT, SWIGLU_LIMIT)
            act = glu * jax.nn.sigmoid(SWIGLU_ALPHA * glu) * (lin + 1.0)
            act_ref[:, c * fc:(c + 1) * fc] = act.astype(BF16)
            for r in range((2 * c + 1) * per, (2 * c + 2) * per):
                row_copy(nxt, r, 1 - slot).start()

    @pl.when(i == nu_ref[0])
    def _():
        wait_block(i, slot)

    @pl.when(i >= nu_ref[0])
    def _():
        act_ref[...] = jnp.zeros_like(act_ref)


def _moe_up(block_expert, n_used, src, hp, w1, b1):
    n_rows = src.shape[0]
    half = hp.shape[1]
    e, d, f2 = w1.shape
    d_ff = f2 // 2
    blk = EXPERT_BLOCK
    n_blocks = n_rows // blk
    fc = 512
    live = lambda i, be, nu: jnp.minimum(i, nu[0] - 1)
    grid_spec = pltpu.PrefetchScalarGridSpec(
        num_scalar_prefetch=3, grid=(n_blocks,),
        in_specs=[pl.BlockSpec(memory_space=pl.ANY),
                  pl.BlockSpec((1, d, f2), lambda i, be, nu, src: (be[live(i, be, nu)], 0, 0)),
                  pl.BlockSpec((1, 1, f2), lambda i, be, nu, src: (be[live(i, be, nu)], 0, 0))],
        out_specs=pl.BlockSpec((blk, d_ff), lambda i, be, nu, src: (i, 0)),
        scratch_shapes=[pltpu.VMEM((2, blk, half), U32), pltpu.SemaphoreType.DMA((2,))])
    vmem = 2 * d * f2 * 2 + 2 * blk * half * 4 + 2 * blk * d_ff * 2 + 8 * blk * fc * 4 + 2 * blk * d * 2 + (4 << 20)
    return pl.pallas_call(
        functools.partial(_moe_up_kernel, fc=fc),
        grid_spec=grid_spec,
        out_shape=jax.ShapeDtypeStruct((n_rows, d_ff), BF16),
        compiler_params=_params(("arbitrary",), vmem), name="moe_up",
    )(block_expert, n_used, src, hp, w1, b1)


def _moe_down_kernel(be_ref, nu_ref, a_ref, w2_ref, b2_ref, y_ref):
    i = pl.program_id(0)

    @pl.when(i < nu_ref[0])
    def _():
        y = _dot(a_ref[...], w2_ref[0]) + b2_ref[0]
        half = y.shape[1] // 2
        y_ref[...] = _pack_halves(y[:, :half], y[:, half:])

    @pl.when(i >= nu_ref[0])
    def _():
        y_ref[...] = jnp.zeros_like(y_ref)


def _moe_down(block_expert, n_used, act, w2, b2):
    n_rows, d_ff = act.shape
    e, _, d = w2.shape
    blk = EXPERT_BLOCK
    n_blocks = n_rows // blk
    live = lambda i, be, nu: jnp.minimum(i, nu[0] - 1)
    grid_spec = pltpu.PrefetchScalarGridSpec(
        num_scalar_prefetch=2, grid=(n_blocks,),
        in_specs=[pl.BlockSpec((blk, d_ff), lambda i, be, nu: (live(i, be, nu), 0)),
                  pl.BlockSpec((1, d_ff, d), lambda i, be, nu: (be[live(i, be, nu)], 0, 0)),
                  pl.BlockSpec((1, 1, d), lambda i, be, nu: (be[live(i, be, nu)], 0, 0))],
        out_specs=pl.BlockSpec((blk, d // 2), lambda i, be, nu: (i, 0)))
    vmem = 2 * d_ff * d * 2 + 2 * blk * d_ff * 2 + 2 * blk * d * 2 + 4 * blk * d * 4 + (4 << 20)
    return pl.pallas_call(
        _moe_down_kernel,
        grid_spec=grid_spec,
        out_shape=jax.ShapeDtypeStruct((n_rows, d // 2), U32),
        compiler_params=_params(("arbitrary",), vmem), name="moe_down",
    )(block_expert, n_used, act, w2, b2)


def _combine_kernel(dest_ref, gate_ref, x1_ref, gfin_ref, yb_ref, o_ref, rows_sc, sem, *, tm):
    i = pl.program_id(0)
    slot = i % 2

    def row_copy(tile, r, j, s):
        return pltpu.make_async_copy(yb_ref.at[pl.ds(dest_ref[(tile * tm + r) * TOP_K + j], 1)],
                                     rows_sc.at[s, j, pl.ds(r, 1)], sem.at[s])

    def start_tile(tile, s):
        def body(r, carry):
            for j in range(TOP_K):
                row_copy(tile, r, j, s).start()
            return carry
        lax.fori_loop(0, tm, body, 0, unroll=4)

    def wait_tile(tile, s):
        def body(r, carry):
            for j in range(TOP_K):
                row_copy(tile, r, j, s).wait()
            return carry
        lax.fori_loop(0, tm, body, 0, unroll=4)

    @pl.when(i == 0)
    def _():
        start_tile(0, 0)

    @pl.when(i + 1 < pl.num_programs(0))
    def _():
        start_tile(i + 1, 1 - slot)

    wait_tile(i, slot)

    x1 = x1_ref[...]
    half = x1.shape[1] // 2
    gates = gate_ref[...]
    left = x1[:, :half]
    right = x1[:, half:]
    for j in range(TOP_K):
        hi, lo = _unpack_halves(rows_sc[slot, j])
        gj = gates[:, j:j + 1]
        left = left + gj * hi
        right = right + gj * lo
    ms = (jnp.sum(left * left, axis=-1, keepdims=True) + jnp.sum(right * right, axis=-1, keepdims=True)) / (2 * half)
    inv = lax.rsqrt(ms + NORM_EPS)
    g = gfin_ref[...]
    o_ref[:, :half] = left * inv * g[:, :half]
    o_ref[:, half:] = right * inv * g[:, half:]


def _combine(dest, gates, x1, gfin, yb, tm):
    t, d = x1.shape
    half = d // 2
    grid_spec = pltpu.PrefetchScalarGridSpec(
        num_scalar_prefetch=1, grid=(t // tm,),
        in_specs=[pl.BlockSpec((tm, TOP_K), lambda i, dst: (i, 0)),
                  pl.BlockSpec((tm, d), lambda i, dst: (i, 0)),
                  pl.BlockSpec((1, d), lambda i, dst: (0, 0)),
                  pl.BlockSpec(memory_space=pl.ANY)],
        out_specs=pl.BlockSpec((tm, d), lambda i, dst: (i, 0)),
        scratch_shapes=[pltpu.VMEM((2, TOP_K, tm, half), U32), pltpu.SemaphoreType.DMA((2,))])
    vmem = 2 * TOP_K * tm * half * 4 + 4 * tm * d * 4 + 6 * tm * d * 4 + 2 * tm * LANES * 4 + (4 << 20)
    return pl.pallas_call(
        functools.partial(_combine_kernel, tm=tm),
        grid_spec=grid_spec,
        out_shape=jax.ShapeDtypeStruct((t, d), F32),
        compiler_params=_params(("arbitrary",), vmem), name="combine",
    )(dest, gates, x1, gfin, yb)


def _rope_tables(seq):
    inv = 1.0 / (ROPE_THETA ** (jnp.arange(0, MLA_ROPE_DIM, 2, dtype=F32) / MLA_ROPE_DIM))
    ang = jnp.arange(seq, dtype=F32)[:, None] * inv[None, :]
    ang = jnp.concatenate([ang, ang], axis=-1)
    ones = jnp.ones((seq, MLA_NOPE_DIM), F32)
    zeros = jnp.zeros((seq, QK_WIDTH - MLA_NOPE_DIM - MLA_ROPE_DIM), F32)
    cos_t = jnp.concatenate([ones, jnp.cos(ang), zeros], axis=-1)
    sin_t = jnp.concatenate([0.0 * ones, jnp.sin(ang), zeros], axis=-1)
    return cos_t, sin_t


def _bf16_pieces(x, n=3):
    out = []
    for _ in range(n):
        p = float(np.asarray(x, dtype=np.float32).astype(jnp.bfloat16).astype(np.float32))
        out.append(p)
        x = x - p
    return out


def _moba_features(seq, tile):
    pos = np.arange(seq)
    blk, off = pos // MOBA_BLOCK, pos % MOBA_BLOCK
    assert seq // MOBA_BLOCK <= FEAT_BLOCKS
    feat = np.zeros((QK_WIDTH - MOBA_HEAD_DIM, seq), np.float32)
    feat[:FEAT_BLOCKS] = np.where(blk[None, :] == np.arange(FEAT_BLOCKS)[:, None], MASKED, 0.0)
    feat[FEAT_ALIBI:FEAT_ALIBI + 3] = blk[None, :]
    feat[FEAT_ALIBI + 3:FEAT_ALIBI + 6] = off[None, :]
    feat = feat.reshape(feat.shape[0], seq // tile, tile).transpose(1, 0, 2)
    aq = np.zeros((MOBA_HEADS, 1, LANES), np.float32)
    for h in range(MOBA_HEADS):
        slope = 2.0 ** (-8.0 * (h + 1) / MOBA_HEADS)
        aq[h, 0, FEAT_ALIBI:FEAT_ALIBI + 3] = _bf16_pieces(slope * LOG2E * MOBA_BLOCK)
        aq[h, 0, FEAT_ALIBI + 3:FEAT_ALIBI + 6] = _bf16_pieces(slope * LOG2E)
    return jnp.asarray(feat, BF16), jnp.asarray(aq, F32)


def _rot_cols(w):
    half = w.shape[-1] // 2
    return jnp.concatenate([-w[..., half:], w[..., :half]], axis=-1)


def _rope_slot(w):
    k = w.shape[0]
    return jnp.concatenate([jnp.zeros((k, MLA_NOPE_DIM), w.dtype), w,
                            jnp.zeros((k, QK_WIDTH - MLA_NOPE_DIM - MLA_ROPE_DIM), w.dtype)], axis=-1)


def _layer(x2, batch, seq, norm_attn_g, w_in, moba_out_g, q_a_norm_g, kv_a_norm_g, w_uq, w_ukv, mla_out_g,
           w_o, norm_ffn_g, w_router, b_router, w1, b1, w2, b2, final_g):
    t, d = x2.shape
    wm = MOBA_HEADS * MOBA_HEAD_DIM
    tile = ATT_TILE
    assert seq % tile == 0 and tile % MOBA_BLOCK == 0
    cos_t, sin_t = _rope_tables(seq)
    feat, aq = _moba_features(seq, tile)

    c_kr = 3 * wm + MLA_Q_RANK + MLA_KV_RANK
    w_kr = w_in[:, c_kr:c_kr + MLA_ROPE_DIM]
    w_all = jnp.concatenate([w_in[:, :c_kr], _rope_slot(w_kr), _rope_slot(_rot_cols(w_kr))], axis=1).astype(BF16)
    wq = w_uq.reshape(MLA_Q_RANK, MLA_HEADS, MLA_NOPE_DIM + MLA_ROPE_DIM)
    zq = jnp.zeros((MLA_Q_RANK, MLA_HEADS, QK_WIDTH - MLA_NOPE_DIM - MLA_ROPE_DIM), F32)
    wqa = jnp.concatenate([wq, zq], axis=-1).reshape(MLA_Q_RANK, -1).astype(BF16)
    wqb = jnp.concatenate([jnp.zeros((MLA_Q_RANK, MLA_HEADS, MLA_NOPE_DIM), F32),
                           _rot_cols(wq[..., MLA_NOPE_DIM:]), zq], axis=-1).reshape(MLA_Q_RANK, -1).astype(BF16)
    wkv = w_ukv.reshape(MLA_KV_RANK, MLA_HEADS, MLA_NOPE_DIM + MLA_V_DIM)
    wk = jnp.concatenate([wkv[..., :MLA_NOPE_DIM],
                          jnp.zeros((MLA_KV_RANK, MLA_HEADS, QK_WIDTH - MLA_NOPE_DIM), F32)],
                         axis=-1).reshape(MLA_KV_RANK, -1).astype(BF16)
    wv = wkv[..., MLA_NOPE_DIM:].reshape(MLA_KV_RANK, -1).astype(BF16)
    wr = jnp.pad(w_router, ((0, 0), (0, LANES - N_EXPERTS)))
    wr_hi = wr.astype(BF16)
    wr_lo = (wr - wr_hi.astype(F32)).astype(BF16)
    br = jnp.pad(b_router, (0, LANES - N_EXPERTS))[None, :]

    qm, ktm, vm, kmean, cq, ckv, kr = _in_proj(x2, norm_attn_g[None, :], w_all, cos_t, sin_t, feat, seq, tm=tile)
    qc, ktl, vl = _mla_up(cq, ckv, q_a_norm_g[None, :], kv_a_norm_g[None, :], wqa, wqb, wk, wv, kr,
                          cos_t, sin_t, seq, tm=tile)
    n_blk = seq // MOBA_BLOCK
    kmt = kmean.reshape(batch, n_blk, MOBA_HEADS, MOBA_HEAD_DIM).transpose(0, 2, 3, 1)
    kmt = jnp.pad(kmt, ((0, 0), (0, 0), (0, 0), (0, LANES - n_blk))).reshape(batch * MOBA_HEADS, MOBA_HEAD_DIM, LANES)
    o_moba, w2b = _moba_attn(qm, ktm, vm, kmt, aq, w2, batch, seq, tile)
    o_mla, w1b = _mla_attn(qc, ktl, vl, w1, batch, seq, tile)
    x1, hp, logits = _out_proj(o_moba, o_mla, moba_out_g[None, :], mla_out_g[None, :], w_o.astype(BF16), x2,
                               norm_ffn_g[None, :], wr_hi, wr_lo, br, tm=256)

    idx, gates, pos, counts = _router(logits, tm=min(512, t))
    blk = EXPERT_BLOCK
    n_blocks = -(-(t * TOP_K) // blk) + N_EXPERTS
    counts = counts[0, :N_EXPERTS].astype(I32)
    padded = ((counts + blk - 1) // blk) * blk
    pad_end = jnp.cumsum(padded)
    pad_start = pad_end - padded
    dest = (pad_start[idx] + pos).reshape(-1)
    block_expert = jnp.minimum(
        jnp.sum((jnp.arange(n_blocks, dtype=I32) * blk)[:, None] >= pad_end[None, :], axis=1), N_EXPERTS - 1).astype(I32)
    n_used = (pad_end[-1:] // blk).astype(I32)
    src = jnp.zeros((n_blocks * blk,), I32).at[dest].set(jnp.arange(t * TOP_K, dtype=I32) // TOP_K,
                                                          unique_indices=True)

    act = _moe_up(block_expert, n_used, src, hp, w1b, b1[:, None, :])
    yb = _moe_down(block_expert, n_used, act, w2b, b2[:, None, :])
    return _combine(dest, gates, x1, final_g[None, :], yb, tm=256)


def kernel(x, norm_attn_g, w_in, moba_out_g, q_a_norm_g, kv_a_norm_g, w_uq, w_ukv, mla_out_g, w_o, norm_ffn_g,
           w_router, b_router, w1, b1, w2, b2, norm_final_g):
    batch, seq, d = x.shape
    depth = w_in.shape[0]
    assert depth == 1, "the final norm is fused into the last layer's combine step"
    x2 = x.reshape(batch * seq, d)
    out = _layer(x2, batch, seq, norm_attn_g[0], w_in[0], moba_out_g[0], q_a_norm_g[0], kv_a_norm_g[0], w_uq[0],
                 w_ukv[0], mla_out_g[0], w_o[0], norm_ffn_g[0], w_router[0], b_router[0], w1[0], b1[0], w2[0],
                 b2[0], norm_final_g)
    return out.reshape(batch, seq, d)
```

```python
import functools

import numpy as np
import jax
import jax.numpy as jnp
from jax import lax
from jax.experimental import pallas as pl
from jax.experimental.pallas import tpu as pltpu

MOBA_HEADS = 8
MOBA_HEAD_DIM = 128
MOBA_BLOCK = 256
MOBA_TOPK = 3
MLA_HEADS = 8
MLA_NOPE_DIM = 128
MLA_ROPE_DIM = 64
MLA_V_DIM = 128
MLA_Q_RANK = 512
MLA_KV_RANK = 512
ROPE_THETA = 10000.0
N_EXPERTS = 32
TOP_K = 4
SWIGLU_LIMIT = 7.0
SWIGLU_ALPHA = 1.702
EXPERT_BLOCK = 256
NORM_EPS = 1e-5

LANES = 128
SUBLANES = 8
QK_WIDTH = 256
ATT_TILE = 512
FLASH_STRIPS = 2
LOG2E = 1.4426950408889634
MASKED = -2.0 ** 100
M_INIT = -2.0 ** 98
FEAT_BLOCKS = 64
FEAT_ALIBI = FEAT_BLOCKS
V7X_VMEM_BUDGET = 56 * 1024 * 1024

F32 = jnp.float32
BF16 = jnp.bfloat16
U32 = jnp.uint32
I32 = jnp.int32


def _params(semantics, vmem_bytes):
    return pltpu.CompilerParams(dimension_semantics=semantics,
                                vmem_limit_bytes=min(int(vmem_bytes), V7X_VMEM_BUDGET))


def _rms(xf, g):
    ms = jnp.mean(xf * xf, axis=-1, keepdims=True)
    return xf * lax.rsqrt(ms + NORM_EPS) * g


def _dot(a, b):
    return jnp.dot(a, b, preferred_element_type=F32)


def _pack_halves(a, b):
    ai = lax.bitcast_convert_type(a.astype(BF16).astype(F32), U32)
    bi = lax.bitcast_convert_type(b.astype(BF16).astype(F32), U32)
    return ai | (bi >> 16)


def _unpack_halves(w):
    hi = lax.bitcast_convert_type(w & jnp.uint32(0xFFFF0000), F32)
    lo = lax.bitcast_convert_type(w << 16, F32)
    return hi, lo


def _store_row_tiles(ref, val):
    rows = val.shape[0]
    for s in range(SUBLANES):
        ref[pl.ds(s, rows, stride=SUBLANES), :] = val[:, s * LANES:(s + 1) * LANES]


def _load_row_tiles(ref):
    rows = ref.shape[0] // SUBLANES
    return jnp.concatenate([ref[pl.ds(s, rows, stride=SUBLANES), :] for s in range(SUBLANES)], axis=1)


def _row_tile(ref, r):
    return ref.at[pl.ds(pl.multiple_of(r * SUBLANES, SUBLANES), SUBLANES)]


def _inproj_kernel(x_ref, g_ref, w_ref, cos_ref, sin_ref, feat_ref,
                   qm_ref, kt_ref, vm_ref, kmean_ref, cq_ref, ckv_ref, kr_ref, *, q_scale):
    h = _rms(x_ref[...], g_ref[...]).astype(BF16)
    w = MOBA_HEADS * MOBA_HEAD_DIM
    dh = MOBA_HEAD_DIM
    tm = h.shape[0]

    def mm(lo, hi):
        return _dot(h, w_ref[:, lo:hi])

    qm_ref[...] = (mm(0, w) * q_scale).astype(BF16)
    k = mm(w, 2 * w)
    for b in range(tm // MOBA_BLOCK):
        kmean_ref[b] = jnp.mean(k[b * MOBA_BLOCK:(b + 1) * MOBA_BLOCK, :], axis=0, keepdims=True)
    v = mm(2 * w, 3 * w).astype(BF16)
    ones = jnp.ones((tm, QK_WIDTH - dh), BF16)
    for hh in range(MOBA_HEADS):
        kt_ref[hh, 0, :dh, :] = k[:, hh * dh:(hh + 1) * dh].T.astype(BF16)
        kt_ref[hh, 0, dh:, :] = feat_ref[0]
        vm_ref[:, hh * QK_WIDTH:hh * QK_WIDTH + dh] = v[:, hh * dh:(hh + 1) * dh]
        vm_ref[:, hh * QK_WIDTH + dh:(hh + 1) * QK_WIDTH] = ones
    c0 = 3 * w
    cq_ref[...] = mm(c0, c0 + MLA_Q_RANK)
    c1 = c0 + MLA_Q_RANK
    ckv_ref[...] = mm(c1, c1 + MLA_KV_RANK)
    c2 = c1 + MLA_KV_RANK
    kr = mm(c2, c2 + QK_WIDTH) * cos_ref[...] + mm(c2 + QK_WIDTH, c2 + 2 * QK_WIDTH) * sin_ref[...]
    kr_ref[...] = kr.astype(BF16)


def _in_proj(x2, g, w_all, cos_t, sin_t, feat, seq, tm):
    t, d = x2.shape
    w = MOBA_HEADS * MOBA_HEAD_DIM
    hw = MOBA_HEADS * QK_WIDTH
    nc = w_all.shape[1]
    n_pos = seq // tm
    nb = tm // MOBA_BLOCK
    row = lambda i: (i, 0)
    const = lambda i: (0, 0)
    pos = lambda i: (i % n_pos, 0)
    out_shape = [jax.ShapeDtypeStruct((t, w), BF16),
                 jax.ShapeDtypeStruct((MOBA_HEADS, t // tm, QK_WIDTH, tm), BF16),
                 jax.ShapeDtypeStruct((t, hw), BF16),
                 jax.ShapeDtypeStruct((t // MOBA_BLOCK, 1, w), F32),
                 jax.ShapeDtypeStruct((t, MLA_Q_RANK), F32), jax.ShapeDtypeStruct((t, MLA_KV_RANK), F32),
                 jax.ShapeDtypeStruct((t, QK_WIDTH), BF16)]
    out_specs = [pl.BlockSpec((tm, w), row),
                 pl.BlockSpec((MOBA_HEADS, 1, QK_WIDTH, tm), lambda i: (0, i, 0, 0)),
                 pl.BlockSpec((tm, hw), row),
                 pl.BlockSpec((nb, 1, w), lambda i: (i, 0, 0)),
                 pl.BlockSpec((tm, MLA_Q_RANK), row), pl.BlockSpec((tm, MLA_KV_RANK), row),
                 pl.BlockSpec((tm, QK_WIDTH), row)]
    vmem = (2 * tm * d * 4 + d * nc * 2 + 2 * tm * (w * 2 + 2 * hw * 2 + 2 * 512 * 4 + 256 * 2)
            + 4 * tm * QK_WIDTH * 4 + tm * d * 2 + 3 * tm * w * 4 + (4 << 20))
    return pl.pallas_call(
        functools.partial(_inproj_kernel, q_scale=MOBA_HEAD_DIM ** -0.5 * LOG2E),
        grid=(t // tm,),
        in_specs=[pl.BlockSpec((tm, d), row), pl.BlockSpec((1, d), const),
                  pl.BlockSpec((d, nc), const, pipeline_mode=pl.Buffered(1)),
                  pl.BlockSpec((tm, QK_WIDTH), pos), pl.BlockSpec((tm, QK_WIDTH), pos),
                  pl.BlockSpec((1, QK_WIDTH - MOBA_HEAD_DIM, tm), lambda i: (i % n_pos, 0, 0))],
        out_specs=out_specs, out_shape=out_shape,
        compiler_params=_params(("parallel",), vmem), name="in_proj",
    )(x2, g, w_all, cos_t, sin_t, feat)


def _mlaup_kernel(cq_ref, ckv_ref, gq_ref, gkv_ref, wqa_ref, wqb_ref, wk_ref, wv_ref,
                  kr_ref, cos_ref, sin_ref, q_ref, kt_ref, v_ref, *, q_scale):
    nq = _rms(cq_ref[...], gq_ref[...]).astype(BF16)
    nkv = _rms(ckv_ref[...], gkv_ref[...]).astype(BF16)
    cos = cos_ref[...]
    sin = sin_ref[...]
    kr = kr_ref[...].astype(F32)
    ones = jnp.ones((nq.shape[0], QK_WIDTH - MLA_V_DIM), BF16)
    for hh in range(MLA_HEADS):
        cols = slice(hh * QK_WIDTH, (hh + 1) * QK_WIDTH)
        q = _dot(nq, wqa_ref[:, cols]) * cos + _dot(nq, wqb_ref[:, cols]) * sin
        q_ref[:, cols] = (q * q_scale).astype(BF16)
        kt_ref[hh, 0] = (_dot(nkv, wk_ref[:, cols]) + kr).T.astype(BF16)
        v_ref[:, hh * QK_WIDTH:hh * QK_WIDTH + MLA_V_DIM] = _dot(
            nkv, wv_ref[:, hh * MLA_V_DIM:(hh + 1) * MLA_V_DIM]).astype(BF16)
        v_ref[:, hh * QK_WIDTH + MLA_V_DIM:(hh + 1) * QK_WIDTH] = ones


def _mla_up(cq, ckv, gq, gkv, wqa, wqb, wk, wv, kr, cos_t, sin_t, seq, tm):
    t = cq.shape[0]
    n_pos = seq // tm
    row = lambda i: (i, 0)
    const = lambda i: (0, 0)
    pos = lambda i: (i % n_pos, 0)
    hq = MLA_HEADS * QK_WIDTH
    hv = MLA_HEADS * MLA_V_DIM
    scale = (MLA_NOPE_DIM + MLA_ROPE_DIM) ** -0.5 * LOG2E
    vmem = (4 * tm * 512 * 4 + 2 * (3 * 512 * hq * 2 + 512 * hv * 2) + 8 * tm * QK_WIDTH * 4
            + 2 * tm * 3 * hq * 2 + (8 << 20))
    return pl.pallas_call(
        functools.partial(_mlaup_kernel, q_scale=scale),
        grid=(t // tm,),
        in_specs=[pl.BlockSpec((tm, MLA_Q_RANK), row), pl.BlockSpec((tm, MLA_KV_RANK), row),
                  pl.BlockSpec((1, MLA_Q_RANK), const), pl.BlockSpec((1, MLA_KV_RANK), const),
                  pl.BlockSpec((MLA_Q_RANK, hq), const), pl.BlockSpec((MLA_Q_RANK, hq), const),
                  pl.BlockSpec((MLA_KV_RANK, hq), const), pl.BlockSpec((MLA_KV_RANK, hv), const),
                  pl.BlockSpec((tm, QK_WIDTH), row),
                  pl.BlockSpec((tm, QK_WIDTH), pos), pl.BlockSpec((tm, QK_WIDTH), pos)],
        out_specs=[pl.BlockSpec((tm, hq), row),
                   pl.BlockSpec((MLA_HEADS, 1, QK_WIDTH, tm), lambda i: (0, i, 0, 0)),
                   pl.BlockSpec((tm, hq), row)],
        out_shape=[jax.ShapeDtypeStruct((t, hq), BF16),
                   jax.ShapeDtypeStruct((MLA_HEADS, t // tm, QK_WIDTH, tm), BF16),
                   jax.ShapeDtypeStruct((t, hq), BF16)],
        compiler_params=_params(("parallel",), vmem), name="mla_up",
    )(cq, ckv, gq, gkv, wqa, wqb, wk, wv, kr, cos_t, sin_t)


def _flash_stages(bufs, q_ref, *, pv=None, softmax=None, scores=None):
    s_sc, p_sc, alpha_sc, m_sc, acc_sc = bufs
    rows = s_sc.shape[0] // FLASH_STRIPS
    for r in range(FLASH_STRIPS):
        sl = slice(r * rows, (r + 1) * rows)
        if pv is not None:
            acc_sc[sl] = alpha_sc[sl] * acc_sc[sl] + _dot(p_sc[sl], pv)
        if softmax is not None:
            s = s_sc[sl]
            if softmax is not True:
                s = jnp.where(softmax[sl], s, MASKED)
            m_old = m_sc[sl]
            m_new = jnp.maximum(m_old, jnp.max(s, axis=-1, keepdims=True))
            p_sc[sl] = jnp.exp2(s - m_new).astype(BF16)
            alpha_sc[sl] = jnp.exp2(m_old - m_new)
            m_sc[sl] = m_new
        if scores is not None:
            s_sc[sl] = _dot(q_ref[sl], scores)


def _flash_loop(i, q_ref, kt_ref, v_ref, o_ref, bufs, diag_mask, *, tile, dv):
    s_sc, p_sc, alpha_sc, m_sc, acc_sc = bufs
    m_sc[...] = jnp.full_like(m_sc, M_INIT)
    acc_sc[...] = jnp.zeros_like(acc_sc)

    def v_tile(j):
        return v_ref[pl.ds(pl.multiple_of(j * tile, tile), tile), :]

    _flash_stages(bufs, q_ref, scores=kt_ref[0, 0])

    @pl.when(i >= 1)
    def _():
        _flash_stages(bufs, q_ref, softmax=True, scores=kt_ref[0, 1])

    def steady(k, carry):
        _flash_stages(bufs, q_ref, pv=v_tile(k - 2), softmax=True, scores=kt_ref[0, k])
        return carry

    lax.fori_loop(2, i + 1, steady, 0)

    @pl.when(i >= 1)
    def _():
        _flash_stages(bufs, q_ref, pv=v_tile(i - 1), softmax=diag_mask)

    @pl.when(i == 0)
    def _():
        _flash_stages(bufs, q_ref, softmax=diag_mask)

    _flash_stages(bufs, q_ref, pv=v_tile(i))
    acc = acc_sc[...]
    o_ref[...] = acc[:, :dv] / acc[:, dv:2 * dv]


def _mla_attn_kernel(q_ref, kt_ref, v_ref, w_ref, o_ref, wb_ref, *bufs, tile):
    wb_ref[...] = w_ref[...].astype(BF16)
    row = lax.broadcasted_iota(I32, (tile, tile), 0)
    col = lax.broadcasted_iota(I32, (tile, tile), 1)
    _flash_loop(pl.program_id(2), q_ref, kt_ref, v_ref, o_ref, bufs, row >= col, tile=tile, dv=MLA_V_DIM)


def _attn_call(kernel, name, q_spec, extra_specs, scratch, args, w_f32, batch, heads, seq, tile, dv):
    t = batch * seq
    nq = seq // tile
    steps = batch * heads * nq
    w2d = w_f32.reshape(-1, w_f32.shape[-1])
    assert w2d.shape[0] % steps == 0
    w_rows = w2d.shape[0] // steps
    w_spec = pl.BlockSpec((w_rows, w2d.shape[1]), lambda b, h, i: ((b * heads + h) * nq + i, 0))
    vmem = (2 * 2 * seq * QK_WIDTH * 2 + 10 * tile * tile * 4 + 10 * tile * QK_WIDTH * 4
            + 3 * w_rows * w2d.shape[1] * 6 + (4 << 20))
    out, wb = pl.pallas_call(
        kernel,
        grid=(batch, heads, nq),
        in_specs=[q_spec,
                  pl.BlockSpec((1, nq, QK_WIDTH, tile), lambda b, h, i: (h, b, 0, 0)),
                  pl.BlockSpec((seq, QK_WIDTH), lambda b, h, i: (b, h))] + extra_specs + [w_spec],
        out_specs=[pl.BlockSpec((tile, dv), lambda b, h, i: (b * nq + i, h)), w_spec],
        out_shape=[jax.ShapeDtypeStruct((t, heads * dv), F32), jax.ShapeDtypeStruct(w2d.shape, BF16)],
        scratch_shapes=scratch + [pltpu.VMEM((tile, tile), F32), pltpu.VMEM((tile, tile), BF16),
                                  pltpu.VMEM((tile, 1), F32), pltpu.VMEM((tile, 1), F32),
                                  pltpu.VMEM((tile, QK_WIDTH), F32)],
        compiler_params=_params(("parallel", "parallel", "arbitrary"), vmem), name=name,
    )(*args, w2d)
    return out, wb.reshape(w_f32.shape)


def _mla_attn(qc, kt, v, w_f32, batch, seq, tile):
    nq = seq // tile
    return _attn_call(functools.partial(_mla_attn_kernel, tile=tile), "mla_attn",
                      pl.BlockSpec((tile, QK_WIDTH), lambda b, h, i: (b * nq + i, h)), [], [],
                      (qc, kt, v), w_f32, batch, MLA_HEADS, seq, tile, MLA_V_DIM)


def _moba_attn_kernel(q_ref, kt_ref, v_ref, kmt_ref, aq_ref, w_ref, o_ref, wb_ref, qa_sc, *bufs, tile):
    wb_ref[...] = w_ref[...].astype(BF16)
    u = pl.program_id(2)
    blk = MOBA_BLOCK
    q = q_ref[...]
    km = kmt_ref[0]
    km_hi = km.astype(BF16)
    km_lo = (km - km_hi.astype(F32)).astype(BF16)
    gate = _dot(q, km_hi) + _dot(q, km_lo)
    lane = lax.broadcasted_iota(I32, (tile, LANES), 1)
    own = u * (tile // blk) + lax.broadcasted_iota(I32, (tile, LANES), 0) // blk
    g = jnp.where(lane < own, gate, -jnp.inf)
    keep = lane == own
    for _ in range(MOBA_TOPK):
        mx = jnp.max(g, axis=-1, keepdims=True)
        cand = jnp.where((g == mx) & (mx > -jnp.inf), lane, LANES)
        pick = lane == jnp.min(cand, axis=-1, keepdims=True)
        keep = keep | pick
        g = jnp.where(pick, -jnp.inf, g)
    feat = jnp.where(lane < FEAT_BLOCKS, jnp.where(keep, 0.0, 1.0), aq_ref[0])
    qa_sc[:, :MOBA_HEAD_DIM] = q
    qa_sc[:, MOBA_HEAD_DIM:] = feat.astype(BF16)

    row = lax.broadcasted_iota(I32, (tile, tile), 0)
    col = lax.broadcasted_iota(I32, (tile, tile), 1)
    causal_in_own_block = (row // blk != col // blk) | (col <= row)
    _flash_loop(u, qa_sc, kt_ref, v_ref, o_ref, bufs, causal_in_own_block, tile=tile, dv=MOBA_HEAD_DIM)


def _moba_attn(qm, kt, vm, kmt, aq, w_f32, batch, seq, tile):
    nq = seq // tile
    dh = MOBA_HEAD_DIM
    return _attn_call(functools.partial(_moba_attn_kernel, tile=tile), "moba_attn",
                      pl.BlockSpec((tile, dh), lambda b, h, i: (b * nq + i, h)),
                      [pl.BlockSpec((1, dh, LANES), lambda b, h, i: (b * MOBA_HEADS + h, 0, 0)),
                       pl.BlockSpec((1, 1, LANES), lambda b, h, i: (h, 0, 0))],
                      [pltpu.VMEM((tile, QK_WIDTH), BF16)],
                      (qm, kt, vm, kmt, aq), w_f32, batch, MOBA_HEADS, seq, tile, dh)


def _outproj_kernel(om_ref, ol_ref, gm_ref, gl_ref, wo_ref, x_ref, gf_ref, wrh_ref, wrl_ref, br_ref,
                    x1_ref, hp_ref, lg_ref):
    wm = om_ref.shape[1]
    a = _rms(om_ref[...], gm_ref[...]).astype(BF16)
    b = _rms(ol_ref[...], gl_ref[...]).astype(BF16)
    x1 = x_ref[...] + _dot(a, wo_ref[:wm, :]) + _dot(b, wo_ref[wm:, :])
    x1_ref[...] = x1
    h2 = _rms(x1, gf_ref[...])
    half = h2.shape[1] // 2
    _store_row_tiles(hp_ref, _pack_halves(h2[:, :half], h2[:, half:]))
    h_hi = h2.astype(BF16)
    h_lo = (h2 - h_hi.astype(F32)).astype(BF16)
    lg_ref[...] = (_dot(h_hi, wrh_ref[...]) + _dot(h_lo, wrh_ref[...]) + _dot(h_hi, wrl_ref[...])
                   + br_ref[...])


def _out_proj(om, ol, gm, gl, wo, x2, gf, wr_hi, wr_lo, br, tm):
    t, d = x2.shape
    wm = om.shape[1]
    wl = ol.shape[1]
    row = lambda i: (i, 0)
    const = lambda i: (0, 0)
    vmem = 2 * (wm + wl) * d * 2 + 2 * tm * (wm + wl + 2 * d) * 4 + 2 * tm * d * 2 + 6 * tm * d * 4 + (4 << 20)
    return pl.pallas_call(
        _outproj_kernel,
        grid=(t // tm,),
        in_specs=[pl.BlockSpec((tm, wm), row), pl.BlockSpec((tm, wl), row),
                  pl.BlockSpec((1, wm), const), pl.BlockSpec((1, wl), const),
                  pl.BlockSpec((wm + wl, d), const), pl.BlockSpec((tm, d), row), pl.BlockSpec((1, d), const),
                  pl.BlockSpec((d, LANES), const), pl.BlockSpec((d, LANES), const), pl.BlockSpec((1, LANES), const)],
        out_specs=[pl.BlockSpec((tm, d), row), pl.BlockSpec((tm * SUBLANES, LANES), row),
                   pl.BlockSpec((tm, LANES), row)],
        out_shape=[jax.ShapeDtypeStruct((t, d), F32), jax.ShapeDtypeStruct((t * SUBLANES, LANES), U32),
                   jax.ShapeDtypeStruct((t, LANES), F32)],
        compiler_params=_params(("parallel",), vmem), name="out_proj",
    )(om, ol, gm, gl, wo, x2, gf, wr_hi, wr_lo, br)


def _router_kernel(lg_ref, idx_ref, gate_ref, pos_ref, cnt_ref, *, tm):
    i = pl.program_id(0)

    @pl.when(i == 0)
    def _():
        cnt_ref[...] = jnp.zeros_like(cnt_ref)

    lane = lax.broadcasted_iota(I32, (tm, LANES), 1)
    work = jnp.where(lane < N_EXPERTS, lg_ref[...], -jnp.inf)
    vals, firsts, picks = [], [], []
    for _ in range(TOP_K):
        mx = jnp.max(work, axis=-1, keepdims=True)
        first = jnp.min(jnp.where(work == mx, lane, LANES), axis=-1, keepdims=True)
        pick = lane == first
        work = jnp.where(pick, -jnp.inf, work)
        vals.append(mx)
        firsts.append(first)
        picks.append(pick)
    es = [jnp.exp(v - vals[0]) for v in vals]
    denom = es[0] + es[1] + es[2] + es[3]
    chosen = jnp.zeros((tm, LANES), F32)
    for p in picks:
        chosen = jnp.where(p, 1.0, chosen)
    r = lax.broadcasted_iota(I32, (tm, tm), 0)
    cc = lax.broadcasted_iota(I32, (tm, tm), 1)
    before = (r > cc).astype(BF16)
    prior = _dot(before, chosen.astype(BF16)) + cnt_ref[...]
    lane4 = lax.broadcasted_iota(I32, (tm, TOP_K), 1)
    idx = jnp.zeros((tm, TOP_K), I32)
    gate = jnp.zeros((tm, TOP_K), F32)
    pos = jnp.zeros((tm, TOP_K), I32)
    for j in range(TOP_K):
        pj = jnp.sum(jnp.where(picks[j], prior, 0.0), axis=-1, keepdims=True)
        idx = jnp.where(lane4 == j, firsts[j], idx)
        gate = jnp.where(lane4 == j, es[j] / denom, gate)
        pos = jnp.where(lane4 == j, pj.astype(I32), pos)
    idx_ref[...] = idx
    gate_ref[...] = gate
    pos_ref[...] = pos
    cnt_ref[...] += jnp.sum(chosen, axis=0, keepdims=True)


def _router(logits, tm):
    t = logits.shape[0]
    row = lambda i: (i, 0)
    vmem = 4 * tm * tm * 4 + 24 * tm * LANES * 4 + (4 << 20)
    return pl.pallas_call(
        functools.partial(_router_kernel, tm=tm),
        grid=(t // tm,),
        in_specs=[pl.BlockSpec((tm, LANES), row)],
        out_specs=[pl.BlockSpec((tm, TOP_K), row)] * 3 + [pl.BlockSpec((1, LANES), lambda i: (0, 0))],
        out_shape=[jax.ShapeDtypeStruct((t, TOP_K), I32), jax.ShapeDtypeStruct((t, TOP_K), F32),
                   jax.ShapeDtypeStruct((t, TOP_K), I32), jax.ShapeDtypeStruct((1, LANES), F32)],
        compiler_params=_params(("arbitrary",), vmem), name="router",
    )(logits)


def _moe_up_kernel(be_ref, nu_ref, src_ref, hp_ref, w1_ref, b1_ref, act_ref, xs_sc, sem, *, fc):
    i = pl.program_id(0)
    blk = act_ref.shape[0]
    slot = i % 2

    def row_copy(block, r, s):
        return pltpu.make_async_copy(_row_tile(hp_ref, src_ref[block * blk + r]),
                                     _row_tile(xs_sc.at[s], r), sem.at[s])

    def start_block(block, s):
        def body(r, carry):
            row_copy(block, r, s).start()
            return carry
        lax.fori_loop(0, blk, body, 0, unroll=8)

    def wait_block(block, s):
        def body(r, carry):
            row_copy(block, r, s).wait()
            return carry
        lax.fori_loop(0, blk, body, 0, unroll=8)

    @pl.when(i == 0)
    def _():
        start_block(0, 0)

    @pl.when(i < nu_ref[0])
    def _():
        wait_block(i, slot)
        hi, lo = _unpack_halves(_load_row_tiles(xs_sc.at[slot]))
        a = hi.astype(BF16)
        b = lo.astype(BF16)
        half = a.shape[1]
        d_ff = act_ref.shape[1]
        n_chunks = d_ff // fc
        nxt = jnp.minimum(i + 1, pl.num_programs(0) - 1)

        def pre(c0):
            return (_dot(a, w1_ref[0, :half, c0:c0 + fc]) + _dot(b, w1_ref[0, half:, c0:c0 + fc])
                    + b1_ref[0, :, c0:c0 + fc])

        per = blk // (2 * n_chunks)
        for c in range(n_chunks):
            glu = jnp.minimum(pre(c * fc), SWIGLU_LIMIT)
            for r in range(2 * c * per, (2 * c + 1) * per):
                row_copy(nxt, r, 1 - slot).start()
            lin = jnp.clip(pre(d_ff + c * fc), -SWIGLU_LIMIT, SWIGLU_LIMIT)
            act = glu * jax.nn.sigmoid(SWIGLU_ALPHA * glu) * (lin + 1.0)
            act_ref[:, c * fc:(c + 1) * fc] = act.astype(BF16)
            for r in range((2 * c + 1) * per, (2 * c + 2) * per):
                row_copy(nxt, r, 1 - slot).start()

    @pl.when(i == nu_ref[0])
    def _():
        wait_block(i, slot)

    @pl.when(i >= nu_ref[0])
    def _():
        act_ref[...] = jnp.zeros_like(act_ref)


def _moe_up(block_expert, n_used, src, hp, w1, b1):
    n_rows = src.shape[0]
    half = SUBLANES * hp.shape[1]
    e, d, f2 = w1.shape
    assert d == 2 * half
    d_ff = f2 // 2
    blk = EXPERT_BLOCK
    n_blocks = n_rows // blk
    fc = 512
    live = lambda i, be, nu: jnp.minimum(i, nu[0] - 1)
    grid_spec = pltpu.PrefetchScalarGridSpec(
        num_scalar_prefetch=3, grid=(n_blocks,),
        in_specs=[pl.BlockSpec(memory_space=pl.ANY),
                  pl.BlockSpec((1, d, f2), lambda i, be, nu, src: (be[live(i, be, nu)], 0, 0)),
                  pl.BlockSpec((1, 1, f2), lambda i, be, nu, src: (be[live(i, be, nu)], 0, 0))],
        out_specs=pl.BlockSpec((blk, d_ff), lambda i, be, nu, src: (i, 0)),
        scratch_shapes=[pltpu.VMEM((2, blk * SUBLANES, LANES), U32), pltpu.SemaphoreType.DMA((2,))])
    vmem = 2 * d * f2 * 2 + 2 * blk * half * 4 + 2 * blk * d_ff * 2 + 8 * blk * fc * 4 + 2 * blk * d * 2 + (4 << 20)
    return pl.pallas_call(
        functools.partial(_moe_up_kernel, fc=fc),
        grid_spec=grid_spec,
        out_shape=jax.ShapeDtypeStruct((n_rows, d_ff), BF16),
        compiler_params=_params(("arbitrary",), vmem), name="moe_up",
    )(block_expert, n_used, src, hp, w1, b1)


def _moe_down_kernel(be_ref, nu_ref, a_ref, w2_ref, b2_ref, y_ref):
    i = pl.program_id(0)

    @pl.when(i < nu_ref[0])
    def _():
        y = _dot(a_ref[...], w2_ref[0]) + b2_ref[0]
        half = y.shape[1] // 2
        _store_row_tiles(y_ref, _pack_halves(y[:, :half], y[:, half:]))

    @pl.when(i >= nu_ref[0])
    def _():
        y_ref[...] = jnp.zeros_like(y_ref)


def _moe_down(block_expert, n_used, act, w2, b2):
    n_rows, d_ff = act.shape
    e, _, d = w2.shape
    blk = EXPERT_BLOCK
    n_blocks = n_rows // blk
    live = lambda i, be, nu: jnp.minimum(i, nu[0] - 1)
    grid_spec = pltpu.PrefetchScalarGridSpec(
        num_scalar_prefetch=2, grid=(n_blocks,),
        in_specs=[pl.BlockSpec((blk, d_ff), lambda i, be, nu: (live(i, be, nu), 0)),
                  pl.BlockSpec((1, d_ff, d), lambda i, be, nu: (be[live(i, be, nu)], 0, 0)),
                  pl.BlockSpec((1, 1, d), lambda i, be, nu: (be[live(i, be, nu)], 0, 0))],
        out_specs=pl.BlockSpec((blk * SUBLANES, LANES), lambda i, be, nu: (i, 0)))
    vmem = 2 * d_ff * d * 2 + 2 * blk * d_ff * 2 + 2 * blk * d * 2 + 4 * blk * d * 4 + (4 << 20)
    return pl.pallas_call(
        _moe_down_kernel,
        grid_spec=grid_spec,
        out_shape=jax.ShapeDtypeStruct((n_rows * SUBLANES, LANES), U32),
        compiler_params=_params(("arbitrary",), vmem), name="moe_down",
    )(block_expert, n_used, act, w2, b2)


def _combine_kernel(dest_ref, gate_ref, x1_ref, gfin_ref, yb_ref, o_ref, rows_sc, sem, *, tm):
    i = pl.program_id(0)
    slot = i % 2

    def row_copy(tile, r, j, s):
        return pltpu.make_async_copy(_row_tile(yb_ref, dest_ref[(tile * tm + r) * TOP_K + j]),
                                     _row_tile(rows_sc.at[s, j], r), sem.at[s])

    def start_tile(tile, s):
        def body(r, carry):
            for j in range(TOP_K):
                row_copy(tile, r, j, s).start()
            return carry
        lax.fori_loop(0, tm, body, 0, unroll=4)

    def wait_tile(tile, s):
        def body(r, carry):
            for j in range(TOP_K):
                row_copy(tile, r, j, s).wait()
            return carry
        lax.fori_loop(0, tm, body, 0, unroll=4)

    @pl.when(i == 0)
    def _():
        start_tile(0, 0)

    @pl.when(i + 1 < pl.num_programs(0))
    def _():
        start_tile(i + 1, 1 - slot)

    wait_tile(i, slot)

    x1 = x1_ref[...]
    half = x1.shape[1] // 2
    gates = gate_ref[...]
    left = x1[:, :half]
    right = x1[:, half:]
    for j in range(TOP_K):
        hi, lo = _unpack_halves(_load_row_tiles(rows_sc.at[slot, j]))
        gj = gates[:, j:j + 1]
        left = left + gj * hi
        right = right + gj * lo
    ms = (jnp.sum(left * left, axis=-1, keepdims=True) + jnp.sum(right * right, axis=-1, keepdims=True)) / (2 * half)
    inv = lax.rsqrt(ms + NORM_EPS)
    g = gfin_ref[...]
    o_ref[:, :half] = left * inv * g[:, :half]
    o_ref[:, half:] = right * inv * g[:, half:]


def _combine(dest, gates, x1, gfin, yb, tm):
    t, d = x1.shape
    half = d // 2
    grid_spec = pltpu.PrefetchScalarGridSpec(
        num_scalar_prefetch=1, grid=(t // tm,),
        in_specs=[pl.BlockSpec((tm, TOP_K), lambda i, dst: (i, 0)),
                  pl.BlockSpec((tm, d), lambda i, dst: (i, 0)),
                  pl.BlockSpec((1, d), lambda i, dst: (0, 0)),
                  pl.BlockSpec(memory_space=pl.ANY)],
        out_specs=pl.BlockSpec((tm, d), lambda i, dst: (i, 0)),
        scratch_shapes=[pltpu.VMEM((2, TOP_K, tm * SUBLANES, LANES), U32), pltpu.SemaphoreType.DMA((2,))])
    vmem = 2 * TOP_K * tm * half * 4 + 4 * tm * d * 4 + 6 * tm * d * 4 + 2 * tm * LANES * 4 + (4 << 20)
    return pl.pallas_call(
        functools.partial(_combine_kernel, tm=tm),
        grid_spec=grid_spec,
        out_shape=jax.ShapeDtypeStruct((t, d), F32),
        compiler_params=_params(("arbitrary",), vmem), name="combine",
    )(dest, gates, x1, gfin, yb)


def _rope_tables(seq):
    inv = 1.0 / (ROPE_THETA ** (jnp.arange(0, MLA_ROPE_DIM, 2, dtype=F32) / MLA_ROPE_DIM))
    ang = jnp.arange(seq, dtype=F32)[:, None] * inv[None, :]
    ang = jnp.concatenate([ang, ang], axis=-1)
    ones = jnp.ones((seq, MLA_NOPE_DIM), F32)
    zeros = jnp.zeros((seq, QK_WIDTH - MLA_NOPE_DIM - MLA_ROPE_DIM), F32)
    cos_t = jnp.concatenate([ones, jnp.cos(ang), zeros], axis=-1)
    sin_t = jnp.concatenate([0.0 * ones, jnp.sin(ang), zeros], axis=-1)
    return cos_t, sin_t


def _bf16_pieces(x, n=3):
    out = []
    for _ in range(n):
        p = float(np.asarray(x, dtype=np.float32).astype(jnp.bfloat16).astype(np.float32))
        out.append(p)
        x = x - p
    return out


def _moba_features(seq, tile):
    pos = np.arange(seq)
    blk, off = pos // MOBA_BLOCK, pos % MOBA_BLOCK
    assert seq // MOBA_BLOCK <= FEAT_BLOCKS
    feat = np.zeros((QK_WIDTH - MOBA_HEAD_DIM, seq), np.float32)
    feat[:FEAT_BLOCKS] = np.where(blk[None, :] == np.arange(FEAT_BLOCKS)[:, None], MASKED, 0.0)
    feat[FEAT_ALIBI:FEAT_ALIBI + 3] = blk[None, :]
    feat[FEAT_ALIBI + 3:FEAT_ALIBI + 6] = off[None, :]
    feat = feat.reshape(feat.shape[0], seq // tile, tile).transpose(1, 0, 2)
    aq = np.zeros((MOBA_HEADS, 1, LANES), np.float32)
    for h in range(MOBA_HEADS):
        slope = 2.0 ** (-8.0 * (h + 1) / MOBA_HEADS)
        aq[h, 0, FEAT_ALIBI:FEAT_ALIBI + 3] = _bf16_pieces(slope * LOG2E * MOBA_BLOCK)
        aq[h, 0, FEAT_ALIBI + 3:FEAT_ALIBI + 6] = _bf16_pieces(slope * LOG2E)
    return jnp.asarray(feat, BF16), jnp.asarray(aq, F32)


def _rot_cols(w):
    half = w.shape[-1] // 2
    return jnp.concatenate([-w[..., half:], w[..., :half]], axis=-1)


def _rope_slot(w):
    k = w.shape[0]
    return jnp.concatenate([jnp.zeros((k, MLA_NOPE_DIM), w.dtype), w,
                            jnp.zeros((k, QK_WIDTH - MLA_NOPE_DIM - MLA_ROPE_DIM), w.dtype)], axis=-1)


def _layer(x2, batch, seq, norm_attn_g, w_in, moba_out_g, q_a_norm_g, kv_a_norm_g, w_uq, w_ukv, mla_out_g,
           w_o, norm_ffn_g, w_router, b_router, w1, b1, w2, b2, final_g):
    t, d = x2.shape
    wm = MOBA_HEADS * MOBA_HEAD_DIM
    tile = ATT_TILE
    assert seq % tile == 0 and tile % MOBA_BLOCK == 0
    cos_t, sin_t = _rope_tables(seq)
    feat, aq = _moba_features(seq, tile)

    c_kr = 3 * wm + MLA_Q_RANK + MLA_KV_RANK
    w_kr = w_in[:, c_kr:c_kr + MLA_ROPE_DIM]
    w_all = jnp.concatenate([w_in[:, :c_kr], _rope_slot(w_kr), _rope_slot(_rot_cols(w_kr))], axis=1).astype(BF16)
    wq = w_uq.reshape(MLA_Q_RANK, MLA_HEADS, MLA_NOPE_DIM + MLA_ROPE_DIM)
    zq = jnp.zeros((MLA_Q_RANK, MLA_HEADS, QK_WIDTH - MLA_NOPE_DIM - MLA_ROPE_DIM), F32)
    wqa = jnp.concatenate([wq, zq], axis=-1).reshape(MLA_Q_RANK, -1).astype(BF16)
    wqb = jnp.concatenate([jnp.zeros((MLA_Q_RANK, MLA_HEADS, MLA_NOPE_DIM), F32),
                           _rot_cols(wq[..., MLA_NOPE_DIM:]), zq], axis=-1).reshape(MLA_Q_RANK, -1).astype(BF16)
    wkv = w_ukv.reshape(MLA_KV_RANK, MLA_HEADS, MLA_NOPE_DIM + MLA_V_DIM)
    wk = jnp.concatenate([wkv[..., :MLA_NOPE_DIM],
                          jnp.zeros((MLA_KV_RANK, MLA_HEADS, QK_WIDTH - MLA_NOPE_DIM), F32)],
                         axis=-1).reshape(MLA_KV_RANK, -1).astype(BF16)
    wv = wkv[..., MLA_NOPE_DIM:].reshape(MLA_KV_RANK, -1).astype(BF16)
    wr = jnp.pad(w_router, ((0, 0), (0, LANES - N_EXPERTS)))
    wr_hi = wr.astype(BF16)
    wr_lo = (wr - wr_hi.astype(F32)).astype(BF16)
    br = jnp.pad(b_router, (0, LANES - N_EXPERTS))[None, :]

    qm, ktm, vm, kmean, cq, ckv, kr = _in_proj(x2, norm_attn_g[None, :], w_all, cos_t, sin_t, feat, seq, tm=tile)
    qc, ktl, vl = _mla_up(cq, ckv, q_a_norm_g[None, :], kv_a_norm_g[None, :], wqa, wqb, wk, wv, kr,
                          cos_t, sin_t, seq, tm=tile)
    n_blk = seq // MOBA_BLOCK
    kmt = kmean.reshape(batch, n_blk, MOBA_HEADS, MOBA_HEAD_DIM).transpose(0, 2, 3, 1)
    kmt = jnp.pad(kmt, ((0, 0), (0, 0), (0, 0), (0, LANES - n_blk))).reshape(batch * MOBA_HEADS, MOBA_HEAD_DIM, LANES)
    o_moba, w2b = _moba_attn(qm, ktm, vm, kmt, aq, w2, batch, seq, tile)
    o_mla, w1b = _mla_attn(qc, ktl, vl, w1, batch, seq, tile)
    x1, hp, logits = _out_proj(o_moba, o_mla, moba_out_g[None, :], mla_out_g[None, :], w_o.astype(BF16), x2,
                               norm_ffn_g[None, :], wr_hi, wr_lo, br, tm=256)

    idx, gates, pos, counts = _router(logits, tm=min(512, t))
    blk = EXPERT_BLOCK
    n_blocks = -(-(t * TOP_K) // blk) + N_EXPERTS
    counts = counts[0, :N_EXPERTS].astype(I32)
    padded = ((counts + blk - 1) // blk) * blk
    pad_end = jnp.cumsum(padded)
    pad_start = pad_end - padded
    dest = (pad_start[idx] + pos).reshape(-1)
    block_expert = jnp.minimum(
        jnp.sum((jnp.arange(n_blocks, dtype=I32) * blk)[:, None] >= pad_end[None, :], axis=1), N_EXPERTS - 1).astype(I32)
    n_used = (pad_end[-1:] // blk).astype(I32)
    src = jnp.zeros((n_blocks * blk,), I32).at[dest].set(jnp.arange(t * TOP_K, dtype=I32) // TOP_K,
                                                          unique_indices=True)

    act = _moe_up(block_expert, n_used, src, hp, w1b, b1[:, None, :])
    yb = _moe_down(block_expert, n_used, act, w2b, b2[:, None, :])
    return _combine(dest, gates, x1, final_g[None, :], yb, tm=256)


def kernel(x, norm_attn_g, w_in, moba_out_g, q_a_norm_g, kv_a_norm_g, w_uq, w_ukv, mla_out_g, w_o, norm_ffn_g,
           w_router, b_router, w1, b1, w2, b2, norm_final_g):
    batch, seq, d = x.shape
    depth = w_in.shape[0]
    assert depth == 1, "the final norm is fused into the last layer's combine step"
    x2 = x.reshape(batch * seq, d)
    out = _layer(x2, batch, seq, norm_attn_g[0], w_in[0], moba_out_g[0], q_a_norm_g[0], kv_a_norm_g[0], w_uq[0],
                 w_ukv[0], mla_out_g[0], w_o[0], norm_ffn_g[0], w_router[0], b_router[0], w1[0], b1[0], w2[0],
                 b2[0], norm_final_g)
    return out.reshape(batch, seq, d)
```

```python
import functools

import numpy as np
import jax
import jax.numpy as jnp
from jax import lax
from jax.experimental import pallas as pl
from jax.experimental.pallas import tpu as pltpu

MOBA_HEADS = 8
MOBA_HEAD_DIM = 128
MOBA_BLOCK = 256
MOBA_TOPK = 3
MLA_HEADS = 8
MLA_NOPE_DIM = 128
MLA_ROPE_DIM = 64
MLA_V_DIM = 128
MLA_Q_RANK = 512
MLA_KV_RANK = 512
ROPE_THETA = 10000.0
N_EXPERTS = 32
TOP_K = 4
SWIGLU_LIMIT = 7.0
SWIGLU_ALPHA = 1.702
EXPERT_BLOCK = 256
NORM_EPS = 1e-5

LANES = 128
SUBLANES = 8
QK_WIDTH = 256
ATT_TILE = 512
FLASH_STRIPS = 2
LOG2E = 1.4426950408889634
MASKED = -2.0 ** 100
M_INIT = -2.0 ** 98
FEAT_BLOCKS = 64
FEAT_ALIBI = FEAT_BLOCKS
V7X_VMEM_BUDGET = 56 * 1024 * 1024

F32 = jnp.float32
BF16 = jnp.bfloat16
U32 = jnp.uint32
I32 = jnp.int32


def _params(semantics, vmem_bytes):
    return pltpu.CompilerParams(dimension_semantics=semantics,
                                vmem_limit_bytes=min(int(vmem_bytes), V7X_VMEM_BUDGET))


def _rms(xf, g):
    ms = jnp.mean(xf * xf, axis=-1, keepdims=True)
    return xf * lax.rsqrt(ms + NORM_EPS) * g


def _dot(a, b):
    return jnp.dot(a, b, preferred_element_type=F32)


def _pack_halves(a, b):
    ai = lax.bitcast_convert_type(a.astype(BF16).astype(F32), U32)
    bi = lax.bitcast_convert_type(b.astype(BF16).astype(F32), U32)
    return ai | (bi >> 16)


def _unpack_halves(w):
    hi = lax.bitcast_convert_type(w & jnp.uint32(0xFFFF0000), F32)
    lo = lax.bitcast_convert_type(w << 16, F32)
    return hi, lo


def _store_row_tiles(ref, val):
    rows = val.shape[0]
    for s in range(SUBLANES):
        ref[pl.ds(s, rows, stride=SUBLANES), :] = val[:, s * LANES:(s + 1) * LANES]


def _load_row_tiles(ref):
    rows = ref.shape[0] // SUBLANES
    return jnp.concatenate([ref[pl.ds(s, rows, stride=SUBLANES), :] for s in range(SUBLANES)], axis=1)


def _row_tile(ref, r):
    return ref.at[pl.ds(pl.multiple_of(r * SUBLANES, SUBLANES), SUBLANES)]


def _inproj_kernel(x_ref, g_ref, w_ref, cos_ref, sin_ref, feat_ref,
                   qm_ref, kt_ref, vm_ref, kmean_ref, cq_ref, ckv_ref, kr_ref, *, q_scale):
    h = _rms(x_ref[...], g_ref[...]).astype(BF16)
    w = MOBA_HEADS * MOBA_HEAD_DIM
    dh = MOBA_HEAD_DIM
    tm = h.shape[0]

    def mm(lo, hi):
        return _dot(h, w_ref[:, lo:hi])

    qm_ref[...] = (mm(0, w) * q_scale).astype(BF16)
    k = mm(w, 2 * w)
    for b in range(tm // MOBA_BLOCK):
        kmean_ref[b] = jnp.mean(k[b * MOBA_BLOCK:(b + 1) * MOBA_BLOCK, :], axis=0, keepdims=True)
    v = mm(2 * w, 3 * w).astype(BF16)
    ones = jnp.ones((tm, QK_WIDTH - dh), BF16)
    for hh in range(MOBA_HEADS):
        kt_ref[hh, 0, :dh, :] = k[:, hh * dh:(hh + 1) * dh].T.astype(BF16)
        kt_ref[hh, 0, dh:, :] = feat_ref[0]
        vm_ref[:, hh * QK_WIDTH:hh * QK_WIDTH + dh] = v[:, hh * dh:(hh + 1) * dh]
        vm_ref[:, hh * QK_WIDTH + dh:(hh + 1) * QK_WIDTH] = ones
    c0 = 3 * w
    cq_ref[...] = mm(c0, c0 + MLA_Q_RANK)
    c1 = c0 + MLA_Q_RANK
    ckv_ref[...] = mm(c1, c1 + MLA_KV_RANK)
    c2 = c1 + MLA_KV_RANK
    kr = mm(c2, c2 + QK_WIDTH) * cos_ref[...] + mm(c2 + QK_WIDTH, c2 + 2 * QK_WIDTH) * sin_ref[...]
    kr_ref[...] = kr.astype(BF16)


def _in_proj(x2, g, w_all, cos_t, sin_t, feat, seq, tm):
    t, d = x2.shape
    w = MOBA_HEADS * MOBA_HEAD_DIM
    hw = MOBA_HEADS * QK_WIDTH
    nc = w_all.shape[1]
    n_pos = seq // tm
    nb = tm // MOBA_BLOCK
    row = lambda i: (i, 0)
    const = lambda i: (0, 0)
    pos = lambda i: (i % n_pos, 0)
    out_shape = [jax.ShapeDtypeStruct((t, w), BF16),
                 jax.ShapeDtypeStruct((MOBA_HEADS, t // tm, QK_WIDTH, tm), BF16),
                 jax.ShapeDtypeStruct((t, hw), BF16),
                 jax.ShapeDtypeStruct((t // MOBA_BLOCK, 1, w), F32),
                 jax.ShapeDtypeStruct((t, MLA_Q_RANK), F32), jax.ShapeDtypeStruct((t, MLA_KV_RANK), F32),
                 jax.ShapeDtypeStruct((t, QK_WIDTH), BF16)]
    out_specs = [pl.BlockSpec((tm, w), row),
                 pl.BlockSpec((MOBA_HEADS, 1, QK_WIDTH, tm), lambda i: (0, i, 0, 0)),
                 pl.BlockSpec((tm, hw), row),
                 pl.BlockSpec((nb, 1, w), lambda i: (i, 0, 0)),
                 pl.BlockSpec((tm, MLA_Q_RANK), row), pl.BlockSpec((tm, MLA_KV_RANK), row),
                 pl.BlockSpec((tm, QK_WIDTH), row)]
    vmem = (2 * tm * d * 4 + d * nc * 2 + 2 * tm * (w * 2 + 2 * hw * 2 + 2 * 512 * 4 + 256 * 2)
            + 4 * tm * QK_WIDTH * 4 + tm * d * 2 + 3 * tm * w * 4 + (4 << 20))
    return pl.pallas_call(
        functools.partial(_inproj_kernel, q_scale=MOBA_HEAD_DIM ** -0.5 * LOG2E),
        grid=(t // tm,),
        in_specs=[pl.BlockSpec((tm, d), row), pl.BlockSpec((1, d), const),
                  pl.BlockSpec((d, nc), const, pipeline_mode=pl.Buffered(1)),
                  pl.BlockSpec((tm, QK_WIDTH), pos), pl.BlockSpec((tm, QK_WIDTH), pos),
                  pl.BlockSpec((1, QK_WIDTH - MOBA_HEAD_DIM, tm), lambda i: (i % n_pos, 0, 0))],
        out_specs=out_specs, out_shape=out_shape,
        compiler_params=_params(("parallel",), vmem), name="in_proj",
    )(x2, g, w_all, cos_t, sin_t, feat)


def _mlaup_kernel(cq_ref, ckv_ref, gq_ref, gkv_ref, wqa_ref, wqb_ref, wk_ref, wv_ref,
                  kr_ref, cos_ref, sin_ref, q_ref, kt_ref, v_ref, *, q_scale):
    nq = _rms(cq_ref[...], gq_ref[...]).astype(BF16)
    nkv = _rms(ckv_ref[...], gkv_ref[...]).astype(BF16)
    cos = cos_ref[...]
    sin = sin_ref[...]
    kr = kr_ref[...].astype(F32)
    ones = jnp.ones((nq.shape[0], QK_WIDTH - MLA_V_DIM), BF16)
    for hh in range(MLA_HEADS):
        cols = slice(hh * QK_WIDTH, (hh + 1) * QK_WIDTH)
        q = _dot(nq, wqa_ref[:, cols]) * cos + _dot(nq, wqb_ref[:, cols]) * sin
        q_ref[:, cols] = (q * q_scale).astype(BF16)
        kt_ref[hh, 0] = (_dot(nkv, wk_ref[:, cols]) + kr).T.astype(BF16)
        v_ref[:, hh * QK_WIDTH:hh * QK_WIDTH + MLA_V_DIM] = _dot(
            nkv, wv_ref[:, hh * MLA_V_DIM:(hh + 1) * MLA_V_DIM]).astype(BF16)
        v_ref[:, hh * QK_WIDTH + MLA_V_DIM:(hh + 1) * QK_WIDTH] = ones


def _mla_up(cq, ckv, gq, gkv, wqa, wqb, wk, wv, kr, cos_t, sin_t, seq, tm):
    t = cq.shape[0]
    n_pos = seq // tm
    row = lambda i: (i, 0)
    const = lambda i: (0, 0)
    pos = lambda i: (i % n_pos, 0)
    hq = MLA_HEADS * QK_WIDTH
    hv = MLA_HEADS * MLA_V_DIM
    scale = (MLA_NOPE_DIM + MLA_ROPE_DIM) ** -0.5 * LOG2E
    vmem = (4 * tm * 512 * 4 + 2 * (3 * 512 * hq * 2 + 512 * hv * 2) + 8 * tm * QK_WIDTH * 4
            + 2 * tm * 3 * hq * 2 + (8 << 20))
    return pl.pallas_call(
        functools.partial(_mlaup_kernel, q_scale=scale),
        grid=(t // tm,),
        in_specs=[pl.BlockSpec((tm, MLA_Q_RANK), row), pl.BlockSpec((tm, MLA_KV_RANK), row),
                  pl.BlockSpec((1, MLA_Q_RANK), const), pl.BlockSpec((1, MLA_KV_RANK), const),
                  pl.BlockSpec((MLA_Q_RANK, hq), const), pl.BlockSpec((MLA_Q_RANK, hq), const),
                  pl.BlockSpec((MLA_KV_RANK, hq), const), pl.BlockSpec((MLA_KV_RANK, hv), const),
                  pl.BlockSpec((tm, QK_WIDTH), row),
                  pl.BlockSpec((tm, QK_WIDTH), pos), pl.BlockSpec((tm, QK_WIDTH), pos)],
        out_specs=[pl.BlockSpec((tm, hq), row),
                   pl.BlockSpec((MLA_HEADS, 1, QK_WIDTH, tm), lambda i: (0, i, 0, 0)),
                   pl.BlockSpec((tm, hq), row)],
        out_shape=[jax.ShapeDtypeStruct((t, hq), BF16),
                   jax.ShapeDtypeStruct((MLA_HEADS, t // tm, QK_WIDTH, tm), BF16),
                   jax.ShapeDtypeStruct((t, hq), BF16)],
        compiler_params=_params(("parallel",), vmem), name="mla_up",
    )(cq, ckv, gq, gkv, wqa, wqb, wk, wv, kr, cos_t, sin_t)


def _flash_stages(bufs, q_ref, *, pv=None, softmax=None, scores=None):
    s_sc, p_sc, alpha_sc, m_sc, acc_sc = bufs
    rows = s_sc.shape[0] // FLASH_STRIPS
    for r in range(FLASH_STRIPS):
        sl = slice(r * rows, (r + 1) * rows)
        if pv is not None:
            acc_sc[sl] = alpha_sc[sl] * acc_sc[sl] + _dot(p_sc[sl], pv)
        if softmax is not None:
            s = s_sc[sl]
            if softmax is not True:
                s = jnp.where(softmax[sl], s, MASKED)
            m_old = m_sc[sl]
            m_new = jnp.maximum(m_old, jnp.max(s, axis=-1, keepdims=True))
            p_sc[sl] = jnp.exp2(s - m_new).astype(BF16)
            alpha_sc[sl] = jnp.exp2(m_old - m_new)
            m_sc[sl] = m_new
        if scores is not None:
            s_sc[sl] = _dot(q_ref[sl], scores)


def _flash_loop(i, q_ref, kt_ref, v_ref, o_ref, bufs, diag_mask, *, tile, dv):
    s_sc, p_sc, alpha_sc, m_sc, acc_sc = bufs
    m_sc[...] = jnp.full_like(m_sc, M_INIT)
    acc_sc[...] = jnp.zeros_like(acc_sc)

    def v_tile(j):
        return v_ref[pl.ds(pl.multiple_of(j * tile, tile), tile), :]

    _flash_stages(bufs, q_ref, scores=kt_ref[0, 0])

    @pl.when(i >= 1)
    def _():
        _flash_stages(bufs, q_ref, softmax=True, scores=kt_ref[0, 1])

    def steady(k, carry):
        _flash_stages(bufs, q_ref, pv=v_tile(k - 2), softmax=True, scores=kt_ref[0, k])
        return carry

    lax.fori_loop(2, i + 1, steady, 0)

    @pl.when(i >= 1)
    def _():
        _flash_stages(bufs, q_ref, pv=v_tile(i - 1), softmax=diag_mask)

    @pl.when(i == 0)
    def _():
        _flash_stages(bufs, q_ref, softmax=diag_mask)

    _flash_stages(bufs, q_ref, pv=v_tile(i))
    acc = acc_sc[...]
    o_ref[...] = acc[:, :dv] / acc[:, dv:2 * dv]


def _mla_attn_kernel(q_ref, kt_ref, v_ref, w_ref, o_ref, wb_ref, *bufs, tile):
    wb_ref[...] = w_ref[...].astype(BF16)
    row = lax.broadcasted_iota(I32, (tile, tile), 0)
    col = lax.broadcasted_iota(I32, (tile, tile), 1)
    _flash_loop(pl.program_id(2), q_ref, kt_ref, v_ref, o_ref, bufs, row >= col, tile=tile, dv=MLA_V_DIM)


def _attn_call(kernel, name, q_spec, extra_specs, scratch, args, w_f32, batch, heads, seq, tile, dv):
    t = batch * seq
    nq = seq // tile
    steps = batch * heads * nq
    w2d = w_f32.reshape(-1, w_f32.shape[-1])
    assert w2d.shape[0] % steps == 0
    w_rows = w2d.shape[0] // steps
    w_spec = pl.BlockSpec((w_rows, w2d.shape[1]), lambda b, h, i: ((b * heads + h) * nq + i, 0))
    vmem = (2 * 2 * seq * QK_WIDTH * 2 + 10 * tile * tile * 4 + 10 * tile * QK_WIDTH * 4
            + 3 * w_rows * w2d.shape[1] * 6 + (4 << 20))
    out, wb = pl.pallas_call(
        kernel,
        grid=(batch, heads, nq),
        in_specs=[q_spec,
                  pl.BlockSpec((1, nq, QK_WIDTH, tile), lambda b, h, i: (h, b, 0, 0)),
                  pl.BlockSpec((seq, QK_WIDTH), lambda b, h, i: (b, h))] + extra_specs + [w_spec],
        out_specs=[pl.BlockSpec((tile, dv), lambda b, h, i: (b * nq + i, h)), w_spec],
        out_shape=[jax.ShapeDtypeStruct((t, heads * dv), F32), jax.ShapeDtypeStruct(w2d.shape, BF16)],
        scratch_shapes=scratch + [pltpu.VMEM((tile, tile), F32), pltpu.VMEM((tile, tile), BF16),
                                  pltpu.VMEM((tile, 1), F32), pltpu.VMEM((tile, 1), F32),
                                  pltpu.VMEM((tile, QK_WIDTH), F32)],
        compiler_params=_params(("parallel", "parallel", "arbitrary"), vmem), name=name,
    )(*args, w2d)
    return out, wb.reshape(w_f32.shape)


def _mla_attn(qc, kt, v, w_f32, batch, seq, tile):
    nq = seq // tile
    return _attn_call(functools.partial(_mla_attn_kernel, tile=tile), "mla_attn",
                      pl.BlockSpec((tile, QK_WIDTH), lambda b, h, i: (b * nq + i, h)), [], [],
                      (qc, kt, v), w_f32, batch, MLA_HEADS, seq, tile, MLA_V_DIM)


def _moba_attn_kernel(q_ref, kt_ref, v_ref, kmt_ref, aq_ref, w_ref, o_ref, wb_ref, qa_sc, *bufs, tile):
    wb_ref[...] = w_ref[...].astype(BF16)
    u = pl.program_id(2)
    blk = MOBA_BLOCK
    q = q_ref[...]
    km = kmt_ref[0]
    km_hi = km.astype(BF16)
    km_lo = (km - km_hi.astype(F32)).astype(BF16)
    gate = _dot(q, km_hi) + _dot(q, km_lo)
    lane = lax.broadcasted_iota(I32, (tile, LANES), 1)
    own = u * (tile // blk) + lax.broadcasted_iota(I32, (tile, LANES), 0) // blk
    g = jnp.where(lane < own, gate, -jnp.inf)
    keep = lane == own
    for _ in range(MOBA_TOPK):
        mx = jnp.max(g, axis=-1, keepdims=True)
        cand = jnp.where((g == mx) & (mx > -jnp.inf), lane, LANES)
        pick = lane == jnp.min(cand, axis=-1, keepdims=True)
        keep = keep | pick
        g = jnp.where(pick, -jnp.inf, g)
    feat = jnp.where(lane < FEAT_BLOCKS, jnp.where(keep, 0.0, 1.0), aq_ref[0])
    qa_sc[:, :MOBA_HEAD_DIM] = q
    qa_sc[:, MOBA_HEAD_DIM:] = feat.astype(BF16)

    row = lax.broadcasted_iota(I32, (tile, tile), 0)
    col = lax.broadcasted_iota(I32, (tile, tile), 1)
    causal_in_own_block = (row // blk != col // blk) | (col <= row)
    _flash_loop(u, qa_sc, kt_ref, v_ref, o_ref, bufs, causal_in_own_block, tile=tile, dv=MOBA_HEAD_DIM)


def _moba_attn(qm, kt, vm, kmt, aq, w_f32, batch, seq, tile):
    nq = seq // tile
    dh = MOBA_HEAD_DIM
    return _attn_call(functools.partial(_moba_attn_kernel, tile=tile), "moba_attn",
                      pl.BlockSpec((tile, dh), lambda b, h, i: (b * nq + i, h)),
                      [pl.BlockSpec((1, dh, LANES), lambda b, h, i: (b * MOBA_HEADS + h, 0, 0)),
                       pl.BlockSpec((1, 1, LANES), lambda b, h, i: (h, 0, 0))],
                      [pltpu.VMEM((tile, QK_WIDTH), BF16)],
                      (qm, kt, vm, kmt, aq), w_f32, batch, MOBA_HEADS, seq, tile, dh)


def _outproj_kernel(om_ref, ol_ref, gm_ref, gl_ref, wo_ref, x_ref, gf_ref, wrh_ref, wrl_ref, br_ref,
                    x1_ref, hp_ref, lg_ref):
    wm = om_ref.shape[1]
    a = _rms(om_ref[...], gm_ref[...]).astype(BF16)
    b = _rms(ol_ref[...], gl_ref[...]).astype(BF16)
    x1 = x_ref[...] + _dot(a, wo_ref[:wm, :]) + _dot(b, wo_ref[wm:, :])
    x1_ref[...] = x1
    h2 = _rms(x1, gf_ref[...])
    half = h2.shape[1] // 2
    _store_row_tiles(hp_ref, _pack_halves(h2[:, :half], h2[:, half:]))
    h_hi = h2.astype(BF16)
    h_lo = (h2 - h_hi.astype(F32)).astype(BF16)
    lg_ref[...] = (_dot(h_hi, wrh_ref[...]) + _dot(h_lo, wrh_ref[...]) + _dot(h_hi, wrl_ref[...])
                   + br_ref[...])


def _out_proj(om, ol, gm, gl, wo, x2, gf, wr_hi, wr_lo, br, tm):
    t, d = x2.shape
    wm = om.shape[1]
    wl = ol.shape[1]
    row = lambda i: (i, 0)
    const = lambda i: (0, 0)
    vmem = 2 * (wm + wl) * d * 2 + 2 * tm * (wm + wl + 2 * d) * 4 + 2 * tm * d * 2 + 6 * tm * d * 4 + (4 << 20)
    return pl.pallas_call(
        _outproj_kernel,
        grid=(t // tm,),
        in_specs=[pl.BlockSpec((tm, wm), row), pl.BlockSpec((tm, wl), row),
                  pl.BlockSpec((1, wm), const), pl.BlockSpec((1, wl), const),
                  pl.BlockSpec((wm + wl, d), const), pl.BlockSpec((tm, d), row), pl.BlockSpec((1, d), const),
                  pl.BlockSpec((d, LANES), const), pl.BlockSpec((d, LANES), const), pl.BlockSpec((1, LANES), const)],
        out_specs=[pl.BlockSpec((tm, d), row), pl.BlockSpec((tm * SUBLANES, LANES), row),
                   pl.BlockSpec((tm, LANES), row)],
        out_shape=[jax.ShapeDtypeStruct((t, d), F32), jax.ShapeDtypeStruct((t * SUBLANES, LANES), U32),
                   jax.ShapeDtypeStruct((t, LANES), F32)],
        compiler_params=_params(("parallel",), vmem), name="out_proj",
    )(om, ol, gm, gl, wo, x2, gf, wr_hi, wr_lo, br)


def _router_kernel(lg_ref, idx_ref, gate_ref, pos_ref, cnt_ref, *, tm):
    i = pl.program_id(0)

    @pl.when(i == 0)
    def _():
        cnt_ref[...] = jnp.zeros_like(cnt_ref)

    lane = lax.broadcasted_iota(I32, (tm, LANES), 1)
    work = jnp.where(lane < N_EXPERTS, lg_ref[...], -jnp.inf)
    vals, firsts, picks = [], [], []
    for _ in range(TOP_K):
        mx = jnp.max(work, axis=-1, keepdims=True)
        first = jnp.min(jnp.where(work == mx, lane, LANES), axis=-1, keepdims=True)
        pick = lane == first
        work = jnp.where(pick, -jnp.inf, work)
        vals.append(mx)
        firsts.append(first)
        picks.append(pick)
    es = [jnp.exp(v - vals[0]) for v in vals]
    denom = es[0] + es[1] + es[2] + es[3]
    chosen = jnp.zeros((tm, LANES), F32)
    for p in picks:
        chosen = jnp.where(p, 1.0, chosen)
    r = lax.broadcasted_iota(I32, (tm, tm), 0)
    cc = lax.broadcasted_iota(I32, (tm, tm), 1)
    before = (r > cc).astype(BF16)
    prior = _dot(before, chosen.astype(BF16)) + cnt_ref[...]
    lane4 = lax.broadcasted_iota(I32, (tm, TOP_K), 1)
    idx = jnp.zeros((tm, TOP_K), I32)
    gate = jnp.zeros((tm, TOP_K), F32)
    pos = jnp.zeros((tm, TOP_K), I32)
    for j in range(TOP_K):
        pj = jnp.sum(jnp.where(picks[j], prior, 0.0), axis=-1, keepdims=True)
        idx = jnp.where(lane4 == j, firsts[j], idx)
        gate = jnp.where(lane4 == j, es[j] / denom, gate)
        pos = jnp.where(lane4 == j, pj.astype(I32), pos)
    idx_ref[...] = idx
    gate_ref[...] = gate
    pos_ref[...] = pos
    cnt_ref[...] += jnp.sum(chosen, axis=0, keepdims=True)


def _router(logits, tm):
    t = logits.shape[0]
    row = lambda i: (i, 0)
    vmem = 4 * tm * tm * 4 + 24 * tm * LANES * 4 + (4 << 20)
    return pl.pallas_call(
        functools.partial(_router_kernel, tm=tm),
        grid=(t // tm,),
        in_specs=[pl.BlockSpec((tm, LANES), row)],
        out_specs=[pl.BlockSpec((tm, TOP_K), row)] * 3 + [pl.BlockSpec((1, LANES), lambda i: (0, 0))],
        out_shape=[jax.ShapeDtypeStruct((t, TOP_K), I32), jax.ShapeDtypeStruct((t, TOP_K), F32),
                   jax.ShapeDtypeStruct((t, TOP_K), I32), jax.ShapeDtypeStruct((1, LANES), F32)],
        compiler_params=_params(("arbitrary",), vmem), name="router",
    )(logits)


def _moe_up_kernel(be_ref, nu_ref, src_ref, hp_ref, w1_ref, b1_ref, act_ref, xs_sc, sem, *, fc):
    i = pl.program_id(0)
    blk = act_ref.shape[0]
    slot = i % 2

    def row_copy(block, r, s):
        return pltpu.make_async_copy(_row_tile(hp_ref, src_ref[block * blk + r]),
                                     _row_tile(xs_sc.at[s], r), sem.at[s])

    def start_block(block, s):
        def body(r, carry):
            row_copy(block, r, s).start()
            return carry
        lax.fori_loop(0, blk, body, 0, unroll=8)

    def wait_block(block, s):
        def body(r, carry):
            row_copy(block, r, s).wait()
            return carry
        lax.fori_loop(0, blk, body, 0, unroll=8)

    @pl.when(i == 0)
    def _():
        start_block(0, 0)

    @pl.when(i + 1 < nu_ref[0])
    def _():
        start_block(i + 1, 1 - slot)

    @pl.when(i < nu_ref[0])
    def _():
        wait_block(i, slot)
        hi, lo = _unpack_halves(_load_row_tiles(xs_sc.at[slot]))
        a = hi.astype(BF16)
        b = lo.astype(BF16)
        half = a.shape[1]
        d_ff = act_ref.shape[1]

        def pre(c0):
            return (_dot(a, w1_ref[0, :half, c0:c0 + fc]) + _dot(b, w1_ref[0, half:, c0:c0 + fc])
                    + b1_ref[0, :, c0:c0 + fc])

        for c in range(d_ff // fc):
            glu = jnp.minimum(pre(c * fc), SWIGLU_LIMIT)
            lin = jnp.clip(pre(d_ff + c * fc), -SWIGLU_LIMIT, SWIGLU_LIMIT)
            act = glu * jax.nn.sigmoid(SWIGLU_ALPHA * glu) * (lin + 1.0)
            act_ref[:, c * fc:(c + 1) * fc] = act.astype(BF16)

    @pl.when(i >= nu_ref[0])
    def _():
        act_ref[...] = jnp.zeros_like(act_ref)


def _moe_up(block_expert, n_used, src, hp, w1, b1):
    n_rows = src.shape[0]
    half = SUBLANES * hp.shape[1]
    e, d, f2 = w1.shape
    assert d == 2 * half
    d_ff = f2 // 2
    blk = EXPERT_BLOCK
    n_blocks = n_rows // blk
    fc = 512
    live = lambda i, be, nu: jnp.minimum(i, nu[0] - 1)
    grid_spec = pltpu.PrefetchScalarGridSpec(
        num_scalar_prefetch=3, grid=(n_blocks,),
        in_specs=[pl.BlockSpec(memory_space=pl.ANY),
                  pl.BlockSpec((1, d, f2), lambda i, be, nu, src: (be[live(i, be, nu)], 0, 0)),
                  pl.BlockSpec((1, 1, f2), lambda i, be, nu, src: (be[live(i, be, nu)], 0, 0))],
        out_specs=pl.BlockSpec((blk, d_ff), lambda i, be, nu, src: (i, 0)),
        scratch_shapes=[pltpu.VMEM((2, blk * SUBLANES, LANES), U32), pltpu.SemaphoreType.DMA((2,))])
    vmem = 2 * d * f2 * 2 + 2 * blk * half * 4 + 2 * blk * d_ff * 2 + 8 * blk * fc * 4 + 2 * blk * d * 2 + (4 << 20)
    return pl.pallas_call(
        functools.partial(_moe_up_kernel, fc=fc),
        grid_spec=grid_spec,
        out_shape=jax.ShapeDtypeStruct((n_rows, d_ff), BF16),
        compiler_params=_params(("arbitrary",), vmem), name="moe_up",
    )(block_expert, n_used, src, hp, w1, b1)


def _moe_down_kernel(be_ref, nu_ref, a_ref, w2_ref, b2_ref, y_ref):
    i = pl.program_id(0)

    @pl.when(i < nu_ref[0])
    def _():
        y = _dot(a_ref[...], w2_ref[0]) + b2_ref[0]
        half = y.shape[1] // 2
        _store_row_tiles(y_ref, _pack_halves(y[:, :half], y[:, half:]))

    @pl.when(i >= nu_ref[0])
    def _():
        y_ref[...] = jnp.zeros_like(y_ref)


def _moe_down(block_expert, n_used, act, w2, b2):
    n_rows, d_ff = act.shape
    e, _, d = w2.shape
    blk = EXPERT_BLOCK
    n_blocks = n_rows // blk
    live = lambda i, be, nu: jnp.minimum(i, nu[0] - 1)
    grid_spec = pltpu.PrefetchScalarGridSpec(
        num_scalar_prefetch=2, grid=(n_blocks,),
        in_specs=[pl.BlockSpec((blk, d_ff), lambda i, be, nu: (live(i, be, nu), 0)),
                  pl.BlockSpec((1, d_ff, d), lambda i, be, nu: (be[live(i, be, nu)], 0, 0)),
                  pl.BlockSpec((1, 1, d), lambda i, be, nu: (be[live(i, be, nu)], 0, 0))],
        out_specs=pl.BlockSpec((blk * SUBLANES, LANES), lambda i, be, nu: (i, 0)))
    vmem = 2 * d_ff * d * 2 + 2 * blk * d_ff * 2 + 2 * blk * d * 2 + 4 * blk * d * 4 + (4 << 20)
    return pl.pallas_call(
        _moe_down_kernel,
        grid_spec=grid_spec,
        out_shape=jax.ShapeDtypeStruct((n_rows * SUBLANES, LANES), U32),
        compiler_params=_params(("arbitrary",), vmem), name="moe_down",
    )(block_expert, n_used, act, w2, b2)


def _combine_kernel(dest_ref, gate_ref, x1_ref, gfin_ref, yb_ref, o_ref, rows_sc, sem, *, tm):
    i = pl.program_id(0)
    slot = i % 2

    def row_copy(tile, r, j, s):
        return pltpu.make_async_copy(_row_tile(yb_ref, dest_ref[(tile * tm + r) * TOP_K + j]),
                                     _row_tile(rows_sc.at[s, j], r), sem.at[s])

    def start_tile(tile, s):
        def body(r, carry):
            for j in range(TOP_K):
                row_copy(tile, r, j, s).start()
            return carry
        lax.fori_loop(0, tm, body, 0, unroll=4)

    def wait_tile(tile, s):
        def body(r, carry):
            for j in range(TOP_K):
                row_copy(tile, r, j, s).wait()
            return carry
        lax.fori_loop(0, tm, body, 0, unroll=4)

    @pl.when(i == 0)
    def _():
        start_tile(0, 0)

    @pl.when(i + 1 < pl.num_programs(0))
    def _():
        start_tile(i + 1, 1 - slot)

    wait_tile(i, slot)

    x1 = x1_ref[...]
    half = x1.shape[1] // 2
    gates = gate_ref[...]
    left = x1[:, :half]
    right = x1[:, half:]
    for j in range(TOP_K):
        hi, lo = _unpack_halves(_load_row_tiles(rows_sc.at[slot, j]))
        gj = gates[:, j:j + 1]
        left = left + gj * hi
        right = right + gj * lo
    ms = (jnp.sum(left * left, axis=-1, keepdims=True) + jnp.sum(right * right, axis=-1, keepdims=True)) / (2 * half)
    inv = lax.rsqrt(ms + NORM_EPS)
    g = gfin_ref[...]
    o_ref[:, :half] = left * inv * g[:, :half]
    o_ref[:, half:] = right * inv * g[:, half:]


def _combine(dest, gates, x1, gfin, yb, tm):
    t, d = x1.shape
    half = d // 2
    grid_spec = pltpu.PrefetchScalarGridSpec(
        num_scalar_prefetch=1, grid=(t // tm,),
        in_specs=[pl.BlockSpec((tm, TOP_K), lambda i, dst: (i, 0)),
                  pl.BlockSpec((tm, d), lambda i, dst: (i, 0)),
                  pl.BlockSpec((1, d), lambda i, dst: (0, 0)),
                  pl.BlockSpec(memory_space=pl.ANY)],
        out_specs=pl.BlockSpec((tm, d), lambda i, dst: (i, 0)),
        scratch_shapes=[pltpu.VMEM((2, TOP_K, tm * SUBLANES, LANES), U32), pltpu.SemaphoreType.DMA((2,))])
    vmem = 2 * TOP_K * tm * half * 4 + 4 * tm * d * 4 + 6 * tm * d * 4 + 2 * tm * LANES * 4 + (4 << 20)
    return pl.pallas_call(
        functools.partial(_combine_kernel, tm=tm),
        grid_spec=grid_spec,
        out_shape=jax.ShapeDtypeStruct((t, d), F32),
        compiler_params=_params(("arbitrary",), vmem), name="combine",
    )(dest, gates, x1, gfin, yb)


def _rope_tables(seq):
    inv = 1.0 / (ROPE_THETA ** (jnp.arange(0, MLA_ROPE_DIM, 2, dtype=F32) / MLA_ROPE_DIM))
    ang = jnp.arange(seq, dtype=F32)[:, None] * inv[None, :]
    ang = jnp.concatenate([ang, ang], axis=-1)
    ones = jnp.ones((seq, MLA_NOPE_DIM), F32)
    zeros = jnp.zeros((seq, QK_WIDTH - MLA_NOPE_DIM - MLA_ROPE_DIM), F32)
    cos_t = jnp.concatenate([ones, jnp.cos(ang), zeros], axis=-1)
    sin_t = jnp.concatenate([0.0 * ones, jnp.sin(ang), zeros], axis=-1)
    return cos_t, sin_t


def _bf16_pieces(x, n=3):
    out = []
    for _ in range(n):
        p = float(np.asarray(x, dtype=np.float32).astype(jnp.bfloat16).astype(np.float32))
        out.append(p)
        x = x - p
    return out


def _moba_features(seq, tile):
    pos = np.arange(seq)
    blk, off = pos // MOBA_BLOCK, pos % MOBA_BLOCK
    assert seq // MOBA_BLOCK <= FEAT_BLOCKS
    feat = np.zeros((QK_WIDTH - MOBA_HEAD_DIM, seq), np.float32)
    feat[:FEAT_BLOCKS] = np.where(blk[None, :] == np.arange(FEAT_BLOCKS)[:, None], MASKED, 0.0)
    feat[FEAT_ALIBI:FEAT_ALIBI + 3] = blk[None, :]
    feat[FEAT_ALIBI + 3:FEAT_ALIBI + 6] = off[None, :]
    feat = feat.reshape(feat.shape[0], seq // tile, tile).transpose(1, 0, 2)
    aq = np.zeros((MOBA_HEADS, 1, LANES), np.float32)
    for h in range(MOBA_HEADS):
        slope = 2.0 ** (-8.0 * (h + 1) / MOBA_HEADS)
        aq[h, 0, FEAT_ALIBI:FEAT_ALIBI + 3] = _bf16_pieces(slope * LOG2E * MOBA_BLOCK)
        aq[h, 0, FEAT_ALIBI + 3:FEAT_ALIBI + 6] = _bf16_pieces(slope * LOG2E)
    return jnp.asarray(feat, BF16), jnp.asarray(aq, F32)


def _rot_cols(w):
    half = w.shape[-1] // 2
    return jnp.concatenate([-w[..., half:], w[..., :half]], axis=-1)


def _rope_slot(w):
    k = w.shape[0]
    return jnp.concatenate([jnp.zeros((k, MLA_NOPE_DIM), w.dtype), w,
                            jnp.zeros((k, QK_WIDTH - MLA_NOPE_DIM - MLA_ROPE_DIM), w.dtype)], axis=-1)


def _layer(x2, batch, seq, norm_attn_g, w_in, moba_out_g, q_a_norm_g, kv_a_norm_g, w_uq, w_ukv, mla_out_g,
           w_o, norm_ffn_g, w_router, b_router, w1, b1, w2, b2, final_g):
    t, d = x2.shape
    wm = MOBA_HEADS * MOBA_HEAD_DIM
    tile = ATT_TILE
    assert seq % tile == 0 and tile % MOBA_BLOCK == 0
    cos_t, sin_t = _rope_tables(seq)
    feat, aq = _moba_features(seq, tile)

    c_kr = 3 * wm + MLA_Q_RANK + MLA_KV_RANK
    w_kr = w_in[:, c_kr:c_kr + MLA_ROPE_DIM]
    w_all = jnp.concatenate([w_in[:, :c_kr], _rope_slot(w_kr), _rope_slot(_rot_cols(w_kr))], axis=1).astype(BF16)
    wq = w_uq.reshape(MLA_Q_RANK, MLA_HEADS, MLA_NOPE_DIM + MLA_ROPE_DIM)
    zq = jnp.zeros((MLA_Q_RANK, MLA_HEADS, QK_WIDTH - MLA_NOPE_DIM - MLA_ROPE_DIM), F32)
    wqa = jnp.concatenate([wq, zq], axis=-1).reshape(MLA_Q_RANK, -1).astype(BF16)
    wqb = jnp.concatenate([jnp.zeros((MLA_Q_RANK, MLA_HEADS, MLA_NOPE_DIM), F32),
                           _rot_cols(wq[..., MLA_NOPE_DIM:]), zq], axis=-1).reshape(MLA_Q_RANK, -1).astype(BF16)
    wkv = w_ukv.reshape(MLA_KV_RANK, MLA_HEADS, MLA_NOPE_DIM + MLA_V_DIM)
    wk = jnp.concatenate([wkv[..., :MLA_NOPE_DIM],
                          jnp.zeros((MLA_KV_RANK, MLA_HEADS, QK_WIDTH - MLA_NOPE_DIM), F32)],
                         axis=-1).reshape(MLA_KV_RANK, -1).astype(BF16)
    wv = wkv[..., MLA_NOPE_DIM:].reshape(MLA_KV_RANK, -1).astype(BF16)
    wr = jnp.pad(w_router, ((0, 0), (0, LANES - N_EXPERTS)))
    wr_hi = wr.astype(BF16)
    wr_lo = (wr - wr_hi.astype(F32)).astype(BF16)
    br = jnp.pad(b_router, (0, LANES - N_EXPERTS))[None, :]

    qm, ktm, vm, kmean, cq, ckv, kr = _in_proj(x2, norm_attn_g[None, :], w_all, cos_t, sin_t, feat, seq, tm=tile)
    qc, ktl, vl = _mla_up(cq, ckv, q_a_norm_g[None, :], kv_a_norm_g[None, :], wqa, wqb, wk, wv, kr,
                          cos_t, sin_t, seq, tm=tile)
    n_blk = seq // MOBA_BLOCK
    kmt = kmean.reshape(batch, n_blk, MOBA_HEADS, MOBA_HEAD_DIM).transpose(0, 2, 3, 1)
    kmt = jnp.pad(kmt, ((0, 0), (0, 0), (0, 0), (0, LANES - n_blk))).reshape(batch * MOBA_HEADS, MOBA_HEAD_DIM, LANES)
    o_moba, w2b = _moba_attn(qm, ktm, vm, kmt, aq, w2, batch, seq, tile)
    o_mla, w1b = _mla_attn(qc, ktl, vl, w1, batch, seq, tile)
    x1, hp, logits = _out_proj(o_moba, o_mla, moba_out_g[None, :], mla_out_g[None, :], w_o.astype(BF16), x2,
                               norm_ffn_g[None, :], wr_hi, wr_lo, br, tm=256)

    idx, gates, pos, counts = _router(logits, tm=min(512, t))
    blk = EXPERT_BLOCK
    n_blocks = -(-(t * TOP_K) // blk) + N_EXPERTS
    counts = counts[0, :N_EXPERTS].astype(I32)
    padded = ((counts + blk - 1) // blk) * blk
    pad_end = jnp.cumsum(padded)
    pad_start = pad_end - padded
    dest = (pad_start[idx] + pos).reshape(-1)
    block_expert = jnp.minimum(
        jnp.sum((jnp.arange(n_blocks, dtype=I32) * blk)[:, None] >= pad_end[None, :], axis=1), N_EXPERTS - 1).astype(I32)
    n_used = (pad_end[-1:] // blk).astype(I32)
    src = jnp.zeros((n_blocks * blk,), I32).at[dest].set(jnp.arange(t * TOP_K, dtype=I32) // TOP_K,
                                                          unique_indices=True)

    act = _moe_up(block_expert, n_used, src, hp, w1b, b1[:, None, :])
    yb = _moe_down(block_expert, n_used, act, w2b, b2[:, None, :])
    return _combine(dest, gates, x1, final_g[None, :], yb, tm=256)


def kernel(x, norm_attn_g, w_in, moba_out_g, q_a_norm_g, kv_a_norm_g, w_uq, w_ukv, mla_out_g, w_o, norm_ffn_g,
           w_router, b_router, w1, b1, w2, b2, norm_final_g):
    batch, seq, d = x.shape
    depth = w_in.shape[0]
    assert depth == 1, "the final norm is fused into the last layer's combine step"
    x2 = x.reshape(batch * seq, d)
    out = _layer(x2, batch, seq, norm_attn_g[0], w_in[0], moba_out_g[0], q_a_norm_g[0], kv_a_norm_g[0], w_uq[0],
                 w_ukv[0], mla_out_g[0], w_o[0], norm_ffn_g[0], w_router[0], b_router[0], w1[0], b1[0], w2[0],
                 b2[0], norm_final_g)
    return out.reshape(batch, seq, d)
```

```python
import functools

import numpy as np
import jax
import jax.numpy as jnp
from jax import lax
from jax.experimental import pallas as pl
from jax.experimental.pallas import tpu as pltpu

MOBA_HEADS = 8
MOBA_HEAD_DIM = 128
MOBA_BLOCK = 256
MOBA_TOPK = 3
MLA_HEADS = 8
MLA_NOPE_DIM = 128
MLA_ROPE_DIM = 64
MLA_V_DIM = 128
MLA_Q_RANK = 512
MLA_KV_RANK = 512
ROPE_THETA = 10000.0
N_EXPERTS = 32
TOP_K = 4
SWIGLU_LIMIT = 7.0
SWIGLU_ALPHA = 1.702
EXPERT_BLOCK = 256
NORM_EPS = 1e-5

LANES = 128
SUBLANES = 8
QK_WIDTH = 256
ATT_TILE = 512
FLASH_STRIPS = 2
LOG2E = 1.4426950408889634
MASKED = -2.0 ** 100
M_INIT = -2.0 ** 98
FEAT_BLOCKS = 64
FEAT_ALIBI = FEAT_BLOCKS
V7X_VMEM_BUDGET = 56 * 1024 * 1024

F32 = jnp.float32
BF16 = jnp.bfloat16
U32 = jnp.uint32
I32 = jnp.int32


def _params(semantics, vmem_bytes):
    return pltpu.CompilerParams(dimension_semantics=semantics,
                                vmem_limit_bytes=min(int(vmem_bytes), V7X_VMEM_BUDGET))


def _rms(xf, g):
    ms = jnp.mean(xf * xf, axis=-1, keepdims=True)
    return xf * lax.rsqrt(ms + NORM_EPS) * g


def _dot(a, b):
    return jnp.dot(a, b, preferred_element_type=F32)


def _pack_halves(a, b):
    ai = lax.bitcast_convert_type(a.astype(BF16).astype(F32), U32)
    bi = lax.bitcast_convert_type(b.astype(BF16).astype(F32), U32)
    return ai | (bi >> 16)


def _unpack_halves(w):
    hi = lax.bitcast_convert_type(w & jnp.uint32(0xFFFF0000), F32)
    lo = lax.bitcast_convert_type(w << 16, F32)
    return hi, lo


def _store_row_tiles(ref, val):
    rows = val.shape[0]
    for s in range(SUBLANES):
        ref[pl.ds(s, rows, stride=SUBLANES), :] = val[:, s * LANES:(s + 1) * LANES]


def _load_row_tiles(ref):
    rows = ref.shape[0] // SUBLANES
    return jnp.concatenate([ref[pl.ds(s, rows, stride=SUBLANES), :] for s in range(SUBLANES)], axis=1)


def _row_tile(ref, r):
    return ref.at[pl.ds(pl.multiple_of(r * SUBLANES, SUBLANES), SUBLANES)]


def _inproj_kernel(x_ref, g_ref, w_ref, cos_ref, sin_ref, feat_ref,
                   qm_ref, kt_ref, vm_ref, kmean_ref, cq_ref, ckv_ref, kr_ref, *, q_scale):
    h = _rms(x_ref[...], g_ref[...]).astype(BF16)
    w = MOBA_HEADS * MOBA_HEAD_DIM
    dh = MOBA_HEAD_DIM
    tm = h.shape[0]

    def mm(lo, hi):
        return _dot(h, w_ref[:, lo:hi])

    qm_ref[...] = (mm(0, w) * q_scale).astype(BF16)
    k = mm(w, 2 * w)
    for b in range(tm // MOBA_BLOCK):
        kmean_ref[b] = jnp.mean(k[b * MOBA_BLOCK:(b + 1) * MOBA_BLOCK, :], axis=0, keepdims=True)
    v = mm(2 * w, 3 * w).astype(BF16)
    ones = jnp.ones((tm, QK_WIDTH - dh), BF16)
    for hh in range(MOBA_HEADS):
        kt_ref[hh, 0, :dh, :] = k[:, hh * dh:(hh + 1) * dh].T.astype(BF16)
        kt_ref[hh, 0, dh:, :] = feat_ref[0]
        vm_ref[:, hh * QK_WIDTH:hh * QK_WIDTH + dh] = v[:, hh * dh:(hh + 1) * dh]
        vm_ref[:, hh * QK_WIDTH + dh:(hh + 1) * QK_WIDTH] = ones
    c0 = 3 * w
    cq_ref[...] = mm(c0, c0 + MLA_Q_RANK)
    c1 = c0 + MLA_Q_RANK
    ckv_ref[...] = mm(c1, c1 + MLA_KV_RANK)
    c2 = c1 + MLA_KV_RANK
    kr = mm(c2, c2 + QK_WIDTH) * cos_ref[...] + mm(c2 + QK_WIDTH, c2 + 2 * QK_WIDTH) * sin_ref[...]
    kr_ref[...] = kr.astype(BF16)


def _in_proj(x2, g, w_all, cos_t, sin_t, feat, seq, tm):
    t, d = x2.shape
    w = MOBA_HEADS * MOBA_HEAD_DIM
    hw = MOBA_HEADS * QK_WIDTH
    nc = w_all.shape[1]
    n_pos = seq // tm
    nb = tm // MOBA_BLOCK
    row = lambda i: (i, 0)
    const = lambda i: (0, 0)
    pos = lambda i: (i % n_pos, 0)
    out_shape = [jax.ShapeDtypeStruct((t, w), BF16),
                 jax.ShapeDtypeStruct((MOBA_HEADS, t // tm, QK_WIDTH, tm), BF16),
                 jax.ShapeDtypeStruct((t, hw), BF16),
                 jax.ShapeDtypeStruct((t // MOBA_BLOCK, 1, w), F32),
                 jax.ShapeDtypeStruct((t, MLA_Q_RANK), F32), jax.ShapeDtypeStruct((t, MLA_KV_RANK), F32),
                 jax.ShapeDtypeStruct((t, QK_WIDTH), BF16)]
    out_specs = [pl.BlockSpec((tm, w), row),
                 pl.BlockSpec((MOBA_HEADS, 1, QK_WIDTH, tm), lambda i: (0, i, 0, 0)),
                 pl.BlockSpec((tm, hw), row),
                 pl.BlockSpec((nb, 1, w), lambda i: (i, 0, 0)),
                 pl.BlockSpec((tm, MLA_Q_RANK), row), pl.BlockSpec((tm, MLA_KV_RANK), row),
                 pl.BlockSpec((tm, QK_WIDTH), row)]
    vmem = (2 * tm * d * 4 + d * nc * 2 + 2 * tm * (w * 2 + 2 * hw * 2 + 2 * 512 * 4 + 256 * 2)
            + 4 * tm * QK_WIDTH * 4 + tm * d * 2 + 3 * tm * w * 4 + (4 << 20))
    return pl.pallas_call(
        functools.partial(_inproj_kernel, q_scale=MOBA_HEAD_DIM ** -0.5 * LOG2E),
        grid=(t // tm,),
        in_specs=[pl.BlockSpec((tm, d), row), pl.BlockSpec((1, d), const),
                  pl.BlockSpec((d, nc), const, pipeline_mode=pl.Buffered(1)),
                  pl.BlockSpec((tm, QK_WIDTH), pos), pl.BlockSpec((tm, QK_WIDTH), pos),
                  pl.BlockSpec((1, QK_WIDTH - MOBA_HEAD_DIM, tm), lambda i: (i % n_pos, 0, 0))],
        out_specs=out_specs, out_shape=out_shape,
        compiler_params=_params(("parallel",), vmem), name="in_proj",
    )(x2, g, w_all, cos_t, sin_t, feat)


def _mlaup_kernel(cq_ref, ckv_ref, gq_ref, gkv_ref, wqa_ref, wqb_ref, wk_ref, wv_ref,
                  kr_ref, cos_ref, sin_ref, q_ref, kt_ref, v_ref, *, q_scale):
    nq = _rms(cq_ref[...], gq_ref[...]).astype(BF16)
    nkv = _rms(ckv_ref[...], gkv_ref[...]).astype(BF16)
    cos = cos_ref[...]
    sin = sin_ref[...]
    kr = kr_ref[...].astype(F32)
    ones = jnp.ones((nq.shape[0], QK_WIDTH - MLA_V_DIM), BF16)
    for hh in range(MLA_HEADS):
        cols = slice(hh * QK_WIDTH, (hh + 1) * QK_WIDTH)
        q = _dot(nq, wqa_ref[:, cols]) * cos + _dot(nq, wqb_ref[:, cols]) * sin
        q_ref[:, cols] = (q * q_scale).astype(BF16)
        kt_ref[hh, 0] = (_dot(nkv, wk_ref[:, cols]) + kr).T.astype(BF16)
        v_ref[:, hh * QK_WIDTH:hh * QK_WIDTH + MLA_V_DIM] = _dot(
            nkv, wv_ref[:, hh * MLA_V_DIM:(hh + 1) * MLA_V_DIM]).astype(BF16)
        v_ref[:, hh * QK_WIDTH + MLA_V_DIM:(hh + 1) * QK_WIDTH] = ones


def _mla_up(cq, ckv, gq, gkv, wqa, wqb, wk, wv, kr, cos_t, sin_t, seq, tm):
    t = cq.shape[0]
    n_pos = seq // tm
    row = lambda i: (i, 0)
    const = lambda i: (0, 0)
    pos = lambda i: (i % n_pos, 0)
    hq = MLA_HEADS * QK_WIDTH
    hv = MLA_HEADS * MLA_V_DIM
    scale = (MLA_NOPE_DIM + MLA_ROPE_DIM) ** -0.5 * LOG2E
    vmem = (4 * tm * 512 * 4 + 2 * (3 * 512 * hq * 2 + 512 * hv * 2) + 8 * tm * QK_WIDTH * 4
            + 2 * tm * 3 * hq * 2 + (8 << 20))
    return pl.pallas_call(
        functools.partial(_mlaup_kernel, q_scale=scale),
        grid=(t // tm,),
        in_specs=[pl.BlockSpec((tm, MLA_Q_RANK), row), pl.BlockSpec((tm, MLA_KV_RANK), row),
                  pl.BlockSpec((1, MLA_Q_RANK), const), pl.BlockSpec((1, MLA_KV_RANK), const),
                  pl.BlockSpec((MLA_Q_RANK, hq), const), pl.BlockSpec((MLA_Q_RANK, hq), const),
                  pl.BlockSpec((MLA_KV_RANK, hq), const), pl.BlockSpec((MLA_KV_RANK, hv), const),
                  pl.BlockSpec((tm, QK_WIDTH), row),
                  pl.BlockSpec((tm, QK_WIDTH), pos), pl.BlockSpec((tm, QK_WIDTH), pos)],
        out_specs=[pl.BlockSpec((tm, hq), row),
                   pl.BlockSpec((MLA_HEADS, 1, QK_WIDTH, tm), lambda i: (0, i, 0, 0)),
                   pl.BlockSpec((tm, hq), row)],
        out_shape=[jax.ShapeDtypeStruct((t, hq), BF16),
                   jax.ShapeDtypeStruct((MLA_HEADS, t // tm, QK_WIDTH, tm), BF16),
                   jax.ShapeDtypeStruct((t, hq), BF16)],
        compiler_params=_params(("parallel",), vmem), name="mla_up",
    )(cq, ckv, gq, gkv, wqa, wqb, wk, wv, kr, cos_t, sin_t)


def _flash_stages(refs, item_tables, *, pv=None, softmax=None, scores=None, tile, dv):
    q_ref, kt_ref, v_ref, o_ref, viol_sc, l_all, m_all, s_sc, p_sc, alpha_sc = refs
    ti_ref, tj_ref = item_tables
    rows = tile // FLASH_STRIPS

    def q_rows(t, r):
        return pl.ds(pl.multiple_of(ti_ref[t] * tile + r * rows, rows), rows)

    for r in range(FLASH_STRIPS):
        sl = slice(r * rows, (r + 1) * rows)
        if pv is not None:
            qr = q_rows(pv, r)
            res = _dot(p_sc[sl], v_ref[pl.ds(pl.multiple_of(tj_ref[pv] * tile, tile), tile), :])
            a = alpha_sc[sl]
            o_ref[qr, :] = a * o_ref[qr, :] + res[:, :dv]
            l_all[qr, :] = a * l_all[qr, :] + res[:, dv:2 * dv]
        if softmax is not None:
            qr = q_rows(softmax, r)
            ahead = (ti_ref[softmax] - tj_ref[softmax]) * tile
            s = jnp.where(viol_sc[sl] - ahead > 0, MASKED, s_sc[sl])
            m_old = m_all[qr, :]
            m_new = jnp.maximum(m_old, jnp.max(s, axis=-1, keepdims=True))
            p_sc[sl] = jnp.exp2(s - m_new).astype(BF16)
            alpha_sc[sl] = jnp.exp2(m_old - m_new)
            m_all[qr, :] = m_new
        if scores is not None:
            s_sc[sl] = _dot(q_ref[q_rows(scores, r), :], kt_ref[0, tj_ref[scores]])


def _flash_sweep(refs, tables, n_items, *, tile, dv):
    ti_ref, tj_ref, st_ref = tables
    q_ref, kt_ref, v_ref, o_ref, viol_sc, l_all, m_all, s_sc, p_sc, alpha_sc = refs
    g = pl.program_id(2)
    last = pl.num_programs(2) - 1
    stages = functools.partial(_flash_stages, refs, (ti_ref, tj_ref), tile=tile, dv=dv)

    @pl.when(g == 0)
    def _():
        o_ref[...] = jnp.zeros_like(o_ref)
        l_all[...] = jnp.zeros_like(l_all)
        m_all[...] = jnp.full_like(m_all, M_INIT)
        stages(scores=0)
        stages(softmax=0, scores=1)

    def steady(t, carry):
        stages(pv=t - 2, softmax=t - 1, scores=t)
        return carry

    lax.fori_loop(jnp.maximum(st_ref[g], 2), st_ref[g + 1], steady, 0)

    @pl.when(g == last)
    def _():
        stages(pv=n_items - 2, softmax=n_items - 1)
        stages(pv=n_items - 1)

        def finish(u, carry):
            rows = pl.ds(pl.multiple_of(u * tile, tile), tile)
            o_ref[rows, :] = o_ref[rows, :] / l_all[rows, :]
            return carry

        lax.fori_loop(0, o_ref.shape[0] // tile, finish, 0)


def _attn_items(nq):
    ti = np.array([i for j in range(nq) for i in range(j, nq)], np.int32)
    tj = np.array([j for j in range(nq) for i in range(j, nq)], np.int32)
    st = np.array([(g * len(ti)) // nq for g in range(nq + 1)], np.int32)
    return jnp.asarray(ti), jnp.asarray(tj), jnp.asarray(st), len(ti)


def _mla_attn_kernel(ti_ref, tj_ref, st_ref, q_ref, kt_ref, v_ref, w_ref, o_ref, wb_ref,
                     viol_sc, *bufs, tile, n_items):
    wb_ref[...] = w_ref[...].astype(BF16)

    @pl.when(pl.program_id(2) == 0)
    def _():
        viol_sc[...] = (lax.broadcasted_iota(I32, (tile, tile), 1) - lax.broadcasted_iota(I32, (tile, tile), 0))

    _flash_sweep((q_ref, kt_ref, v_ref, o_ref, viol_sc) + bufs, (ti_ref, tj_ref, st_ref), n_items,
                 tile=tile, dv=MLA_V_DIM)


def _attn_call(kernel, name, q_width, extra_specs, scratch, args, w_f32, batch, heads, seq, tile, dv):
    t = batch * seq
    nq = seq // tile
    assert nq >= 2
    ti, tj, st, n_items = _attn_items(nq)
    steps = batch * heads * nq
    w2d = w_f32.reshape(-1, w_f32.shape[-1])
    assert w2d.shape[0] % steps == 0
    w_rows = w2d.shape[0] // steps
    w_spec = pl.BlockSpec((w_rows, w2d.shape[1]), lambda b, h, g, *_: ((b * heads + h) * nq + g, 0))
    once = pl.Buffered(1)
    grid_spec = pltpu.PrefetchScalarGridSpec(
        num_scalar_prefetch=3, grid=(batch, heads, nq),
        in_specs=[pl.BlockSpec((seq, q_width), lambda b, h, g, *_: (b, h), pipeline_mode=once),
                  pl.BlockSpec((1, nq, QK_WIDTH, tile), lambda b, h, g, *_: (h, b, 0, 0), pipeline_mode=once),
                  pl.BlockSpec((seq, QK_WIDTH), lambda b, h, g, *_: (b, h), pipeline_mode=once)]
                 + extra_specs + [w_spec],
        out_specs=[pl.BlockSpec((seq, dv), lambda b, h, g, *_: (b, h)), w_spec],
        scratch_shapes=scratch + [pltpu.VMEM((tile, tile), I32), pltpu.VMEM((seq, dv), F32),
                                  pltpu.VMEM((seq, 1), F32), pltpu.VMEM((tile, tile), F32),
                                  pltpu.VMEM((tile, tile), BF16), pltpu.VMEM((tile, 1), F32)])
    vmem = (seq * (q_width + 2 * QK_WIDTH) * 2 + 2 * seq * dv * 4 + seq * dv * 4 + seq * LANES * 4
            + 8 * tile * tile * 4 + 3 * w_rows * w2d.shape[1] * 6 + seq * QK_WIDTH * 2 + (4 << 20))
    out, wb = pl.pallas_call(
        functools.partial(kernel, n_items=n_items),
        grid_spec=grid_spec,
        out_shape=[jax.ShapeDtypeStruct((t, heads * dv), F32), jax.ShapeDtypeStruct(w2d.shape, BF16)],
        compiler_params=_params(("parallel", "parallel", "arbitrary"), vmem), name=name,
    )(ti, tj, st, *args, w2d)
    return out, wb.reshape(w_f32.shape)


def _mla_attn(qc, kt, v, w_f32, batch, seq, tile):
    return _attn_call(functools.partial(_mla_attn_kernel, tile=tile), "mla_attn", QK_WIDTH, [], [],
                      (qc, kt, v), w_f32, batch, MLA_HEADS, seq, tile, MLA_V_DIM)


def _moba_attn_kernel(ti_ref, tj_ref, st_ref, q_ref, kt_ref, v_ref, kmt_ref, aq_ref, w_ref, o_ref, wb_ref,
                      qa_sc, viol_sc, *bufs, tile, n_items):
    wb_ref[...] = w_ref[...].astype(BF16)
    blk = MOBA_BLOCK

    @pl.when(pl.program_id(2) == 0)
    def _():
        row = lax.broadcasted_iota(I32, (tile, tile), 0)
        col = lax.broadcasted_iota(I32, (tile, tile), 1)
        viol_sc[...] = jnp.where(row // blk == col // blk, col - row, -1)
        km = kmt_ref[0]
        km_hi = km.astype(BF16)
        km_lo = (km - km_hi.astype(F32)).astype(BF16)
        lane = lax.broadcasted_iota(I32, (tile, LANES), 1)
        in_tile = lax.broadcasted_iota(I32, (tile, LANES), 0) // blk

        def augment(u, carry):
            rows = pl.ds(pl.multiple_of(u * tile, tile), tile)
            q = q_ref[rows, :]
            gate = _dot(q, km_hi) + _dot(q, km_lo)
            own = u * (tile // blk) + in_tile
            g = jnp.where(lane < own, gate, -jnp.inf)
            keep = lane == own
            for _ in range(MOBA_TOPK):
                mx = jnp.max(g, axis=-1, keepdims=True)
                cand = jnp.where((g == mx) & (mx > -jnp.inf), lane, LANES)
                pick = lane == jnp.min(cand, axis=-1, keepdims=True)
                keep = keep | pick
                g = jnp.where(pick, -jnp.inf, g)
            feat = jnp.where(lane < FEAT_BLOCKS, jnp.where(keep, 0.0, 1.0), aq_ref[0])
            qa_sc[rows, :MOBA_HEAD_DIM] = q
            qa_sc[rows, MOBA_HEAD_DIM:] = feat.astype(BF16)
            return carry

        lax.fori_loop(0, q_ref.shape[0] // tile, augment, 0)

    _flash_sweep((qa_sc, kt_ref, v_ref, o_ref, viol_sc) + bufs, (ti_ref, tj_ref, st_ref), n_items,
                 tile=tile, dv=MOBA_HEAD_DIM)


def _moba_attn(qm, kt, vm, kmt, aq, w_f32, batch, seq, tile):
    dh = MOBA_HEAD_DIM
    return _attn_call(functools.partial(_moba_attn_kernel, tile=tile), "moba_attn", dh,
                      [pl.BlockSpec((1, dh, LANES), lambda b, h, g, *_: (b * MOBA_HEADS + h, 0, 0)),
                       pl.BlockSpec((1, 1, LANES), lambda b, h, g, *_: (h, 0, 0))],
                      [pltpu.VMEM((seq, QK_WIDTH), BF16)],
                      (qm, kt, vm, kmt, aq), w_f32, batch, MOBA_HEADS, seq, tile, dh)


def _outproj_kernel(om_ref, ol_ref, gm_ref, gl_ref, wo_ref, x_ref, gf_ref, wrh_ref, wrl_ref, br_ref,
                    x1_ref, hp_ref, lg_ref):
    wm = om_ref.shape[1]
    a = _rms(om_ref[...], gm_ref[...]).astype(BF16)
    b = _rms(ol_ref[...], gl_ref[...]).astype(BF16)
    x1 = x_ref[...] + _dot(a, wo_ref[:wm, :]) + _dot(b, wo_ref[wm:, :])
    x1_ref[...] = x1
    h2 = _rms(x1, gf_ref[...])
    half = h2.shape[1] // 2
    _store_row_tiles(hp_ref, _pack_halves(h2[:, :half], h2[:, half:]))
    h_hi = h2.astype(BF16)
    h_lo = (h2 - h_hi.astype(F32)).astype(BF16)
    lg_ref[...] = (_dot(h_hi, wrh_ref[...]) + _dot(h_lo, wrh_ref[...]) + _dot(h_hi, wrl_ref[...])
                   + br_ref[...])


def _out_proj(om, ol, gm, gl, wo, x2, gf, wr_hi, wr_lo, br, tm):
    t, d = x2.shape
    wm = om.shape[1]
    wl = ol.shape[1]
    row = lambda i: (i, 0)
    const = lambda i: (0, 0)
    vmem = 2 * (wm + wl) * d * 2 + 2 * tm * (wm + wl + 2 * d) * 4 + 2 * tm * d * 2 + 6 * tm * d * 4 + (4 << 20)
    return pl.pallas_call(
        _outproj_kernel,
        grid=(t // tm,),
        in_specs=[pl.BlockSpec((tm, wm), row), pl.BlockSpec((tm, wl), row),
                  pl.BlockSpec((1, wm), const), pl.BlockSpec((1, wl), const),
                  pl.BlockSpec((wm + wl, d), const), pl.BlockSpec((tm, d), row), pl.BlockSpec((1, d), const),
                  pl.BlockSpec((d, LANES), const), pl.BlockSpec((d, LANES), const), pl.BlockSpec((1, LANES), const)],
        out_specs=[pl.BlockSpec((tm, d), row), pl.BlockSpec((tm * SUBLANES, LANES), row),
                   pl.BlockSpec((tm, LANES), row)],
        out_shape=[jax.ShapeDtypeStruct((t, d), F32), jax.ShapeDtypeStruct((t * SUBLANES, LANES), U32),
                   jax.ShapeDtypeStruct((t, LANES), F32)],
        compiler_params=_params(("parallel",), vmem), name="out_proj",
    )(om, ol, gm, gl, wo, x2, gf, wr_hi, wr_lo, br)


def _router_kernel(lg_ref, idx_ref, gate_ref, pos_ref, cnt_ref, *, tm):
    i = pl.program_id(0)

    @pl.when(i == 0)
    def _():
        cnt_ref[...] = jnp.zeros_like(cnt_ref)

    lane = lax.broadcasted_iota(I32, (tm, LANES), 1)
    work = jnp.where(lane < N_EXPERTS, lg_ref[...], -jnp.inf)
    vals, firsts, picks = [], [], []
    for _ in range(TOP_K):
        mx = jnp.max(work, axis=-1, keepdims=True)
        first = jnp.min(jnp.where(work == mx, lane, LANES), axis=-1, keepdims=True)
        pick = lane == first
        work = jnp.where(pick, -jnp.inf, work)
        vals.append(mx)
        firsts.append(first)
        picks.append(pick)
    es = [jnp.exp(v - vals[0]) for v in vals]
    denom = es[0] + es[1] + es[2] + es[3]
    chosen = jnp.zeros((tm, LANES), F32)
    for p in picks:
        chosen = jnp.where(p, 1.0, chosen)
    r = lax.broadcasted_iota(I32, (tm, tm), 0)
    cc = lax.broadcasted_iota(I32, (tm, tm), 1)
    before = (r > cc).astype(BF16)
    prior = _dot(before, chosen.astype(BF16)) + cnt_ref[...]
    lane4 = lax.broadcasted_iota(I32, (tm, TOP_K), 1)
    idx = jnp.zeros((tm, TOP_K), I32)
    gate = jnp.zeros((tm, TOP_K), F32)
    pos = jnp.zeros((tm, TOP_K), I32)
    for j in range(TOP_K):
        pj = jnp.sum(jnp.where(picks[j], prior, 0.0), axis=-1, keepdims=True)
        idx = jnp.where(lane4 == j, firsts[j], idx)
        gate = jnp.where(lane4 == j, es[j] / denom, gate)
        pos = jnp.where(lane4 == j, pj.astype(I32), pos)
    idx_ref[...] = idx
    gate_ref[...] = gate
    pos_ref[...] = pos
    cnt_ref[...] += jnp.sum(chosen, axis=0, keepdims=True)


def _router(logits, tm):
    t = logits.shape[0]
    row = lambda i: (i, 0)
    vmem = 4 * tm * tm * 4 + 24 * tm * LANES * 4 + (4 << 20)
    return pl.pallas_call(
        functools.partial(_router_kernel, tm=tm),
        grid=(t // tm,),
        in_specs=[pl.BlockSpec((tm, LANES), row)],
        out_specs=[pl.BlockSpec((tm, TOP_K), row)] * 3 + [pl.BlockSpec((1, LANES), lambda i: (0, 0))],
        out_shape=[jax.ShapeDtypeStruct((t, TOP_K), I32), jax.ShapeDtypeStruct((t, TOP_K), F32),
                   jax.ShapeDtypeStruct((t, TOP_K), I32), jax.ShapeDtypeStruct((1, LANES), F32)],
        compiler_params=_params(("arbitrary",), vmem), name="router",
    )(logits)


def _moe_up_kernel(be_ref, nu_ref, src_ref, hp_ref, w1_ref, b1_ref, act_ref, xs_sc, sem, *, fc):
    i = pl.program_id(0)
    blk = act_ref.shape[0]
    slot = i % 2

    def row_copy(block, r, s):
        return pltpu.make_async_copy(_row_tile(hp_ref, src_ref[block * blk + r]),
                                     _row_tile(xs_sc.at[s], r), sem.at[s])

    def start_block(block, s):
        def body(r, carry):
            row_copy(block, r, s).start()
            return carry
        lax.fori_loop(0, blk, body, 0, unroll=8)

    def wait_block(block, s):
        def body(r, carry):
            row_copy(block, r, s).wait()
            return carry
        lax.fori_loop(0, blk, body, 0, unroll=8)

    @pl.when(i == 0)
    def _():
        start_block(0, 0)

    @pl.when(i + 1 < nu_ref[0])
    def _():
        start_block(i + 1, 1 - slot)

    @pl.when(i < nu_ref[0])
    def _():
        wait_block(i, slot)
        hi, lo = _unpack_halves(_load_row_tiles(xs_sc.at[slot]))
        a = hi.astype(BF16)
        b = lo.astype(BF16)
        half = a.shape[1]
        d_ff = act_ref.shape[1]

        def pre(c0):
            return (_dot(a, w1_ref[0, :half, c0:c0 + fc]) + _dot(b, w1_ref[0, half:, c0:c0 + fc])
                    + b1_ref[0, :, c0:c0 + fc])

        for c in range(d_ff // fc):
            glu = jnp.minimum(pre(c * fc), SWIGLU_LIMIT)
            lin = jnp.clip(pre(d_ff + c * fc), -SWIGLU_LIMIT, SWIGLU_LIMIT)
            act = glu * jax.nn.sigmoid(SWIGLU_ALPHA * glu) * (lin + 1.0)
            act_ref[:, c * fc:(c + 1) * fc] = act.astype(BF16)

    @pl.when(i >= nu_ref[0])
    def _():
        act_ref[...] = jnp.zeros_like(act_ref)


def _moe_up(block_expert, n_used, src, hp, w1, b1):
    n_rows = src.shape[0]
    half = SUBLANES * hp.shape[1]
    e, d, f2 = w1.shape
    assert d == 2 * half
    d_ff = f2 // 2
    blk = EXPERT_BLOCK
    n_blocks = n_rows // blk
    fc = 512
    live = lambda i, be, nu: jnp.minimum(i, nu[0] - 1)
    grid_spec = pltpu.PrefetchScalarGridSpec(
        num_scalar_prefetch=3, grid=(n_blocks,),
        in_specs=[pl.BlockSpec(memory_space=pl.ANY),
                  pl.BlockSpec((1, d, f2), lambda i, be, nu, src: (be[live(i, be, nu)], 0, 0)),
                  pl.BlockSpec((1, 1, f2), lambda i, be, nu, src: (be[live(i, be, nu)], 0, 0))],
        out_specs=pl.BlockSpec((blk, d_ff), lambda i, be, nu, src: (i, 0)),
        scratch_shapes=[pltpu.VMEM((2, blk * SUBLANES, LANES), U32), pltpu.SemaphoreType.DMA((2,))])
    vmem = 2 * d * f2 * 2 + 2 * blk * half * 4 + 2 * blk * d_ff * 2 + 8 * blk * fc * 4 + 2 * blk * d * 2 + (4 << 20)
    return pl.pallas_call(
        functools.partial(_moe_up_kernel, fc=fc),
        grid_spec=grid_spec,
        out_shape=jax.ShapeDtypeStruct((n_rows, d_ff), BF16),
        compiler_params=_params(("arbitrary",), vmem), name="moe_up",
    )(block_expert, n_used, src, hp, w1, b1)


def _moe_down_kernel(be_ref, nu_ref, a_ref, w2_ref, b2_ref, y_ref):
    i = pl.program_id(0)

    @pl.when(i < nu_ref[0])
    def _():
        y = _dot(a_ref[...], w2_ref[0]) + b2_ref[0]
        half = y.shape[1] // 2
        _store_row_tiles(y_ref, _pack_halves(y[:, :half], y[:, half:]))

    @pl.when(i >= nu_ref[0])
    def _():
        y_ref[...] = jnp.zeros_like(y_ref)


def _moe_down(block_expert, n_used, act, w2, b2):
    n_rows, d_ff = act.shape
    e, _, d = w2.shape
    blk = EXPERT_BLOCK
    n_blocks = n_rows // blk
    live = lambda i, be, nu: jnp.minimum(i, nu[0] - 1)
    grid_spec = pltpu.PrefetchScalarGridSpec(
        num_scalar_prefetch=2, grid=(n_blocks,),
        in_specs=[pl.BlockSpec((blk, d_ff), lambda i, be, nu: (live(i, be, nu), 0)),
                  pl.BlockSpec((1, d_ff, d), lambda i, be, nu: (be[live(i, be, nu)], 0, 0)),
                  pl.BlockSpec((1, 1, d), lambda i, be, nu: (be[live(i, be, nu)], 0, 0))],
        out_specs=pl.BlockSpec((blk * SUBLANES, LANES), lambda i, be, nu: (i, 0)))
    vmem = 2 * d_ff * d * 2 + 2 * blk * d_ff * 2 + 2 * blk * d * 2 + 4 * blk * d * 4 + (4 << 20)
    return pl.pallas_call(
        _moe_down_kernel,
        grid_spec=grid_spec,
        out_shape=jax.ShapeDtypeStruct((n_rows * SUBLANES, LANES), U32),
        compiler_params=_params(("arbitrary",), vmem), name="moe_down",
    )(block_expert, n_used, act, w2, b2)


def _combine_kernel(dest_ref, gate_ref, x1_ref, gfin_ref, yb_ref, o_ref, rows_sc, sem, *, tm):
    i = pl.program_id(0)
    slot = i % 2

    def row_copy(tile, r, j, s):
        return pltpu.make_async_copy(_row_tile(yb_ref, dest_ref[(tile * tm + r) * TOP_K + j]),
                                     _row_tile(rows_sc.at[s, j], r), sem.at[s])

    def start_tile(tile, s):
        def body(r, carry):
            for j in range(TOP_K):
                row_copy(tile, r, j, s).start()
            return carry
        lax.fori_loop(0, tm, body, 0, unroll=4)

    def wait_tile(tile, s):
        def body(r, carry):
            for j in range(TOP_K):
                row_copy(tile, r, j, s).wait()
            return carry
        lax.fori_loop(0, tm, body, 0, unroll=4)

    @pl.when(i == 0)
    def _():
        start_tile(0, 0)

    @pl.when(i + 1 < pl.num_programs(0))
    def _():
        start_tile(i + 1, 1 - slot)

    wait_tile(i, slot)

    x1 = x1_ref[...]
    half = x1.shape[1] // 2
    gates = gate_ref[...]
    left = x1[:, :half]
    right = x1[:, half:]
    for j in range(TOP_K):
        hi, lo = _unpack_halves(_load_row_tiles(rows_sc.at[slot, j]))
        gj = gates[:, j:j + 1]
        left = left + gj * hi
        right = right + gj * lo
    ms = (jnp.sum(left * left, axis=-1, keepdims=True) + jnp.sum(right * right, axis=-1, keepdims=True)) / (2 * half)
    inv = lax.rsqrt(ms + NORM_EPS)
    g = gfin_ref[...]
    o_ref[:, :half] = left * inv * g[:, :half]
    o_ref[:, half:] = right * inv * g[:, half:]


def _combine(dest, gates, x1, gfin, yb, tm):
    t, d = x1.shape
    half = d // 2
    grid_spec = pltpu.PrefetchScalarGridSpec(
        num_scalar_prefetch=1, grid=(t // tm,),
        in_specs=[pl.BlockSpec((tm, TOP_K), lambda i, dst: (i, 0)),
                  pl.BlockSpec((tm, d), lambda i, dst: (i, 0)),
                  pl.BlockSpec((1, d), lambda i, dst: (0, 0)),
                  pl.BlockSpec(memory_space=pl.ANY)],
        out_specs=pl.BlockSpec((tm, d), lambda i, dst: (i, 0)),
        scratch_shapes=[pltpu.VMEM((2, TOP_K, tm * SUBLANES, LANES), U32), pltpu.SemaphoreType.DMA((2,))])
    vmem = 2 * TOP_K * tm * half * 4 + 4 * tm * d * 4 + 6 * tm * d * 4 + 2 * tm * LANES * 4 + (4 << 20)
    return pl.pallas_call(
        functools.partial(_combine_kernel, tm=tm),
        grid_spec=grid_spec,
        out_shape=jax.ShapeDtypeStruct((t, d), F32),
        compiler_params=_params(("arbitrary",), vmem), name="combine",
    )(dest, gates, x1, gfin, yb)


def _rope_tables(seq):
    inv = 1.0 / (ROPE_THETA ** (jnp.arange(0, MLA_ROPE_DIM, 2, dtype=F32) / MLA_ROPE_DIM))
    ang = jnp.arange(seq, dtype=F32)[:, None] * inv[None, :]
    ang = jnp.concatenate([ang, ang], axis=-1)
    ones = jnp.ones((seq, MLA_NOPE_DIM), F32)
    zeros = jnp.zeros((seq, QK_WIDTH - MLA_NOPE_DIM - MLA_ROPE_DIM), F32)
    cos_t = jnp.concatenate([ones, jnp.cos(ang), zeros], axis=-1)
    sin_t = jnp.concatenate([0.0 * ones, jnp.sin(ang), zeros], axis=-1)
    return cos_t, sin_t


def _bf16_pieces(x, n=3):
    out = []
    for _ in range(n):
        p = float(np.asarray(x, dtype=np.float32).astype(jnp.bfloat16).astype(np.float32))
        out.append(p)
        x = x - p
    return out


def _moba_features(seq, tile):
    pos = np.arange(seq)
    blk, off = pos // MOBA_BLOCK, pos % MOBA_BLOCK
    assert seq // MOBA_BLOCK <= FEAT_BLOCKS
    feat = np.zeros((QK_WIDTH - MOBA_HEAD_DIM, seq), np.float32)
    feat[:FEAT_BLOCKS] = np.where(blk[None, :] == np.arange(FEAT_BLOCKS)[:, None], MASKED, 0.0)
    feat[FEAT_ALIBI:FEAT_ALIBI + 3] = blk[None, :]
    feat[FEAT_ALIBI + 3:FEAT_ALIBI + 6] = off[None, :]
    feat = feat.reshape(feat.shape[0], seq // tile, tile).transpose(1, 0, 2)
    aq = np.zeros((MOBA_HEADS, 1, LANES), np.float32)
    for h in range(MOBA_HEADS):
        slope = 2.0 ** (-8.0 * (h + 1) / MOBA_HEADS)
        aq[h, 0, FEAT_ALIBI:FEAT_ALIBI + 3] = _bf16_pieces(slope * LOG2E * MOBA_BLOCK)
        aq[h, 0, FEAT_ALIBI + 3:FEAT_ALIBI + 6] = _bf16_pieces(slope * LOG2E)
    return jnp.asarray(feat, BF16), jnp.asarray(aq, F32)


def _rot_cols(w):
    half = w.shape[-1] // 2
    return jnp.concatenate([-w[..., half:], w[..., :half]], axis=-1)


def _rope_slot(w):
    k = w.shape[0]
    return jnp.concatenate([jnp.zeros((k, MLA_NOPE_DIM), w.dtype), w,
                            jnp.zeros((k, QK_WIDTH - MLA_NOPE_DIM - MLA_ROPE_DIM), w.dtype)], axis=-1)


def _layer(x2, batch, seq, norm_attn_g, w_in, moba_out_g, q_a_norm_g, kv_a_norm_g, w_uq, w_ukv, mla_out_g,
           w_o, norm_ffn_g, w_router, b_router, w1, b1, w2, b2, final_g):
    t, d = x2.shape
    wm = MOBA_HEADS * MOBA_HEAD_DIM
    tile = ATT_TILE
    assert seq % tile == 0 and tile % MOBA_BLOCK == 0
    cos_t, sin_t = _rope_tables(seq)
    feat, aq = _moba_features(seq, tile)

    c_kr = 3 * wm + MLA_Q_RANK + MLA_KV_RANK
    w_kr = w_in[:, c_kr:c_kr + MLA_ROPE_DIM]
    w_all = jnp.concatenate([w_in[:, :c_kr], _rope_slot(w_kr), _rope_slot(_rot_cols(w_kr))], axis=1).astype(BF16)
    wq = w_uq.reshape(MLA_Q_RANK, MLA_HEADS, MLA_NOPE_DIM + MLA_ROPE_DIM)
    zq = jnp.zeros((MLA_Q_RANK, MLA_HEADS, QK_WIDTH - MLA_NOPE_DIM - MLA_ROPE_DIM), F32)
    wqa = jnp.concatenate([wq, zq], axis=-1).reshape(MLA_Q_RANK, -1).astype(BF16)
    wqb = jnp.concatenate([jnp.zeros((MLA_Q_RANK, MLA_HEADS, MLA_NOPE_DIM), F32),
                           _rot_cols(wq[..., MLA_NOPE_DIM:]), zq], axis=-1).reshape(MLA_Q_RANK, -1).astype(BF16)
    wkv = w_ukv.reshape(MLA_KV_RANK, MLA_HEADS, MLA_NOPE_DIM + MLA_V_DIM)
    wk = jnp.concatenate([wkv[..., :MLA_NOPE_DIM],
                          jnp.zeros((MLA_KV_RANK, MLA_HEADS, QK_WIDTH - MLA_NOPE_DIM), F32)],
                         axis=-1).reshape(MLA_KV_RANK, -1).astype(BF16)
    wv = wkv[..., MLA_NOPE_DIM:].reshape(MLA_KV_RANK, -1).astype(BF16)
    wr = jnp.pad(w_router, ((0, 0), (0, LANES - N_EXPERTS)))
    wr_hi = wr.astype(BF16)
    wr_lo = (wr - wr_hi.astype(F32)).astype(BF16)
    br = jnp.pad(b_router, (0, LANES - N_EXPERTS))[None, :]

    qm, ktm, vm, kmean, cq, ckv, kr = _in_proj(x2, norm_attn_g[None, :], w_all, cos_t, sin_t, feat, seq, tm=tile)
    qc, ktl, vl = _mla_up(cq, ckv, q_a_norm_g[None, :], kv_a_norm_g[None, :], wqa, wqb, wk, wv, kr,
                          cos_t, sin_t, seq, tm=tile)
    n_blk = seq // MOBA_BLOCK
    kmt = kmean.reshape(batch, n_blk, MOBA_HEADS, MOBA_HEAD_DIM).transpose(0, 2, 3, 1)
    kmt = jnp.pad(kmt, ((0, 0), (0, 0), (0, 0), (0, LANES - n_blk))).reshape(batch * MOBA_HEADS, MOBA_HEAD_DIM, LANES)
    o_moba, w2b = _moba_attn(qm, ktm, vm, kmt, aq, w2, batch, seq, tile)
    o_mla, w1b = _mla_attn(qc, ktl, vl, w1, batch, seq, tile)
    x1, hp, logits = _out_proj(o_moba, o_mla, moba_out_g[None, :], mla_out_g[None, :], w_o.astype(BF16), x2,
                               norm_ffn_g[None, :], wr_hi, wr_lo, br, tm=256)

    idx, gates, pos, counts = _router(logits, tm=min(512, t))
    blk = EXPERT_BLOCK
    n_blocks = -(-(t * TOP_K) // blk) + N_EXPERTS
    counts = counts[0, :N_EXPERTS].astype(I32)
    padded = ((counts + blk - 1) // blk) * blk
    pad_end = jnp.cumsum(padded)
    pad_start = pad_end - padded
    dest = (pad_start[idx] + pos).reshape(-1)
    block_expert = jnp.minimum(
        jnp.sum((jnp.arange(n_blocks, dtype=I32) * blk)[:, None] >= pad_end[None, :], axis=1), N_EXPERTS - 1).astype(I32)
    n_used = (pad_end[-1:] // blk).astype(I32)
    src = jnp.zeros((n_blocks * blk,), I32).at[dest].set(jnp.arange(t * TOP_K, dtype=I32) // TOP_K,
                                                          unique_indices=True)

    act = _moe_up(block_expert, n_used, src, hp, w1b, b1[:, None, :])
    yb = _moe_down(block_expert, n_used, act, w2b, b2[:, None, :])
    return _combine(dest, gates, x1, final_g[None, :], yb, tm=256)


def kernel(x, norm_attn_g, w_in, moba_out_g, q_a_norm_g, kv_a_norm_g, w_uq, w_ukv, mla_out_g, w_o, norm_ffn_g,
           w_router, b_router, w1, b1, w2, b2, norm_final_g):
    batch, seq, d = x.shape
    depth = w_in.shape[0]
    assert depth == 1, "the final norm is fused into the last layer's combine step"
    x2 = x.reshape(batch * seq, d)
    out = _layer(x2, batch, seq, norm_attn_g[0], w_in[0], moba_out_g[0], q_a_norm_g[0], kv_a_norm_g[0], w_uq[0],
                 w_ukv[0], mla_out_g[0], w_o[0], norm_ffn_g[0], w_router[0], b_router[0], w1[0], b1[0], w2[0],
                 b2[0], norm_final_g)
    return out.reshape(batch, seq, d)
```

```python
import functools

import numpy as np
import jax
import jax.numpy as jnp
from jax import lax
from jax.experimental import pallas as pl
from jax.experimental.pallas import tpu as pltpu

MOBA_HEADS = 8
MOBA_HEAD_DIM = 128
MOBA_BLOCK = 256
MOBA_TOPK = 3
MLA_HEADS = 8
MLA_NOPE_DIM = 128
MLA_ROPE_DIM = 64
MLA_V_DIM = 128
MLA_Q_RANK = 512
MLA_KV_RANK = 512
ROPE_THETA = 10000.0
N_EXPERTS = 32
TOP_K = 4
SWIGLU_LIMIT = 7.0
SWIGLU_ALPHA = 1.702
EXPERT_BLOCK = 256
NORM_EPS = 1e-5

LANES = 128
SUBLANES = 8
QK_WIDTH = 256
ATT_TILE = 512
FLASH_STRIPS = 2
LOG2E = 1.4426950408889634
MASKED = -2.0 ** 100
M_INIT = -2.0 ** 98
FEAT_BLOCKS = 64
FEAT_ALIBI = FEAT_BLOCKS
V7X_VMEM_BUDGET = 56 * 1024 * 1024

F32 = jnp.float32
BF16 = jnp.bfloat16
U32 = jnp.uint32
I32 = jnp.int32


def _params(semantics, vmem_bytes):
    return pltpu.CompilerParams(dimension_semantics=semantics,
                                vmem_limit_bytes=min(int(vmem_bytes), V7X_VMEM_BUDGET))


def _rms(xf, g):
    ms = jnp.mean(xf * xf, axis=-1, keepdims=True)
    return xf * lax.rsqrt(ms + NORM_EPS) * g


def _dot(a, b):
    return jnp.dot(a, b, preferred_element_type=F32)


def _pack_halves(a, b):
    ai = lax.bitcast_convert_type(a.astype(BF16).astype(F32), U32)
    bi = lax.bitcast_convert_type(b.astype(BF16).astype(F32), U32)
    return ai | (bi >> 16)


def _unpack_halves(w):
    hi = lax.bitcast_convert_type(w & jnp.uint32(0xFFFF0000), F32)
    lo = lax.bitcast_convert_type(w << 16, F32)
    return hi, lo


def _store_row_tiles(ref, val):
    rows = val.shape[0]
    for s in range(SUBLANES):
        ref[pl.ds(s, rows, stride=SUBLANES), :] = val[:, s * LANES:(s + 1) * LANES]


def _load_row_tiles(ref):
    rows = ref.shape[0] // SUBLANES
    return jnp.concatenate([ref[pl.ds(s, rows, stride=SUBLANES), :] for s in range(SUBLANES)], axis=1)


def _row_tile(ref, r):
    return ref.at[pl.ds(pl.multiple_of(r * SUBLANES, SUBLANES), SUBLANES)]


def _inproj_kernel(x_ref, g_ref, w_ref, cos_ref, sin_ref, feat_ref,
                   qm_ref, qt_ref, kt_ref, vm_ref, kmean_ref, cq_ref, ckv_ref, kr_ref, *, q_scale):
    h = _rms(x_ref[...], g_ref[...]).astype(BF16)
    w = MOBA_HEADS * MOBA_HEAD_DIM
    dh = MOBA_HEAD_DIM
    tm = h.shape[0]

    def mm(lo, hi):
        return _dot(h, w_ref[:, lo:hi])

    q = mm(0, w) * q_scale
    qm_ref[...] = q.astype(BF16)
    k = mm(w, 2 * w)
    for b in range(tm // MOBA_BLOCK):
        kmean_ref[b] = jnp.mean(k[b * MOBA_BLOCK:(b + 1) * MOBA_BLOCK, :], axis=0, keepdims=True)
    v = mm(2 * w, 3 * w).astype(BF16)
    ones = jnp.ones((tm, QK_WIDTH - dh), BF16)
    for hh in range(MOBA_HEADS):
        qt_ref[hh, 0] = q[:, hh * dh:(hh + 1) * dh].T.astype(BF16)
        kt_ref[hh, 0, :dh, :] = k[:, hh * dh:(hh + 1) * dh].T.astype(BF16)
        kt_ref[hh, 0, dh:, :] = feat_ref[0]
        vm_ref[:, hh * QK_WIDTH:hh * QK_WIDTH + dh] = v[:, hh * dh:(hh + 1) * dh]
        vm_ref[:, hh * QK_WIDTH + dh:(hh + 1) * QK_WIDTH] = ones
    c0 = 3 * w
    cq_ref[...] = mm(c0, c0 + MLA_Q_RANK)
    c1 = c0 + MLA_Q_RANK
    ckv_ref[...] = mm(c1, c1 + MLA_KV_RANK)
    c2 = c1 + MLA_KV_RANK
    kr = mm(c2, c2 + QK_WIDTH) * cos_ref[...] + mm(c2 + QK_WIDTH, c2 + 2 * QK_WIDTH) * sin_ref[...]
    kr_ref[...] = kr.astype(BF16)


def _in_proj(x2, g, w_all, cos_t, sin_t, feat, seq, tm):
    t, d = x2.shape
    w = MOBA_HEADS * MOBA_HEAD_DIM
    hw = MOBA_HEADS * QK_WIDTH
    nc = w_all.shape[1]
    n_pos = seq // tm
    nb = tm // MOBA_BLOCK
    row = lambda i: (i, 0)
    const = lambda i: (0, 0)
    pos = lambda i: (i % n_pos, 0)
    out_shape = [jax.ShapeDtypeStruct((t, w), BF16),
                 jax.ShapeDtypeStruct((MOBA_HEADS, t // tm, MOBA_HEAD_DIM, tm), BF16),
                 jax.ShapeDtypeStruct((MOBA_HEADS, t // tm, QK_WIDTH, tm), BF16),
                 jax.ShapeDtypeStruct((t, hw), BF16),
                 jax.ShapeDtypeStruct((t // MOBA_BLOCK, 1, w), F32),
                 jax.ShapeDtypeStruct((t, MLA_Q_RANK), F32), jax.ShapeDtypeStruct((t, MLA_KV_RANK), F32),
                 jax.ShapeDtypeStruct((t, QK_WIDTH), BF16)]
    out_specs = [pl.BlockSpec((tm, w), row),
                 pl.BlockSpec((MOBA_HEADS, 1, MOBA_HEAD_DIM, tm), lambda i: (0, i, 0, 0)),
                 pl.BlockSpec((MOBA_HEADS, 1, QK_WIDTH, tm), lambda i: (0, i, 0, 0)),
                 pl.BlockSpec((tm, hw), row),
                 pl.BlockSpec((nb, 1, w), lambda i: (i, 0, 0)),
                 pl.BlockSpec((tm, MLA_Q_RANK), row), pl.BlockSpec((tm, MLA_KV_RANK), row),
                 pl.BlockSpec((tm, QK_WIDTH), row)]
    vmem = (2 * tm * d * 4 + d * nc * 2 + 2 * tm * (w * 2 + 2 * hw * 2 + 2 * 512 * 4 + 256 * 2)
            + 4 * tm * QK_WIDTH * 4 + tm * d * 2 + 3 * tm * w * 4 + (4 << 20))
    return pl.pallas_call(
        functools.partial(_inproj_kernel, q_scale=MOBA_HEAD_DIM ** -0.5 * LOG2E),
        grid=(t // tm,),
        in_specs=[pl.BlockSpec((tm, d), row), pl.BlockSpec((1, d), const),
                  pl.BlockSpec((d, nc), const, pipeline_mode=pl.Buffered(1)),
                  pl.BlockSpec((tm, QK_WIDTH), pos), pl.BlockSpec((tm, QK_WIDTH), pos),
                  pl.BlockSpec((1, QK_WIDTH - MOBA_HEAD_DIM, tm), lambda i: (i % n_pos, 0, 0))],
        out_specs=out_specs, out_shape=out_shape,
        compiler_params=_params(("parallel",), vmem), name="in_proj",
    )(x2, g, w_all, cos_t, sin_t, feat)


def _mlaup_kernel(cq_ref, ckv_ref, gq_ref, gkv_ref, wqa_ref, wqb_ref, wk_ref, wv_ref,
                  kr_ref, cos_ref, sin_ref, q_ref, kt_ref, v_ref, *, q_scale):
    nq = _rms(cq_ref[...], gq_ref[...]).astype(BF16)
    nkv = _rms(ckv_ref[...], gkv_ref[...]).astype(BF16)
    cos = cos_ref[...]
    sin = sin_ref[...]
    kr = kr_ref[...].astype(F32)
    ones = jnp.ones((nq.shape[0], QK_WIDTH - MLA_V_DIM), BF16)
    for hh in range(MLA_HEADS):
        cols = slice(hh * QK_WIDTH, (hh + 1) * QK_WIDTH)
        q = _dot(nq, wqa_ref[:, cols]) * cos + _dot(nq, wqb_ref[:, cols]) * sin
        q_ref[:, cols] = (q * q_scale).astype(BF16)
        kt_ref[hh, 0] = (_dot(nkv, wk_ref[:, cols]) + kr).T.astype(BF16)
        v_ref[:, hh * QK_WIDTH:hh * QK_WIDTH + MLA_V_DIM] = _dot(
            nkv, wv_ref[:, hh * MLA_V_DIM:(hh + 1) * MLA_V_DIM]).astype(BF16)
        v_ref[:, hh * QK_WIDTH + MLA_V_DIM:(hh + 1) * QK_WIDTH] = ones


def _mla_up(cq, ckv, gq, gkv, wqa, wqb, wk, wv, kr, cos_t, sin_t, seq, tm):
    t = cq.shape[0]
    n_pos = seq // tm
    row = lambda i: (i, 0)
    const = lambda i: (0, 0)
    pos = lambda i: (i % n_pos, 0)
    hq = MLA_HEADS * QK_WIDTH
    hv = MLA_HEADS * MLA_V_DIM
    scale = (MLA_NOPE_DIM + MLA_ROPE_DIM) ** -0.5 * LOG2E
    vmem = (4 * tm * 512 * 4 + 2 * (3 * 512 * hq * 2 + 512 * hv * 2) + 8 * tm * QK_WIDTH * 4
            + 2 * tm * 3 * hq * 2 + (8 << 20))
    return pl.pallas_call(
        functools.partial(_mlaup_kernel, q_scale=scale),
        grid=(t // tm,),
        in_specs=[pl.BlockSpec((tm, MLA_Q_RANK), row), pl.BlockSpec((tm, MLA_KV_RANK), row),
                  pl.BlockSpec((1, MLA_Q_RANK), const), pl.BlockSpec((1, MLA_KV_RANK), const),
                  pl.BlockSpec((MLA_Q_RANK, hq), const), pl.BlockSpec((MLA_Q_RANK, hq), const),
                  pl.BlockSpec((MLA_KV_RANK, hq), const), pl.BlockSpec((MLA_KV_RANK, hv), const),
                  pl.BlockSpec((tm, QK_WIDTH), row),
                  pl.BlockSpec((tm, QK_WIDTH), pos), pl.BlockSpec((tm, QK_WIDTH), pos)],
        out_specs=[pl.BlockSpec((tm, hq), row),
                   pl.BlockSpec((MLA_HEADS, 1, QK_WIDTH, tm), lambda i: (0, i, 0, 0)),
                   pl.BlockSpec((tm, hq), row)],
        out_shape=[jax.ShapeDtypeStruct((t, hq), BF16),
                   jax.ShapeDtypeStruct((MLA_HEADS, t // tm, QK_WIDTH, tm), BF16),
                   jax.ShapeDtypeStruct((t, hq), BF16)],
        compiler_params=_params(("parallel",), vmem), name="mla_up",
    )(cq, ckv, gq, gkv, wqa, wqb, wk, wv, kr, cos_t, sin_t)


def _flash_stages(refs, item_tables, *, pv=None, softmax=None, scores=None, tile, dv):
    q_ref, kt_ref, v_ref, o_ref, bias_sc, l_all, m_all, s_sc, p_sc, alpha_sc = refs
    ti_ref, tj_ref = item_tables
    rows = tile // FLASH_STRIPS

    def q_rows(t, r):
        return pl.ds(pl.multiple_of(ti_ref[t] * tile + r * rows, rows), rows)

    for r in range(FLASH_STRIPS):
        sl = slice(r * rows, (r + 1) * rows)
        if pv is not None:
            qr = q_rows(pv, r)
            res = _dot(p_sc[sl], v_ref[pl.ds(pl.multiple_of(tj_ref[pv] * tile, tile), tile), :])
            a = alpha_sc[sl]
            o_ref[qr, :] = a * o_ref[qr, :] + res[:, :dv]
            l_all[qr, :] = a * l_all[qr, :] + res[:, dv:2 * dv]
        if softmax is not None:
            qr = q_rows(softmax, r)
            diag = (ti_ref[softmax] == tj_ref[softmax]).astype(I32)
            s = s_sc[sl] + bias_sc[diag, sl]
            m_old = m_all[qr, :]
            m_new = jnp.maximum(m_old, jnp.max(s, axis=-1, keepdims=True))
            p_sc[sl] = jnp.exp2(s - m_new).astype(BF16)
            alpha_sc[sl] = jnp.exp2(m_old - m_new)
            m_all[qr, :] = m_new
        if scores is not None:
            s_sc[sl] = _dot(q_ref[q_rows(scores, r), :], kt_ref[0, tj_ref[scores]])


def _flash_sweep(refs, tables, n_items, *, tile, dv):
    ti_ref, tj_ref, st_ref = tables
    q_ref, kt_ref, v_ref, o_ref, bias_sc, l_all, m_all, s_sc, p_sc, alpha_sc = refs
    g = pl.program_id(2)
    last = pl.num_programs(2) - 1
    stages = functools.partial(_flash_stages, refs, (ti_ref, tj_ref), tile=tile, dv=dv)

    @pl.when(g == 0)
    def _():
        o_ref[...] = jnp.zeros_like(o_ref)
        l_all[...] = jnp.zeros_like(l_all)
        m_all[...] = jnp.full_like(m_all, M_INIT)
        stages(scores=0)
        stages(softmax=0, scores=1)

    def steady(t, carry):
        stages(pv=t - 2, softmax=t - 1, scores=t)
        return carry

    lax.fori_loop(jnp.maximum(st_ref[g], 2), st_ref[g + 1], steady, 0)

    @pl.when(g == last)
    def _():
        stages(pv=n_items - 2, softmax=n_items - 1)
        stages(pv=n_items - 1)

        def finish(u, carry):
            rows = pl.ds(pl.multiple_of(u * tile, tile), tile)
            o_ref[rows, :] = o_ref[rows, :] / l_all[rows, :]
            return carry

        lax.fori_loop(0, o_ref.shape[0] // tile, finish, 0)


def _attn_items(nq):
    ti = np.array([i for j in range(nq) for i in range(j, nq)], np.int32)
    tj = np.array([j for j in range(nq) for i in range(j, nq)], np.int32)
    st = np.array([(g * len(ti)) // nq for g in range(nq + 1)], np.int32)
    return jnp.asarray(ti), jnp.asarray(tj), jnp.asarray(st), len(ti)


def _mla_attn_kernel(ti_ref, tj_ref, st_ref, q_ref, kt_ref, v_ref, w_ref, o_ref, wb_ref,
                     bias_sc, *bufs, tile, n_items):
    wb_ref[...] = w_ref[...].astype(BF16)

    @pl.when(pl.program_id(2) == 0)
    def _():
        row = lax.broadcasted_iota(I32, (tile, tile), 0)
        col = lax.broadcasted_iota(I32, (tile, tile), 1)
        bias_sc[0] = jnp.zeros((tile, tile), F32)
        bias_sc[1] = jnp.where(col > row, MASKED, 0.0)

    _flash_sweep((q_ref, kt_ref, v_ref, o_ref, bias_sc) + bufs, (ti_ref, tj_ref, st_ref), n_items,
                 tile=tile, dv=MLA_V_DIM)


def _attn_call(kernel, name, q_width, extra_specs, scratch, args, w_f32, batch, heads, seq, tile, dv):
    t = batch * seq
    nq = seq // tile
    assert nq >= 2
    ti, tj, st, n_items = _attn_items(nq)
    steps = batch * heads * nq
    w2d = w_f32.reshape(-1, w_f32.shape[-1])
    assert w2d.shape[0] % steps == 0
    w_rows = w2d.shape[0] // steps
    w_spec = pl.BlockSpec((w_rows, w2d.shape[1]), lambda b, h, g, *_: ((b * heads + h) * nq + g, 0))
    once = pl.Buffered(1)
    grid_spec = pltpu.PrefetchScalarGridSpec(
        num_scalar_prefetch=3, grid=(batch, heads, nq),
        in_specs=[pl.BlockSpec((seq, q_width), lambda b, h, g, *_: (b, h), pipeline_mode=once),
                  pl.BlockSpec((1, nq, QK_WIDTH, tile), lambda b, h, g, *_: (h, b, 0, 0), pipeline_mode=once),
                  pl.BlockSpec((seq, QK_WIDTH), lambda b, h, g, *_: (b, h), pipeline_mode=once)]
                 + extra_specs + [w_spec],
        out_specs=[pl.BlockSpec((seq, dv), lambda b, h, g, *_: (b, h)), w_spec],
        scratch_shapes=scratch + [pltpu.VMEM((2, tile, tile), F32), pltpu.VMEM((seq, dv), F32),
                                  pltpu.VMEM((seq, 1), F32), pltpu.VMEM((tile, tile), F32),
                                  pltpu.VMEM((tile, tile), BF16), pltpu.VMEM((tile, 1), F32)])
    vmem = (seq * (q_width + 2 * QK_WIDTH) * 2 + 2 * seq * dv * 4 + seq * dv * 4 + seq * LANES * 4
            + 8 * tile * tile * 4 + 3 * w_rows * w2d.shape[1] * 6 + seq * QK_WIDTH * 2 + (4 << 20))
    out, wb = pl.pallas_call(
        functools.partial(kernel, n_items=n_items),
        grid_spec=grid_spec,
        out_shape=[jax.ShapeDtypeStruct((t, heads * dv), F32), jax.ShapeDtypeStruct(w2d.shape, BF16)],
        compiler_params=_params(("parallel", "parallel", "arbitrary"), vmem), name=name,
    )(ti, tj, st, *args, w2d)
    return out, wb.reshape(w_f32.shape)


def _mla_attn(qc, kt, v, w_f32, batch, seq, tile):
    return _attn_call(functools.partial(_mla_attn_kernel, tile=tile), "mla_attn", QK_WIDTH, [], [],
                      (qc, kt, v), w_f32, batch, MLA_HEADS, seq, tile, MLA_V_DIM)


def _moba_attn_kernel(ti_ref, tj_ref, st_ref, q_ref, kt_ref, v_ref, qt_ref, km_ref, aq_ref, w_ref, o_ref, wb_ref,
                      qa_sc, bias_sc, *bufs, tile, n_items):
    wb_ref[...] = w_ref[...].astype(BF16)
    blk = MOBA_BLOCK

    @pl.when(pl.program_id(2) == 0)
    def _():
        row = lax.broadcasted_iota(I32, (tile, tile), 0)
        col = lax.broadcasted_iota(I32, (tile, tile), 1)
        bias_sc[0] = jnp.zeros((tile, tile), F32)
        bias_sc[1] = jnp.where((row // blk == col // blk) & (col > row), MASKED, 0.0)
        km = km_ref[0]
        km_hi = km.astype(BF16)
        km_lo = (km - km_hi.astype(F32)).astype(BF16)
        blk_id = lax.broadcasted_iota(I32, (FEAT_BLOCKS, tile), 0)
        in_tile = lax.broadcasted_iota(I32, (FEAT_BLOCKS, tile), 1) // blk
        consts = jnp.broadcast_to(aq_ref[0, FEAT_BLOCKS:, :], (LANES - FEAT_BLOCKS, tile))

        def augment(u, carry):
            rows = pl.ds(pl.multiple_of(u * tile, tile), tile)
            qt = qt_ref[0, u]
            gate = (_dot(km_hi, qt) + _dot(km_lo, qt))[:FEAT_BLOCKS]
            own = u * (tile // blk) + in_tile
            g = jnp.where(blk_id < own, gate, -jnp.inf)
            keep = blk_id == own
            for _ in range(MOBA_TOPK):
                mx = jnp.max(g, axis=0, keepdims=True)
                cand = jnp.where((g == mx) & (mx > -jnp.inf), blk_id, LANES)
                pick = blk_id == jnp.min(cand, axis=0, keepdims=True)
                keep = keep | pick
                g = jnp.where(pick, -jnp.inf, g)
            feat_t = jnp.concatenate([jnp.where(keep, 0.0, 1.0), consts], axis=0)
            qa_sc[rows, :MOBA_HEAD_DIM] = q_ref[rows, :]
            qa_sc[rows, MOBA_HEAD_DIM:] = feat_t.T.astype(BF16)
            return carry

        lax.fori_loop(0, q_ref.shape[0] // tile, augment, 0)

    _flash_sweep((qa_sc, kt_ref, v_ref, o_ref, bias_sc) + bufs, (ti_ref, tj_ref, st_ref), n_items,
                 tile=tile, dv=MOBA_HEAD_DIM)


def _moba_attn(qm, qt, kt, vm, kmr, aq, w_f32, batch, seq, tile):
    dh = MOBA_HEAD_DIM
    nq = seq // tile
    return _attn_call(functools.partial(_moba_attn_kernel, tile=tile), "moba_attn", dh,
                      [pl.BlockSpec((1, nq, dh, tile), lambda b, h, g, *_: (h, b, 0, 0),
                                    pipeline_mode=pl.Buffered(1)),
                       pl.BlockSpec((1, LANES, dh), lambda b, h, g, *_: (b * MOBA_HEADS + h, 0, 0)),
                       pl.BlockSpec((1, LANES, 1), lambda b, h, g, *_: (h, 0, 0))],
                      [pltpu.VMEM((seq, QK_WIDTH), BF16)],
                      (qm, kt, vm, qt, kmr, aq), w_f32, batch, MOBA_HEADS, seq, tile, dh)


def _outproj_kernel(om_ref, ol_ref, gm_ref, gl_ref, wo_ref, x_ref, gf_ref, wrh_ref, wrl_ref, br_ref,
                    x1_ref, hp_ref, lg_ref):
    wm = om_ref.shape[1]
    a = _rms(om_ref[...], gm_ref[...]).astype(BF16)
    b = _rms(ol_ref[...], gl_ref[...]).astype(BF16)
    x1 = x_ref[...] + _dot(a, wo_ref[:wm, :]) + _dot(b, wo_ref[wm:, :])
    x1_ref[...] = x1
    h2 = _rms(x1, gf_ref[...])
    half = h2.shape[1] // 2
    _store_row_tiles(hp_ref, _pack_halves(h2[:, :half], h2[:, half:]))
    h_hi = h2.astype(BF16)
    h_lo = (h2 - h_hi.astype(F32)).astype(BF16)
    lg_ref[...] = (_dot(h_hi, wrh_ref[...]) + _dot(h_lo, wrh_ref[...]) + _dot(h_hi, wrl_ref[...])
                   + br_ref[...])


def _out_proj(om, ol, gm, gl, wo, x2, gf, wr_hi, wr_lo, br, tm):
    t, d = x2.shape
    wm = om.shape[1]
    wl = ol.shape[1]
    row = lambda i: (i, 0)
    const = lambda i: (0, 0)
    vmem = 2 * (wm + wl) * d * 2 + 2 * tm * (wm + wl + 2 * d) * 4 + 2 * tm * d * 2 + 6 * tm * d * 4 + (4 << 20)
    return pl.pallas_call(
        _outproj_kernel,
        grid=(t // tm,),
        in_specs=[pl.BlockSpec((tm, wm), row), pl.BlockSpec((tm, wl), row),
                  pl.BlockSpec((1, wm), const), pl.BlockSpec((1, wl), const),
                  pl.BlockSpec((wm + wl, d), const), pl.BlockSpec((tm, d), row), pl.BlockSpec((1, d), const),
                  pl.BlockSpec((d, LANES), const), pl.BlockSpec((d, LANES), const), pl.BlockSpec((1, LANES), const)],
        out_specs=[pl.BlockSpec((tm, d), row), pl.BlockSpec((tm * SUBLANES, LANES), row),
                   pl.BlockSpec((tm, LANES), row)],
        out_shape=[jax.ShapeDtypeStruct((t, d), F32), jax.ShapeDtypeStruct((t * SUBLANES, LANES), U32),
                   jax.ShapeDtypeStruct((t, LANES), F32)],
        compiler_params=_params(("parallel",), vmem), name="out_proj",
    )(om, ol, gm, gl, wo, x2, gf, wr_hi, wr_lo, br)


def _router_kernel(lg_ref, idx_ref, gate_ref, pos_ref, cnt_ref, *, tm):
    i = pl.program_id(0)

    @pl.when(i == 0)
    def _():
        cnt_ref[...] = jnp.zeros_like(cnt_ref)

    lane = lax.broadcasted_iota(I32, (tm, LANES), 1)
    work = jnp.where(lane < N_EXPERTS, lg_ref[...], -jnp.inf)
    vals, firsts, picks = [], [], []
    for _ in range(TOP_K):
        mx = jnp.max(work, axis=-1, keepdims=True)
        first = jnp.min(jnp.where(work == mx, lane, LANES), axis=-1, keepdims=True)
        pick = lane == first
        work = jnp.where(pick, -jnp.inf, work)
        vals.append(mx)
        firsts.append(first)
        picks.append(pick)
    es = [jnp.exp(v - vals[0]) for v in vals]
    denom = es[0] + es[1] + es[2] + es[3]
    chosen = jnp.zeros((tm, LANES), F32)
    for p in picks:
        chosen = jnp.where(p, 1.0, chosen)
    r = lax.broadcasted_iota(I32, (tm, tm), 0)
    cc = lax.broadcasted_iota(I32, (tm, tm), 1)
    before = (r > cc).astype(BF16)
    prior = _dot(before, chosen.astype(BF16)) + cnt_ref[...]
    lane4 = lax.broadcasted_iota(I32, (tm, TOP_K), 1)
    idx = jnp.zeros((tm, TOP_K), I32)
    gate = jnp.zeros((tm, TOP_K), F32)
    pos = jnp.zeros((tm, TOP_K), I32)
    for j in range(TOP_K):
        pj = jnp.sum(jnp.where(picks[j], prior, 0.0), axis=-1, keepdims=True)
        idx = jnp.where(lane4 == j, firsts[j], idx)
        gate = jnp.where(lane4 == j, es[j] / denom, gate)
        pos = jnp.where(lane4 == j, pj.astype(I32), pos)
    idx_ref[...] = idx
    gate_ref[...] = gate
    pos_ref[...] = pos
    cnt_ref[...] += jnp.sum(chosen, axis=0, keepdims=True)


def _router(logits, tm):
    t = logits.shape[0]
    row = lambda i: (i, 0)
    vmem = 4 * tm * tm * 4 + 24 * tm * LANES * 4 + (4 << 20)
    return pl.pallas_call(
        functools.partial(_router_kernel, tm=tm),
        grid=(t // tm,),
        in_specs=[pl.BlockSpec((tm, LANES), row)],
        out_specs=[pl.BlockSpec((tm, TOP_K), row)] * 3 + [pl.BlockSpec((1, LANES), lambda i: (0, 0))],
        out_shape=[jax.ShapeDtypeStruct((t, TOP_K), I32), jax.ShapeDtypeStruct((t, TOP_K), F32),
                   jax.ShapeDtypeStruct((t, TOP_K), I32), jax.ShapeDtypeStruct((1, LANES), F32)],
        compiler_params=_params(("arbitrary",), vmem), name="router",
    )(logits)


def _moe_up_kernel(be_ref, nu_ref, src_ref, hp_ref, w1_ref, b1_ref, act_ref, xs_sc, sem, *, fc):
    i = pl.program_id(0)
    blk = act_ref.shape[0]
    slot = i % 2

    def row_copy(block, r, s):
        return pltpu.make_async_copy(_row_tile(hp_ref, src_ref[block * blk + r]),
                                     _row_tile(xs_sc.at[s], r), sem.at[s])

    def start_block(block, s):
        def body(r, carry):
            row_copy(block, r, s).start()
            return carry
        lax.fori_loop(0, blk, body, 0, unroll=8)

    def wait_block(block, s):
        def body(r, carry):
            row_copy(block, r, s).wait()
            return carry
        lax.fori_loop(0, blk, body, 0, unroll=8)

    @pl.when(i == 0)
    def _():
        start_block(0, 0)

    @pl.when(i + 1 < nu_ref[0])
    def _():
        start_block(i + 1, 1 - slot)

    @pl.when(i < nu_ref[0])
    def _():
        wait_block(i, slot)
        hi, lo = _unpack_halves(_load_row_tiles(xs_sc.at[slot]))
        a = hi.astype(BF16)
        b = lo.astype(BF16)
        half = a.shape[1]
        d_ff = act_ref.shape[1]

        def pre(c0):
            return (_dot(a, w1_ref[0, :half, c0:c0 + fc]) + _dot(b, w1_ref[0, half:, c0:c0 + fc])
                    + b1_ref[0, :, c0:c0 + fc])

        for c in range(d_ff // fc):
            glu = jnp.minimum(pre(c * fc), SWIGLU_LIMIT)
            lin = jnp.clip(pre(d_ff + c * fc), -SWIGLU_LIMIT, SWIGLU_LIMIT)
            act = glu * jax.nn.sigmoid(SWIGLU_ALPHA * glu) * (lin + 1.0)
            act_ref[:, c * fc:(c + 1) * fc] = act.astype(BF16)

    @pl.when(i >= nu_ref[0])
    def _():
        act_ref[...] = jnp.zeros_like(act_ref)


def _moe_up(block_expert, n_used, src, hp, w1, b1):
    n_rows = src.shape[0]
    half = SUBLANES * hp.shape[1]
    e, d, f2 = w1.shape
    assert d == 2 * half
    d_ff = f2 // 2
    blk = EXPERT_BLOCK
    n_blocks = n_rows // blk
    fc = 512
    live = lambda i, be, nu: jnp.minimum(i, nu[0] - 1)
    grid_spec = pltpu.PrefetchScalarGridSpec(
        num_scalar_prefetch=3, grid=(n_blocks,),
        in_specs=[pl.BlockSpec(memory_space=pl.ANY),
                  pl.BlockSpec((1, d, f2), lambda i, be, nu, src: (be[live(i, be, nu)], 0, 0)),
                  pl.BlockSpec((1, 1, f2), lambda i, be, nu, src: (be[live(i, be, nu)], 0, 0))],
        out_specs=pl.BlockSpec((blk, d_ff), lambda i, be, nu, src: (i, 0)),
        scratch_shapes=[pltpu.VMEM((2, blk * SUBLANES, LANES), U32), pltpu.SemaphoreType.DMA((2,))])
    vmem = 2 * d * f2 * 2 + 2 * blk * half * 4 + 2 * blk * d_ff * 2 + 8 * blk * fc * 4 + 2 * blk * d * 2 + (4 << 20)
    return pl.pallas_call(
        functools.partial(_moe_up_kernel, fc=fc),
        grid_spec=grid_spec,
        out_shape=jax.ShapeDtypeStruct((n_rows, d_ff), BF16),
        compiler_params=_params(("arbitrary",), vmem), name="moe_up",
    )(block_expert, n_used, src, hp, w1, b1)


def _moe_down_kernel(be_ref, nu_ref, a_ref, w2_ref, b2_ref, y_ref):
    i = pl.program_id(0)

    @pl.when(i < nu_ref[0])
    def _():
        y = _dot(a_ref[...], w2_ref[0]) + b2_ref[0]
        half = y.shape[1] // 2
        _store_row_tiles(y_ref, _pack_halves(y[:, :half], y[:, half:]))

    @pl.when(i >= nu_ref[0])
    def _():
        y_ref[...] = jnp.zeros_like(y_ref)


def _moe_down(block_expert, n_used, act, w2, b2):
    n_rows, d_ff = act.shape
    e, _, d = w2.shape
    blk = EXPERT_BLOCK
    n_blocks = n_rows // blk
    live = lambda i, be, nu: jnp.minimum(i, nu[0] - 1)
    grid_spec = pltpu.PrefetchScalarGridSpec(
        num_scalar_prefetch=2, grid=(n_blocks,),
        in_specs=[pl.BlockSpec((blk, d_ff), lambda i, be, nu: (live(i, be, nu), 0)),
                  pl.BlockSpec((1, d_ff, d), lambda i, be, nu: (be[live(i, be, nu)], 0, 0)),
                  pl.BlockSpec((1, 1, d), lambda i, be, nu: (be[live(i, be, nu)], 0, 0))],
        out_specs=pl.BlockSpec((blk * SUBLANES, LANES), lambda i, be, nu: (i, 0)))
    vmem = 2 * d_ff * d * 2 + 2 * blk * d_ff * 2 + 2 * blk * d * 2 + 4 * blk * d * 4 + (4 << 20)
    return pl.pallas_call(
        _moe_down_kernel,
        grid_spec=grid_spec,
        out_shape=jax.ShapeDtypeStruct((n_rows * SUBLANES, LANES), U32),
        compiler_params=_params(("arbitrary",), vmem), name="moe_down",
    )(block_expert, n_used, act, w2, b2)


def _combine_kernel(dest_ref, gate_ref, x1_ref, gfin_ref, yb_ref, o_ref, rows_sc, sem, *, tm):
    i = pl.program_id(0)
    slot = i % 2

    def row_copy(tile, r, j, s):
        return pltpu.make_async_copy(_row_tile(yb_ref, dest_ref[(tile * tm + r) * TOP_K + j]),
                                     _row_tile(rows_sc.at[s, j], r), sem.at[s])

    def start_tile(tile, s):
        def body(r, carry):
            for j in range(TOP_K):
                row_copy(tile, r, j, s).start()
            return carry
        lax.fori_loop(0, tm, body, 0, unroll=4)

    def wait_tile(tile, s):
        def body(r, carry):
            for j in range(TOP_K):
                row_copy(tile, r, j, s).wait()
            return carry
        lax.fori_loop(0, tm, body, 0, unroll=4)

    @pl.when(i == 0)
    def _():
        start_tile(0, 0)

    @pl.when(i + 1 < pl.num_programs(0))
    def _():
        start_tile(i + 1, 1 - slot)

    wait_tile(i, slot)

    x1 = x1_ref[...]
    half = x1.shape[1] // 2
    gates = gate_ref[...]
    left = x1[:, :half]
    right = x1[:, half:]
    for j in range(TOP_K):
        hi, lo = _unpack_halves(_load_row_tiles(rows_sc.at[slot, j]))
        gj = gates[:, j:j + 1]
        left = left + gj * hi
        right = right + gj * lo
    ms = (jnp.sum(left * left, axis=-1, keepdims=True) + jnp.sum(right * right, axis=-1, keepdims=True)) / (2 * half)
    inv = lax.rsqrt(ms + NORM_EPS)
    g = gfin_ref[...]
    o_ref[:, :half] = left * inv * g[:, :half]
    o_ref[:, half:] = right * inv * g[:, half:]


def _combine(dest, gates, x1, gfin, yb, tm):
    t, d = x1.shape
    half = d // 2
    grid_spec = pltpu.PrefetchScalarGridSpec(
        num_scalar_prefetch=1, grid=(t // tm,),
        in_specs=[pl.BlockSpec((tm, TOP_K), lambda i, dst: (i, 0)),
                  pl.BlockSpec((tm, d), lambda i, dst: (i, 0)),
                  pl.BlockSpec((1, d), lambda i, dst: (0, 0)),
                  pl.BlockSpec(memory_space=pl.ANY)],
        out_specs=pl.BlockSpec((tm, d), lambda i, dst: (i, 0)),
        scratch_shapes=[pltpu.VMEM((2, TOP_K, tm * SUBLANES, LANES), U32), pltpu.SemaphoreType.DMA((2,))])
    vmem = 2 * TOP_K * tm * half * 4 + 4 * tm * d * 4 + 6 * tm * d * 4 + 2 * tm * LANES * 4 + (4 << 20)
    return pl.pallas_call(
        functools.partial(_combine_kernel, tm=tm),
        grid_spec=grid_spec,
        out_shape=jax.ShapeDtypeStruct((t, d), F32),
        compiler_params=_params(("arbitrary",), vmem), name="combine",
    )(dest, gates, x1, gfin, yb)


def _rope_tables(seq):
    inv = 1.0 / (ROPE_THETA ** (jnp.arange(0, MLA_ROPE_DIM, 2, dtype=F32) / MLA_ROPE_DIM))
    ang = jnp.arange(seq, dtype=F32)[:, None] * inv[None, :]
    ang = jnp.concatenate([ang, ang], axis=-1)
    ones = jnp.ones((seq, MLA_NOPE_DIM), F32)
    zeros = jnp.zeros((seq, QK_WIDTH - MLA_NOPE_DIM - MLA_ROPE_DIM), F32)
    cos_t = jnp.concatenate([ones, jnp.cos(ang), zeros], axis=-1)
    sin_t = jnp.concatenate([0.0 * ones, jnp.sin(ang), zeros], axis=-1)
    return cos_t, sin_t


def _bf16_pieces(x, n=3):
    out = []
    for _ in range(n):
        p = float(np.asarray(x, dtype=np.float32).astype(jnp.bfloat16).astype(np.float32))
        out.append(p)
        x = x - p
    return out


def _moba_features(seq, tile):
    pos = np.arange(seq)
    blk, off = pos // MOBA_BLOCK, pos % MOBA_BLOCK
    assert seq // MOBA_BLOCK <= FEAT_BLOCKS
    feat = np.zeros((QK_WIDTH - MOBA_HEAD_DIM, seq), np.float32)
    feat[:FEAT_BLOCKS] = np.where(blk[None, :] == np.arange(FEAT_BLOCKS)[:, None], MASKED, 0.0)
    feat[FEAT_ALIBI:FEAT_ALIBI + 3] = blk[None, :]
    feat[FEAT_ALIBI + 3:FEAT_ALIBI + 6] = off[None, :]
    feat = feat.reshape(feat.shape[0], seq // tile, tile).transpose(1, 0, 2)
    aq = np.zeros((MOBA_HEADS, 1, LANES), np.float32)
    for h in range(MOBA_HEADS):
        slope = 2.0 ** (-8.0 * (h + 1) / MOBA_HEADS)
        aq[h, 0, FEAT_ALIBI:FEAT_ALIBI + 3] = _bf16_pieces(slope * LOG2E * MOBA_BLOCK)
        aq[h, 0, FEAT_ALIBI + 3:FEAT_ALIBI + 6] = _bf16_pieces(slope * LOG2E)
    return jnp.asarray(feat, BF16), jnp.asarray(aq.transpose(0, 2, 1), F32)


def _rot_cols(w):
    half = w.shape[-1] // 2
    return jnp.concatenate([-w[..., half:], w[..., :half]], axis=-1)


def _rope_slot(w):
    k = w.shape[0]
    return jnp.concatenate([jnp.zeros((k, MLA_NOPE_DIM), w.dtype), w,
                            jnp.zeros((k, QK_WIDTH - MLA_NOPE_DIM - MLA_ROPE_DIM), w.dtype)], axis=-1)


def _layer(x2, batch, seq, norm_attn_g, w_in, moba_out_g, q_a_norm_g, kv_a_norm_g, w_uq, w_ukv, mla_out_g,
           w_o, norm_ffn_g, w_router, b_router, w1, b1, w2, b2, final_g):
    t, d = x2.shape
    wm = MOBA_HEADS * MOBA_HEAD_DIM
    tile = ATT_TILE
    assert seq % tile == 0 and tile % MOBA_BLOCK == 0
    cos_t, sin_t = _rope_tables(seq)
    feat, aq = _moba_features(seq, tile)

    c_kr = 3 * wm + MLA_Q_RANK + MLA_KV_RANK
    w_kr = w_in[:, c_kr:c_kr + MLA_ROPE_DIM]
    w_all = jnp.concatenate([w_in[:, :c_kr], _rope_slot(w_kr), _rope_slot(_rot_cols(w_kr))], axis=1).astype(BF16)
    wq = w_uq.reshape(MLA_Q_RANK, MLA_HEADS, MLA_NOPE_DIM + MLA_ROPE_DIM)
    zq = jnp.zeros((MLA_Q_RANK, MLA_HEADS, QK_WIDTH - MLA_NOPE_DIM - MLA_ROPE_DIM), F32)
    wqa = jnp.concatenate([wq, zq], axis=-1).reshape(MLA_Q_RANK, -1).astype(BF16)
    wqb = jnp.concatenate([jnp.zeros((MLA_Q_RANK, MLA_HEADS, MLA_NOPE_DIM), F32),
                           _rot_cols(wq[..., MLA_NOPE_DIM:]), zq], axis=-1).reshape(MLA_Q_RANK, -1).astype(BF16)
    wkv = w_ukv.reshape(MLA_KV_RANK, MLA_HEADS, MLA_NOPE_DIM + MLA_V_DIM)
    wk = jnp.concatenate([wkv[..., :MLA_NOPE_DIM],
                          jnp.zeros((MLA_KV_RANK, MLA_HEADS, QK_WIDTH - MLA_NOPE_DIM), F32)],
                         axis=-1).reshape(MLA_KV_RANK, -1).astype(BF16)
    wv = wkv[..., MLA_NOPE_DIM:].reshape(MLA_KV_RANK, -1).astype(BF16)
    wr = jnp.pad(w_router, ((0, 0), (0, LANES - N_EXPERTS)))
    wr_hi = wr.astype(BF16)
    wr_lo = (wr - wr_hi.astype(F32)).astype(BF16)
    br = jnp.pad(b_router, (0, LANES - N_EXPERTS))[None, :]

    qm, qtm, ktm, vm, kmean, cq, ckv, kr = _in_proj(x2, norm_attn_g[None, :], w_all, cos_t, sin_t, feat, seq, tm=tile)
    qc, ktl, vl = _mla_up(cq, ckv, q_a_norm_g[None, :], kv_a_norm_g[None, :], wqa, wqb, wk, wv, kr,
                          cos_t, sin_t, seq, tm=tile)
    n_blk = seq // MOBA_BLOCK
    kmr = kmean.reshape(batch, n_blk, MOBA_HEADS, MOBA_HEAD_DIM).transpose(0, 2, 1, 3)
    kmr = jnp.pad(kmr, ((0, 0), (0, 0), (0, LANES - n_blk), (0, 0))).reshape(batch * MOBA_HEADS, LANES, MOBA_HEAD_DIM)
    o_moba, w2b = _moba_attn(qm, qtm, ktm, vm, kmr, aq, w2, batch, seq, tile)
    o_mla, w1b = _mla_attn(qc, ktl, vl, w1, batch, seq, tile)
    x1, hp, logits = _out_proj(o_moba, o_mla, moba_out_g[None, :], mla_out_g[None, :], w_o.astype(BF16), x2,
                               norm_ffn_g[None, :], wr_hi, wr_lo, br, tm=256)

    idx, gates, pos, counts = _router(logits, tm=min(512, t))
    blk = EXPERT_BLOCK
    n_blocks = -(-(t * TOP_K) // blk) + N_EXPERTS
    counts = counts[0, :N_EXPERTS].astype(I32)
    padded = ((counts + blk - 1) // blk) * blk
    pad_end = jnp.cumsum(padded)
    pad_start = pad_end - padded
    dest = (pad_start[idx] + pos).reshape(-1)
    block_expert = jnp.minimum(
        jnp.sum((jnp.arange(n_blocks, dtype=I32) * blk)[:, None] >= pad_end[None, :], axis=1), N_EXPERTS - 1).astype(I32)
    n_used = (pad_end[-1:] // blk).astype(I32)
    src = jnp.zeros((n_blocks * blk,), I32).at[dest].set(jnp.arange(t * TOP_K, dtype=I32) // TOP_K,
                                                          unique_indices=True)

    act = _moe_up(block_expert, n_used, src, hp, w1b, b1[:, None, :])
    yb = _moe_down(block_expert, n_used, act, w2b, b2[:, None, :])
    return _combine(dest, gates, x1, final_g[None, :], yb, tm=256)


def kernel(x, norm_attn_g, w_in, moba_out_g, q_a_norm_g, kv_a_norm_g, w_uq, w_ukv, mla_out_g, w_o, norm_ffn_g,
           w_router, b_router, w1, b1, w2, b2, norm_final_g):
    batch, seq, d = x.shape
    depth = w_in.shape[0]
    assert depth == 1, "the final norm is fused into the last layer's combine step"
    x2 = x.reshape(batch * seq, d)
    out = _layer(x2, batch, seq, norm_attn_g[0], w_in[0], moba_out_g[0], q_a_norm_g[0], kv_a_norm_g[0], w_uq[0],
                 w_ukv[0], mla_out_g[0], w_o[0], norm_ffn_g[0], w_router[0], b_router[0], w1[0], b1[0], w2[0],
                 b2[0], norm_final_g)
    return out.reshape(batch, seq, d)
```

```python
import functools

import numpy as np
import jax
import jax.numpy as jnp
from jax import lax
from jax.experimental import pallas as pl
from jax.experimental.pallas import tpu as pltpu

MOBA_HEADS = 8
MOBA_HEAD_DIM = 128
MOBA_BLOCK = 256
MOBA_TOPK = 3
MLA_HEADS = 8
MLA_NOPE_DIM = 128
MLA_ROPE_DIM = 64
MLA_V_DIM = 128
MLA_Q_RANK = 512
MLA_KV_RANK = 512
ROPE_THETA = 10000.0
N_EXPERTS = 32
TOP_K = 4
SWIGLU_LIMIT = 7.0
SWIGLU_ALPHA = 1.702
EXPERT_BLOCK = 256
NORM_EPS = 1e-5

LANES = 128
SUBLANES = 8
QK_WIDTH = 256
ATT_TILE = 512
FLASH_STRIPS = 2
LOG2E = 1.4426950408889634
MASKED = -2.0 ** 100
M_INIT = -2.0 ** 98
FEAT_BLOCKS = 64
FEAT_ALIBI = FEAT_BLOCKS
V7X_VMEM_BUDGET = 56 * 1024 * 1024

F32 = jnp.float32
BF16 = jnp.bfloat16
U32 = jnp.uint32
I32 = jnp.int32


def _params(semantics, vmem_bytes):
    return pltpu.CompilerParams(dimension_semantics=semantics,
                                vmem_limit_bytes=min(int(vmem_bytes), V7X_VMEM_BUDGET))


def _rms(xf, g):
    ms = jnp.mean(xf * xf, axis=-1, keepdims=True)
    return xf * lax.rsqrt(ms + NORM_EPS) * g


def _dot(a, b):
    return jnp.dot(a, b, preferred_element_type=F32)


def _pack_halves(a, b):
    ai = lax.bitcast_convert_type(a.astype(BF16).astype(F32), U32)
    bi = lax.bitcast_convert_type(b.astype(BF16).astype(F32), U32)
    return ai | (bi >> 16)


def _unpack_halves(w):
    hi = lax.bitcast_convert_type(w & jnp.uint32(0xFFFF0000), F32)
    lo = lax.bitcast_convert_type(w << 16, F32)
    return hi, lo


def _store_row_tiles(ref, val):
    rows = val.shape[0]
    for s in range(SUBLANES):
        ref[pl.ds(s, rows, stride=SUBLANES), :] = val[:, s * LANES:(s + 1) * LANES]


def _load_row_tiles(ref):
    rows = ref.shape[0] // SUBLANES
    return jnp.concatenate([ref[pl.ds(s, rows, stride=SUBLANES), :] for s in range(SUBLANES)], axis=1)


def _row_tile(ref, r):
    return ref.at[pl.ds(pl.multiple_of(r * SUBLANES, SUBLANES), SUBLANES)]


def _inproj_kernel(x_ref, g_ref, w_ref, cos_ref, sin_ref, feat_ref,
                   qm_ref, qt_ref, kt_ref, vm_ref, kmean_ref, cq_ref, ckv_ref, kr_ref, *, q_scale):
    h = _rms(x_ref[...], g_ref[...]).astype(BF16)
    w = MOBA_HEADS * MOBA_HEAD_DIM
    dh = MOBA_HEAD_DIM
    tm = h.shape[0]

    def mm(lo, hi):
        return _dot(h, w_ref[:, lo:hi])

    q = mm(0, w) * q_scale
    qm_ref[...] = q.astype(BF16)
    k = mm(w, 2 * w)
    for b in range(tm // MOBA_BLOCK):
        kmean_ref[b] = jnp.mean(k[b * MOBA_BLOCK:(b + 1) * MOBA_BLOCK, :], axis=0, keepdims=True)
    v = mm(2 * w, 3 * w).astype(BF16)
    ones = jnp.ones((tm, QK_WIDTH - dh), BF16)
    for hh in range(MOBA_HEADS):
        qt_ref[hh, 0] = q[:, hh * dh:(hh + 1) * dh].T.astype(BF16)
        kt_ref[hh, 0, :dh, :] = k[:, hh * dh:(hh + 1) * dh].T.astype(BF16)
        kt_ref[hh, 0, dh:, :] = feat_ref[0]
        vm_ref[:, hh * QK_WIDTH:hh * QK_WIDTH + dh] = v[:, hh * dh:(hh + 1) * dh]
        vm_ref[:, hh * QK_WIDTH + dh:(hh + 1) * QK_WIDTH] = ones
    c0 = 3 * w
    cq_ref[...] = mm(c0, c0 + MLA_Q_RANK)
    c1 = c0 + MLA_Q_RANK
    ckv_ref[...] = mm(c1, c1 + MLA_KV_RANK)
    c2 = c1 + MLA_KV_RANK
    kr = mm(c2, c2 + QK_WIDTH) * cos_ref[...] + mm(c2 + QK_WIDTH, c2 + 2 * QK_WIDTH) * sin_ref[...]
    kr_ref[...] = kr.astype(BF16)


def _in_proj(x2, g, w_all, cos_t, sin_t, feat, seq, tm):
    t, d = x2.shape
    w = MOBA_HEADS * MOBA_HEAD_DIM
    hw = MOBA_HEADS * QK_WIDTH
    nc = w_all.shape[1]
    n_pos = seq // tm
    nb = tm // MOBA_BLOCK
    row = lambda i: (i, 0)
    const = lambda i: (0, 0)
    pos = lambda i: (i % n_pos, 0)
    out_shape = [jax.ShapeDtypeStruct((t, w), BF16),
                 jax.ShapeDtypeStruct((MOBA_HEADS, t // tm, MOBA_HEAD_DIM, tm), BF16),
                 jax.ShapeDtypeStruct((MOBA_HEADS, t // tm, QK_WIDTH, tm), BF16),
                 jax.ShapeDtypeStruct((t, hw), BF16),
                 jax.ShapeDtypeStruct((t // MOBA_BLOCK, 1, w), F32),
                 jax.ShapeDtypeStruct((t, MLA_Q_RANK), F32), jax.ShapeDtypeStruct((t, MLA_KV_RANK), F32),
                 jax.ShapeDtypeStruct((t, QK_WIDTH), BF16)]
    out_specs = [pl.BlockSpec((tm, w), row),
                 pl.BlockSpec((MOBA_HEADS, 1, MOBA_HEAD_DIM, tm), lambda i: (0, i, 0, 0)),
                 pl.BlockSpec((MOBA_HEADS, 1, QK_WIDTH, tm), lambda i: (0, i, 0, 0)),
                 pl.BlockSpec((tm, hw), row),
                 pl.BlockSpec((nb, 1, w), lambda i: (i, 0, 0)),
                 pl.BlockSpec((tm, MLA_Q_RANK), row), pl.BlockSpec((tm, MLA_KV_RANK), row),
                 pl.BlockSpec((tm, QK_WIDTH), row)]
    vmem = (2 * tm * d * 4 + d * nc * 2 + 2 * tm * (w * 2 + 2 * hw * 2 + 2 * 512 * 4 + 256 * 2)
            + 4 * tm * QK_WIDTH * 4 + tm * d * 2 + 3 * tm * w * 4 + (4 << 20))
    return pl.pallas_call(
        functools.partial(_inproj_kernel, q_scale=MOBA_HEAD_DIM ** -0.5 * LOG2E),
        grid=(t // tm,),
        in_specs=[pl.BlockSpec((tm, d), row), pl.BlockSpec((1, d), const),
                  pl.BlockSpec((d, nc), const, pipeline_mode=pl.Buffered(1)),
                  pl.BlockSpec((tm, QK_WIDTH), pos), pl.BlockSpec((tm, QK_WIDTH), pos),
                  pl.BlockSpec((1, QK_WIDTH - MOBA_HEAD_DIM, tm), lambda i: (i % n_pos, 0, 0))],
        out_specs=out_specs, out_shape=out_shape,
        compiler_params=_params(("parallel",), vmem), name="in_proj",
    )(x2, g, w_all, cos_t, sin_t, feat)


def _mlaup_kernel(cq_ref, ckv_ref, gq_ref, gkv_ref, wqa_ref, wqb_ref, wk_ref, wv_ref,
                  kr_ref, cos_ref, sin_ref, q_ref, kt_ref, v_ref, *, q_scale):
    nq = _rms(cq_ref[...], gq_ref[...]).astype(BF16)
    nkv = _rms(ckv_ref[...], gkv_ref[...]).astype(BF16)
    cos = cos_ref[...]
    sin = sin_ref[...]
    kr = kr_ref[...].astype(F32)
    ones = jnp.ones((nq.shape[0], QK_WIDTH - MLA_V_DIM), BF16)
    for hh in range(MLA_HEADS):
        cols = slice(hh * QK_WIDTH, (hh + 1) * QK_WIDTH)
        q = _dot(nq, wqa_ref[:, cols]) * cos + _dot(nq, wqb_ref[:, cols]) * sin
        q_ref[:, cols] = (q * q_scale).astype(BF16)
        kt_ref[hh, 0] = (_dot(nkv, wk_ref[:, cols]) + kr).T.astype(BF16)
        v_ref[:, hh * QK_WIDTH:hh * QK_WIDTH + MLA_V_DIM] = _dot(
            nkv, wv_ref[:, hh * MLA_V_DIM:(hh + 1) * MLA_V_DIM]).astype(BF16)
        v_ref[:, hh * QK_WIDTH + MLA_V_DIM:(hh + 1) * QK_WIDTH] = ones


def _mla_up(cq, ckv, gq, gkv, wqa, wqb, wk, wv, kr, cos_t, sin_t, seq, tm):
    t = cq.shape[0]
    n_pos = seq // tm
    row = lambda i: (i, 0)
    const = lambda i: (0, 0)
    pos = lambda i: (i % n_pos, 0)
    hq = MLA_HEADS * QK_WIDTH
    hv = MLA_HEADS * MLA_V_DIM
    scale = (MLA_NOPE_DIM + MLA_ROPE_DIM) ** -0.5 * LOG2E
    vmem = (4 * tm * 512 * 4 + 2 * (3 * 512 * hq * 2 + 512 * hv * 2) + 8 * tm * QK_WIDTH * 4
            + 2 * tm * 3 * hq * 2 + (8 << 20))
    return pl.pallas_call(
        functools.partial(_mlaup_kernel, q_scale=scale),
        grid=(t // tm,),
        in_specs=[pl.BlockSpec((tm, MLA_Q_RANK), row), pl.BlockSpec((tm, MLA_KV_RANK), row),
                  pl.BlockSpec((1, MLA_Q_RANK), const), pl.BlockSpec((1, MLA_KV_RANK), const),
                  pl.BlockSpec((MLA_Q_RANK, hq), const), pl.BlockSpec((MLA_Q_RANK, hq), const),
                  pl.BlockSpec((MLA_KV_RANK, hq), const), pl.BlockSpec((MLA_KV_RANK, hv), const),
                  pl.BlockSpec((tm, QK_WIDTH), row),
                  pl.BlockSpec((tm, QK_WIDTH), pos), pl.BlockSpec((tm, QK_WIDTH), pos)],
        out_specs=[pl.BlockSpec((tm, hq), row),
                   pl.BlockSpec((MLA_HEADS, 1, QK_WIDTH, tm), lambda i: (0, i, 0, 0)),
                   pl.BlockSpec((tm, hq), row)],
        out_shape=[jax.ShapeDtypeStruct((t, hq), BF16),
                   jax.ShapeDtypeStruct((MLA_HEADS, t // tm, QK_WIDTH, tm), BF16),
                   jax.ShapeDtypeStruct((t, hq), BF16)],
        compiler_params=_params(("parallel",), vmem), name="mla_up",
    )(cq, ckv, gq, gkv, wqa, wqb, wk, wv, kr, cos_t, sin_t)


def _flash_stages(refs, item_tables, *, pv=None, softmax=None, scores=None, tile, dv):
    q_ref, kt_ref, v_ref, o_ref, bias_sc, l_all, m_all, s_sc, p_sc, alpha_sc = refs
    ti_ref, tj_ref = item_tables
    rows = tile // FLASH_STRIPS

    def q_rows(t, r):
        return pl.ds(pl.multiple_of(ti_ref[t] * tile + r * rows, rows), rows)

    for r in range(FLASH_STRIPS):
        sl = slice(r * rows, (r + 1) * rows)
        if pv is not None:
            qr = q_rows(pv, r)
            res = _dot(p_sc[sl], v_ref[pl.ds(pl.multiple_of(tj_ref[pv] * tile, tile), tile), :])
            a = alpha_sc[sl]
            o_ref[qr, :] = a * o_ref[qr, :] + res[:, :dv]
            l_all[qr, :] = a * l_all[qr, :] + res[:, dv:2 * dv]
        if softmax is not None:
            qr = q_rows(softmax, r)
            diag = (ti_ref[softmax] == tj_ref[softmax]).astype(I32)
            s = s_sc[sl] + bias_sc[diag, sl]
            m_old = m_all[qr, :]
            m_new = jnp.maximum(m_old, jnp.max(s, axis=-1, keepdims=True))
            p_sc[sl] = jnp.exp2(s - m_new).astype(BF16)
            alpha_sc[sl] = jnp.exp2(m_old - m_new)
            m_all[qr, :] = m_new
        if scores is not None:
            s_sc[sl] = _dot(q_ref[q_rows(scores, r), :], kt_ref[0, tj_ref[scores]])


def _flash_sweep(refs, tables, n_items, *, tile, dv):
    ti_ref, tj_ref, st_ref = tables
    q_ref, kt_ref, v_ref, o_ref, bias_sc, l_all, m_all, s_sc, p_sc, alpha_sc = refs
    g = pl.program_id(2)
    last = pl.num_programs(2) - 1
    stages = functools.partial(_flash_stages, refs, (ti_ref, tj_ref), tile=tile, dv=dv)

    @pl.when(g == 0)
    def _():
        o_ref[...] = jnp.zeros_like(o_ref)
        l_all[...] = jnp.zeros_like(l_all)
        m_all[...] = jnp.full_like(m_all, M_INIT)
        stages(scores=0)
        stages(softmax=0, scores=1)

    lo = jnp.maximum(st_ref[g], 2)
    n_here = jnp.maximum(st_ref[g + 1] - lo, 0)

    def steady(k, carry):
        t = lo + 2 * k
        stages(pv=t - 2, softmax=t - 1, scores=t)
        stages(pv=t - 1, softmax=t, scores=t + 1)
        return carry

    lax.fori_loop(0, n_here // 2, steady, 0)

    @pl.when(n_here % 2 == 1)
    def _():
        t = lo + n_here - 1
        stages(pv=t - 2, softmax=t - 1, scores=t)

    @pl.when(g == last)
    def _():
        stages(pv=n_items - 2, softmax=n_items - 1)
        stages(pv=n_items - 1)

        def finish(u, carry):
            rows = pl.ds(pl.multiple_of(u * tile, tile), tile)
            o_ref[rows, :] = o_ref[rows, :] / l_all[rows, :]
            return carry

        lax.fori_loop(0, o_ref.shape[0] // tile, finish, 0)


def _attn_items(nq):
    ti = np.array([i for j in range(nq) for i in range(j, nq)], np.int32)
    tj = np.array([j for j in range(nq) for i in range(j, nq)], np.int32)
    st = np.array([(g * len(ti)) // nq for g in range(nq + 1)], np.int32)
    return jnp.asarray(ti), jnp.asarray(tj), jnp.asarray(st), len(ti)


def _mla_attn_kernel(ti_ref, tj_ref, st_ref, q_ref, kt_ref, v_ref, w_ref, o_ref, wb_ref,
                     bias_sc, *bufs, tile, n_items):
    wb_ref[...] = w_ref[...].astype(BF16)

    @pl.when(pl.program_id(2) == 0)
    def _():
        row = lax.broadcasted_iota(I32, (tile, tile), 0)
        col = lax.broadcasted_iota(I32, (tile, tile), 1)
        bias_sc[0] = jnp.zeros((tile, tile), F32)
        bias_sc[1] = jnp.where(col > row, MASKED, 0.0)

    _flash_sweep((q_ref, kt_ref, v_ref, o_ref, bias_sc) + bufs, (ti_ref, tj_ref, st_ref), n_items,
                 tile=tile, dv=MLA_V_DIM)


def _attn_call(kernel, name, q_width, extra_specs, scratch, args, w_f32, batch, heads, seq, tile, dv):
    t = batch * seq
    nq = seq // tile
    assert nq >= 2
    ti, tj, st, n_items = _attn_items(nq)
    steps = batch * heads * nq
    w2d = w_f32.reshape(-1, w_f32.shape[-1])
    assert w2d.shape[0] % steps == 0
    w_rows = w2d.shape[0] // steps
    w_spec = pl.BlockSpec((w_rows, w2d.shape[1]), lambda b, h, g, *_: ((b * heads + h) * nq + g, 0))
    once = pl.Buffered(1)
    grid_spec = pltpu.PrefetchScalarGridSpec(
        num_scalar_prefetch=3, grid=(batch, heads, nq),
        in_specs=[pl.BlockSpec((seq, q_width), lambda b, h, g, *_: (b, h), pipeline_mode=once),
                  pl.BlockSpec((1, nq, QK_WIDTH, tile), lambda b, h, g, *_: (h, b, 0, 0), pipeline_mode=once),
                  pl.BlockSpec((seq, QK_WIDTH), lambda b, h, g, *_: (b, h), pipeline_mode=once)]
                 + extra_specs + [w_spec],
        out_specs=[pl.BlockSpec((seq, dv), lambda b, h, g, *_: (b, h)), w_spec],
        scratch_shapes=scratch + [pltpu.VMEM((2, tile, tile), F32), pltpu.VMEM((seq, dv), F32),
                                  pltpu.VMEM((seq, 1), F32), pltpu.VMEM((tile, tile), F32),
                                  pltpu.VMEM((tile, tile), BF16), pltpu.VMEM((tile, 1), F32)])
    vmem = (seq * (q_width + 2 * QK_WIDTH) * 2 + 2 * seq * dv * 4 + seq * dv * 4 + seq * LANES * 4
            + 8 * tile * tile * 4 + 3 * w_rows * w2d.shape[1] * 6 + seq * QK_WIDTH * 2 + (4 << 20))
    out, wb = pl.pallas_call(
        functools.partial(kernel, n_items=n_items),
        grid_spec=grid_spec,
        out_shape=[jax.ShapeDtypeStruct((t, heads * dv), F32), jax.ShapeDtypeStruct(w2d.shape, BF16)],
        compiler_params=_params(("parallel", "parallel", "arbitrary"), vmem), name=name,
    )(ti, tj, st, *args, w2d)
    return out, wb.reshape(w_f32.shape)


def _mla_attn(qc, kt, v, w_f32, batch, seq, tile):
    return _attn_call(functools.partial(_mla_attn_kernel, tile=tile), "mla_attn", QK_WIDTH, [], [],
                      (qc, kt, v), w_f32, batch, MLA_HEADS, seq, tile, MLA_V_DIM)


def _moba_attn_kernel(ti_ref, tj_ref, st_ref, q_ref, kt_ref, v_ref, qt_ref, km_ref, aq_ref, w_ref, o_ref, wb_ref,
                      qa_sc, bias_sc, *bufs, tile, n_items):
    wb_ref[...] = w_ref[...].astype(BF16)
    blk = MOBA_BLOCK

    @pl.when(pl.program_id(2) == 0)
    def _():
        row = lax.broadcasted_iota(I32, (tile, tile), 0)
        col = lax.broadcasted_iota(I32, (tile, tile), 1)
        bias_sc[0] = jnp.zeros((tile, tile), F32)
        bias_sc[1] = jnp.where((row // blk == col // blk) & (col > row), MASKED, 0.0)
        km = km_ref[0]
        km_hi = km.astype(BF16)
        km_lo = (km - km_hi.astype(F32)).astype(BF16)
        blk_id = lax.broadcasted_iota(I32, (FEAT_BLOCKS, tile), 0)
        in_tile = lax.broadcasted_iota(I32, (FEAT_BLOCKS, tile), 1) // blk
        consts = jnp.broadcast_to(aq_ref[0, FEAT_BLOCKS:, :], (LANES - FEAT_BLOCKS, tile))

        def augment(u, carry):
            rows = pl.ds(pl.multiple_of(u * tile, tile), tile)
            qt = qt_ref[0, u]
            gate = (_dot(km_hi, qt) + _dot(km_lo, qt))[:FEAT_BLOCKS]
            own = u * (tile // blk) + in_tile
            g = jnp.where(blk_id < own, gate, -jnp.inf)
            keep = blk_id == own
            for _ in range(MOBA_TOPK):
                mx = jnp.max(g, axis=0, keepdims=True)
                cand = jnp.where((g == mx) & (mx > -jnp.inf), blk_id, LANES)
                pick = blk_id == jnp.min(cand, axis=0, keepdims=True)
                keep = keep | pick
                g = jnp.where(pick, -jnp.inf, g)
            feat_t = jnp.concatenate([jnp.where(keep, 0.0, 1.0), consts], axis=0)
            qa_sc[rows, :MOBA_HEAD_DIM] = q_ref[rows, :]
            qa_sc[rows, MOBA_HEAD_DIM:] = feat_t.T.astype(BF16)
            return carry

        lax.fori_loop(0, q_ref.shape[0] // tile, augment, 0)

    _flash_sweep((qa_sc, kt_ref, v_ref, o_ref, bias_sc) + bufs, (ti_ref, tj_ref, st_ref), n_items,
                 tile=tile, dv=MOBA_HEAD_DIM)


def _moba_attn(qm, qt, kt, vm, kmr, aq, w_f32, batch, seq, tile):
    dh = MOBA_HEAD_DIM
    nq = seq // tile
    return _attn_call(functools.partial(_moba_attn_kernel, tile=tile), "moba_attn", dh,
                      [pl.BlockSpec((1, nq, dh, tile), lambda b, h, g, *_: (h, b, 0, 0),
                                    pipeline_mode=pl.Buffered(1)),
                       pl.BlockSpec((1, LANES, dh), lambda b, h, g, *_: (b * MOBA_HEADS + h, 0, 0)),
                       pl.BlockSpec((1, LANES, 1), lambda b, h, g, *_: (h, 0, 0))],
                      [pltpu.VMEM((seq, QK_WIDTH), BF16)],
                      (qm, kt, vm, qt, kmr, aq), w_f32, batch, MOBA_HEADS, seq, tile, dh)


def _outproj_kernel(om_ref, ol_ref, gm_ref, gl_ref, wo_ref, x_ref, gf_ref, wrh_ref, wrl_ref, br_ref,
                    x1_ref, hp_ref, lg_ref):
    wm = om_ref.shape[1]
    a = _rms(om_ref[...], gm_ref[...]).astype(BF16)
    b = _rms(ol_ref[...], gl_ref[...]).astype(BF16)
    x1 = x_ref[...] + _dot(a, wo_ref[:wm, :]) + _dot(b, wo_ref[wm:, :])
    x1_ref[...] = x1
    h2 = _rms(x1, gf_ref[...])
    half = h2.shape[1] // 2
    _store_row_tiles(hp_ref, _pack_halves(h2[:, :half], h2[:, half:]))
    h_hi = h2.astype(BF16)
    h_lo = (h2 - h_hi.astype(F32)).astype(BF16)
    lg_ref[...] = (_dot(h_hi, wrh_ref[...]) + _dot(h_lo, wrh_ref[...]) + _dot(h_hi, wrl_ref[...])
                   + br_ref[...])


def _out_proj(om, ol, gm, gl, wo, x2, gf, wr_hi, wr_lo, br, tm):
    t, d = x2.shape
    wm = om.shape[1]
    wl = ol.shape[1]
    row = lambda i: (i, 0)
    const = lambda i: (0, 0)
    vmem = 2 * (wm + wl) * d * 2 + 2 * tm * (wm + wl + 2 * d) * 4 + 2 * tm * d * 2 + 6 * tm * d * 4 + (4 << 20)
    return pl.pallas_call(
        _outproj_kernel,
        grid=(t // tm,),
        in_specs=[pl.BlockSpec((tm, wm), row), pl.BlockSpec((tm, wl), row),
                  pl.BlockSpec((1, wm), const), pl.BlockSpec((1, wl), const),
                  pl.BlockSpec((wm + wl, d), const), pl.BlockSpec((tm, d), row), pl.BlockSpec((1, d), const),
                  pl.BlockSpec((d, LANES), const), pl.BlockSpec((d, LANES), const), pl.BlockSpec((1, LANES), const)],
        out_specs=[pl.BlockSpec((tm, d), row), pl.BlockSpec((tm * SUBLANES, LANES), row),
                   pl.BlockSpec((tm, LANES), row)],
        out_shape=[jax.ShapeDtypeStruct((t, d), F32), jax.ShapeDtypeStruct((t * SUBLANES, LANES), U32),
                   jax.ShapeDtypeStruct((t, LANES), F32)],
        compiler_params=_params(("parallel",), vmem), name="out_proj",
    )(om, ol, gm, gl, wo, x2, gf, wr_hi, wr_lo, br)


def _router_kernel(lg_ref, idx_ref, gate_ref, pos_ref, cnt_ref, *, tm):
    i = pl.program_id(0)

    @pl.when(i == 0)
    def _():
        cnt_ref[...] = jnp.zeros_like(cnt_ref)

    lane = lax.broadcasted_iota(I32, (tm, LANES), 1)
    work = jnp.where(lane < N_EXPERTS, lg_ref[...], -jnp.inf)
    vals, firsts, picks = [], [], []
    for _ in range(TOP_K):
        mx = jnp.max(work, axis=-1, keepdims=True)
        first = jnp.min(jnp.where(work == mx, lane, LANES), axis=-1, keepdims=True)
        pick = lane == first
        work = jnp.where(pick, -jnp.inf, work)
        vals.append(mx)
        firsts.append(first)
        picks.append(pick)
    es = [jnp.exp(v - vals[0]) for v in vals]
    denom = es[0] + es[1] + es[2] + es[3]
    chosen = jnp.zeros((tm, LANES), F32)
    for p in picks:
        chosen = jnp.where(p, 1.0, chosen)
    r = lax.broadcasted_iota(I32, (tm, tm), 0)
    cc = lax.broadcasted_iota(I32, (tm, tm), 1)
    before = (r > cc).astype(BF16)
    prior = _dot(before, chosen.astype(BF16)) + cnt_ref[...]
    lane4 = lax.broadcasted_iota(I32, (tm, TOP_K), 1)
    idx = jnp.zeros((tm, TOP_K), I32)
    gate = jnp.zeros((tm, TOP_K), F32)
    pos = jnp.zeros((tm, TOP_K), I32)
    for j in range(TOP_K):
        pj = jnp.sum(jnp.where(picks[j], prior, 0.0), axis=-1, keepdims=True)
        idx = jnp.where(lane4 == j, firsts[j], idx)
        gate = jnp.where(lane4 == j, es[j] / denom, gate)
        pos = jnp.where(lane4 == j, pj.astype(I32), pos)
    idx_ref[...] = idx
    gate_ref[...] = gate
    pos_ref[...] = pos
    cnt_ref[...] += jnp.sum(chosen, axis=0, keepdims=True)


def _router(logits, tm):
    t = logits.shape[0]
    row = lambda i: (i, 0)
    vmem = 4 * tm * tm * 4 + 24 * tm * LANES * 4 + (4 << 20)
    return pl.pallas_call(
        functools.partial(_router_kernel, tm=tm),
        grid=(t // tm,),
        in_specs=[pl.BlockSpec((tm, LANES), row)],
        out_specs=[pl.BlockSpec((tm, TOP_K), row)] * 3 + [pl.BlockSpec((1, LANES), lambda i: (0, 0))],
        out_shape=[jax.ShapeDtypeStruct((t, TOP_K), I32), jax.ShapeDtypeStruct((t, TOP_K), F32),
                   jax.ShapeDtypeStruct((t, TOP_K), I32), jax.ShapeDtypeStruct((1, LANES), F32)],
        compiler_params=_params(("arbitrary",), vmem), name="router",
    )(logits)


def _moe_up_kernel(be_ref, nu_ref, src_ref, hp_ref, w1_ref, b1_ref, act_ref, xs_sc, sem, *, fc):
    i = pl.program_id(0)
    blk = act_ref.shape[0]
    slot = i % 2

    def row_copy(block, r, s):
        return pltpu.make_async_copy(_row_tile(hp_ref, src_ref[block * blk + r]),
                                     _row_tile(xs_sc.at[s], r), sem.at[s])

    def start_block(block, s):
        def body(r, carry):
            row_copy(block, r, s).start()
            return carry
        lax.fori_loop(0, blk, body, 0, unroll=8)

    def wait_block(block, s):
        def body(r, carry):
            row_copy(block, r, s).wait()
            return carry
        lax.fori_loop(0, blk, body, 0, unroll=8)

    @pl.when(i == 0)
    def _():
        start_block(0, 0)

    @pl.when(i + 1 < nu_ref[0])
    def _():
        start_block(i + 1, 1 - slot)

    @pl.when(i < nu_ref[0])
    def _():
        wait_block(i, slot)
        hi, lo = _unpack_halves(_load_row_tiles(xs_sc.at[slot]))
        a = hi.astype(BF16)
        b = lo.astype(BF16)
        half = a.shape[1]
        d_ff = act_ref.shape[1]

        def pre(c0):
            return (_dot(a, w1_ref[0, :half, c0:c0 + fc]) + _dot(b, w1_ref[0, half:, c0:c0 + fc])
                    + b1_ref[0, :, c0:c0 + fc])

        for c in range(d_ff // fc):
            glu = jnp.minimum(pre(c * fc), SWIGLU_LIMIT)
            lin = jnp.clip(pre(d_ff + c * fc), -SWIGLU_LIMIT, SWIGLU_LIMIT)
            act = glu * jax.nn.sigmoid(SWIGLU_ALPHA * glu) * (lin + 1.0)
            act_ref[:, c * fc:(c + 1) * fc] = act.astype(BF16)

    @pl.when(i >= nu_ref[0])
    def _():
        act_ref[...] = jnp.zeros_like(act_ref)


def _moe_up(block_expert, n_used, src, hp, w1, b1):
    n_rows = src.shape[0]
    half = SUBLANES * hp.shape[1]
    e, d, f2 = w1.shape
    assert d == 2 * half
    d_ff = f2 // 2
    blk = EXPERT_BLOCK
    n_blocks = n_rows // blk
    fc = 512
    live = lambda i, be, nu: jnp.minimum(i, nu[0] - 1)
    grid_spec = pltpu.PrefetchScalarGridSpec(
        num_scalar_prefetch=3, grid=(n_blocks,),
        in_specs=[pl.BlockSpec(memory_space=pl.ANY),
                  pl.BlockSpec((1, d, f2), lambda i, be, nu, src: (be[live(i, be, nu)], 0, 0)),
                  pl.BlockSpec((1, 1, f2), lambda i, be, nu, src: (be[live(i, be, nu)], 0, 0))],
        out_specs=pl.BlockSpec((blk, d_ff), lambda i, be, nu, src: (i, 0)),
        scratch_shapes=[pltpu.VMEM((2, blk * SUBLANES, LANES), U32), pltpu.SemaphoreType.DMA((2,))])
    vmem = 2 * d * f2 * 2 + 2 * blk * half * 4 + 2 * blk * d_ff * 2 + 8 * blk * fc * 4 + 2 * blk * d * 2 + (4 << 20)
    return pl.pallas_call(
        functools.partial(_moe_up_kernel, fc=fc),
        grid_spec=grid_spec,
        out_shape=jax.ShapeDtypeStruct((n_rows, d_ff), BF16),
        compiler_params=_params(("arbitrary",), vmem), name="moe_up",
    )(block_expert, n_used, src, hp, w1, b1)


def _moe_down_kernel(be_ref, nu_ref, a_ref, w2_ref, b2_ref, y_ref):
    i = pl.program_id(0)

    @pl.when(i < nu_ref[0])
    def _():
        y = _dot(a_ref[...], w2_ref[0]) + b2_ref[0]
        half = y.shape[1] // 2
        _store_row_tiles(y_ref, _pack_halves(y[:, :half], y[:, half:]))

    @pl.when(i >= nu_ref[0])
    def _():
        y_ref[...] = jnp.zeros_like(y_ref)


def _moe_down(block_expert, n_used, act, w2, b2):
    n_rows, d_ff = act.shape
    e, _, d = w2.shape
    blk = EXPERT_BLOCK
    n_blocks = n_rows // blk
    live = lambda i, be, nu: jnp.minimum(i, nu[0] - 1)
    grid_spec = pltpu.PrefetchScalarGridSpec(
        num_scalar_prefetch=2, grid=(n_blocks,),
        in_specs=[pl.BlockSpec((blk, d_ff), lambda i, be, nu: (live(i, be, nu), 0)),
                  pl.BlockSpec((1, d_ff, d), lambda i, be, nu: (be[live(i, be, nu)], 0, 0)),
                  pl.BlockSpec((1, 1, d), lambda i, be, nu: (be[live(i, be, nu)], 0, 0))],
        out_specs=pl.BlockSpec((blk * SUBLANES, LANES), lambda i, be, nu: (i, 0)))
    vmem = 2 * d_ff * d * 2 + 2 * blk * d_ff * 2 + 2 * blk * d * 2 + 4 * blk * d * 4 + (4 << 20)
    return pl.pallas_call(
        _moe_down_kernel,
        grid_spec=grid_spec,
        out_shape=jax.ShapeDtypeStruct((n_rows * SUBLANES, LANES), U32),
        compiler_params=_params(("arbitrary",), vmem), name="moe_down",
    )(block_expert, n_used, act, w2, b2)


def _combine_kernel(dest_ref, gate_ref, x1_ref, gfin_ref, yb_ref, o_ref, rows_sc, sem, *, tm):
    i = pl.program_id(0)
    slot = i % 2

    def row_copy(tile, r, j, s):
        return pltpu.make_async_copy(_row_tile(yb_ref, dest_ref[(tile * tm + r) * TOP_K + j]),
                                     _row_tile(rows_sc.at[s, j], r), sem.at[s])

    def start_tile(tile, s):
        def body(r, carry):
            for j in range(TOP_K):
                row_copy(tile, r, j, s).start()
            return carry
        lax.fori_loop(0, tm, body, 0, unroll=4)

    def wait_tile(tile, s):
        def body(r, carry):
            for j in range(TOP_K):
                row_copy(tile, r, j, s).wait()
            return carry
        lax.fori_loop(0, tm, body, 0, unroll=4)

    @pl.when(i == 0)
    def _():
        start_tile(0, 0)

    @pl.when(i + 1 < pl.num_programs(0))
    def _():
        start_tile(i + 1, 1 - slot)

    wait_tile(i, slot)

    x1 = x1_ref[...]
    half = x1.shape[1] // 2
    gates = gate_ref[...]
    left = x1[:, :half]
    right = x1[:, half:]
    for j in range(TOP_K):
        hi, lo = _unpack_halves(_load_row_tiles(rows_sc.at[slot, j]))
        gj = gates[:, j:j + 1]
        left = left + gj * hi
        right = right + gj * lo
    ms = (jnp.sum(left * left, axis=-1, keepdims=True) + jnp.sum(right * right, axis=-1, keepdims=True)) / (2 * half)
    inv = lax.rsqrt(ms + NORM_EPS)
    g = gfin_ref[...]
    o_ref[:, :half] = left * inv * g[:, :half]
    o_ref[:, half:] = right * inv * g[:, half:]


def _combine(dest, gates, x1, gfin, yb, tm):
    t, d = x1.shape
    half = d // 2
    grid_spec = pltpu.PrefetchScalarGridSpec(
        num_scalar_prefetch=1, grid=(t // tm,),
        in_specs=[pl.BlockSpec((tm, TOP_K), lambda i, dst: (i, 0)),
                  pl.BlockSpec((tm, d), lambda i, dst: (i, 0)),
                  pl.BlockSpec((1, d), lambda i, dst: (0, 0)),
                  pl.BlockSpec(memory_space=pl.ANY)],
        out_specs=pl.BlockSpec((tm, d), lambda i, dst: (i, 0)),
        scratch_shapes=[pltpu.VMEM((2, TOP_K, tm * SUBLANES, LANES), U32), pltpu.SemaphoreType.DMA((2,))])
    vmem = 2 * TOP_K * tm * half * 4 + 4 * tm * d * 4 + 6 * tm * d * 4 + 2 * tm * LANES * 4 + (4 << 20)
    return pl.pallas_call(
        functools.partial(_combine_kernel, tm=tm),
        grid_spec=grid_spec,
        out_shape=jax.ShapeDtypeStruct((t, d), F32),
        compiler_params=_params(("arbitrary",), vmem), name="combine",
    )(dest, gates, x1, gfin, yb)


def _rope_tables(seq):
    inv = 1.0 / (ROPE_THETA ** (jnp.arange(0, MLA_ROPE_DIM, 2, dtype=F32) / MLA_ROPE_DIM))
    ang = jnp.arange(seq, dtype=F32)[:, None] * inv[None, :]
    ang = jnp.concatenate([ang, ang], axis=-1)
    ones = jnp.ones((seq, MLA_NOPE_DIM), F32)
    zeros = jnp.zeros((seq, QK_WIDTH - MLA_NOPE_DIM - MLA_ROPE_DIM), F32)
    cos_t = jnp.concatenate([ones, jnp.cos(ang), zeros], axis=-1)
    sin_t = jnp.concatenate([0.0 * ones, jnp.sin(ang), zeros], axis=-1)
    return cos_t, sin_t


def _bf16_pieces(x, n=3):
    out = []
    for _ in range(n):
        p = float(np.asarray(x, dtype=np.float32).astype(jnp.bfloat16).astype(np.float32))
        out.append(p)
        x = x - p
    return out


def _moba_features(seq, tile):
    pos = np.arange(seq)
    blk, off = pos // MOBA_BLOCK, pos % MOBA_BLOCK
    assert seq // MOBA_BLOCK <= FEAT_BLOCKS
    feat = np.zeros((QK_WIDTH - MOBA_HEAD_DIM, seq), np.float32)
    feat[:FEAT_BLOCKS] = np.where(blk[None, :] == np.arange(FEAT_BLOCKS)[:, None], MASKED, 0.0)
    feat[FEAT_ALIBI:FEAT_ALIBI + 3] = blk[None, :]
    feat[FEAT_ALIBI + 3:FEAT_ALIBI + 6] = off[None, :]
    feat = feat.reshape(feat.shape[0], seq // tile, tile).transpose(1, 0, 2)
    aq = np.zeros((MOBA_HEADS, 1, LANES), np.float32)
    for h in range(MOBA_HEADS):
        slope = 2.0 ** (-8.0 * (h + 1) / MOBA_HEADS)
        aq[h, 0, FEAT_ALIBI:FEAT_ALIBI + 3] = _bf16_pieces(slope * LOG2E * MOBA_BLOCK)
        aq[h, 0, FEAT_ALIBI + 3:FEAT_ALIBI + 6] = _bf16_pieces(slope * LOG2E)
    return jnp.asarray(feat, BF16), jnp.asarray(aq.transpose(0, 2, 1), F32)


def _rot_cols(w):
    half = w.shape[-1] // 2
    return jnp.concatenate([-w[..., half:], w[..., :half]], axis=-1)


def _rope_slot(w):
    k = w.shape[0]
    return jnp.concatenate([jnp.zeros((k, MLA_NOPE_DIM), w.dtype), w,
                            jnp.zeros((k, QK_WIDTH - MLA_NOPE_DIM - MLA_ROPE_DIM), w.dtype)], axis=-1)


def _layer(x2, batch, seq, norm_attn_g, w_in, moba_out_g, q_a_norm_g, kv_a_norm_g, w_uq, w_ukv, mla_out_g,
           w_o, norm_ffn_g, w_router, b_router, w1, b1, w2, b2, final_g):
    t, d = x2.shape
    wm = MOBA_HEADS * MOBA_HEAD_DIM
    tile = ATT_TILE
    assert seq % tile == 0 and tile % MOBA_BLOCK == 0
    cos_t, sin_t = _rope_tables(seq)
    feat, aq = _moba_features(seq, tile)

    c_kr = 3 * wm + MLA_Q_RANK + MLA_KV_RANK
    w_kr = w_in[:, c_kr:c_kr + MLA_ROPE_DIM]
    w_all = jnp.concatenate([w_in[:, :c_kr], _rope_slot(w_kr), _rope_slot(_rot_cols(w_kr))], axis=1).astype(BF16)
    wq = w_uq.reshape(MLA_Q_RANK, MLA_HEADS, MLA_NOPE_DIM + MLA_ROPE_DIM)
    zq = jnp.zeros((MLA_Q_RANK, MLA_HEADS, QK_WIDTH - MLA_NOPE_DIM - MLA_ROPE_DIM), F32)
    wqa = jnp.concatenate([wq, zq], axis=-1).reshape(MLA_Q_RANK, -1).astype(BF16)
    wqb = jnp.concatenate([jnp.zeros((MLA_Q_RANK, MLA_HEADS, MLA_NOPE_DIM), F32),
                           _rot_cols(wq[..., MLA_NOPE_DIM:]), zq], axis=-1).reshape(MLA_Q_RANK, -1).astype(BF16)
    wkv = w_ukv.reshape(MLA_KV_RANK, MLA_HEADS, MLA_NOPE_DIM + MLA_V_DIM)
    wk = jnp.concatenate([wkv[..., :MLA_NOPE_DIM],
                          jnp.zeros((MLA_KV_RANK, MLA_HEADS, QK_WIDTH - MLA_NOPE_DIM), F32)],
                         axis=-1).reshape(MLA_KV_RANK, -1).astype(BF16)
    wv = wkv[..., MLA_NOPE_DIM:].reshape(MLA_KV_RANK, -1).astype(BF16)
    wr = jnp.pad(w_router, ((0, 0), (0, LANES - N_EXPERTS)))
    wr_hi = wr.astype(BF16)
    wr_lo = (wr - wr_hi.astype(F32)).astype(BF16)
    br = jnp.pad(b_router, (0, LANES - N_EXPERTS))[None, :]

    qm, qtm, ktm, vm, kmean, cq, ckv, kr = _in_proj(x2, norm_attn_g[None, :], w_all, cos_t, sin_t, feat, seq, tm=tile)
    qc, ktl, vl = _mla_up(cq, ckv, q_a_norm_g[None, :], kv_a_norm_g[None, :], wqa, wqb, wk, wv, kr,
                          cos_t, sin_t, seq, tm=tile)
    n_blk = seq // MOBA_BLOCK
    kmr = kmean.reshape(batch, n_blk, MOBA_HEADS, MOBA_HEAD_DIM).transpose(0, 2, 1, 3)
    kmr = jnp.pad(kmr, ((0, 0), (0, 0), (0, LANES - n_blk), (0, 0))).reshape(batch * MOBA_HEADS, LANES, MOBA_HEAD_DIM)
    o_moba, w2b = _moba_attn(qm, qtm, ktm, vm, kmr, aq, w2, batch, seq, tile)
    o_mla, w1b = _mla_attn(qc, ktl, vl, w1, batch, seq, tile)
    x1, hp, logits = _out_proj(o_moba, o_mla, moba_out_g[None, :], mla_out_g[None, :], w_o.astype(BF16), x2,
                               norm_ffn_g[None, :], wr_hi, wr_lo, br, tm=256)

    idx, gates, pos, counts = _router(logits, tm=min(512, t))
    blk = EXPERT_BLOCK
    n_blocks = -(-(t * TOP_K) // blk) + N_EXPERTS
    counts = counts[0, :N_EXPERTS].astype(I32)
    padded = ((counts + blk - 1) // blk) * blk
    pad_end = jnp.cumsum(padded)
    pad_start = pad_end - padded
    dest = (pad_start[idx] + pos).reshape(-1)
    block_expert = jnp.minimum(
        jnp.sum((jnp.arange(n_blocks, dtype=I32) * blk)[:, None] >= pad_end[None, :], axis=1), N_EXPERTS - 1).astype(I32)
    n_used = (pad_end[-1:] // blk).astype(I32)
    src = jnp.zeros((n_blocks * blk,), I32).at[dest].set(jnp.arange(t * TOP_K, dtype=I32) // TOP_K,
                                                          unique_indices=True)

    act = _moe_up(block_expert, n_used, src, hp, w1b, b1[:, None, :])
    yb = _moe_down(block_expert, n_used, act, w2b, b2[:, None, :])
    return _combine(dest, gates, x1, final_g[None, :], yb, tm=256)


def kernel(x, norm_attn_g, w_in, moba_out_g, q_a_norm_g, kv_a_norm_g, w_uq, w_ukv, mla_out_g, w_o, norm_ffn_g,
           w_router, b_router, w1, b1, w2, b2, norm_final_g):
    batch, seq, d = x.shape
    depth = w_in.shape[0]
    assert depth == 1, "the final norm is fused into the last layer's combine step"
    x2 = x.reshape(batch * seq, d)
    out = _layer(x2, batch, seq, norm_attn_g[0], w_in[0], moba_out_g[0], q_a_norm_g[0], kv_a_norm_g[0], w_uq[0],
                 w_ukv[0], mla_out_g[0], w_o[0], norm_ffn_g[0], w_router[0], b_router[0], w1[0], b1[0], w2[0],
                 b2[0], norm_final_g)
    return out.reshape(batch, seq, d)
```

```python
import functools

import numpy as np
import jax
import jax.numpy as jnp
from jax import lax
from jax.experimental import pallas as pl
from jax.experimental.pallas import tpu as pltpu

MOBA_HEADS = 8
MOBA_HEAD_DIM = 128
MOBA_BLOCK = 256
MOBA_TOPK = 3
MLA_HEADS = 8
MLA_NOPE_DIM = 128
MLA_ROPE_DIM = 64
MLA_V_DIM = 128
MLA_Q_RANK = 512
MLA_KV_RANK = 512
ROPE_THETA = 10000.0
N_EXPERTS = 32
TOP_K = 4
SWIGLU_LIMIT = 7.0
SWIGLU_ALPHA = 1.702
EXPERT_BLOCK = 256
NORM_EPS = 1e-5

LANES = 128
SUBLANES = 8
QK_WIDTH = 256
ATT_TILE = 512
FLASH_STRIPS = 2
FLASH_UNROLL = 4
LOG2E = 1.4426950408889634
MASKED = -2.0 ** 100
M_INIT = -2.0 ** 98
FEAT_BLOCKS = 64
FEAT_ALIBI = FEAT_BLOCKS
V7X_VMEM_BUDGET = 56 * 1024 * 1024

F32 = jnp.float32
BF16 = jnp.bfloat16
U32 = jnp.uint32
I32 = jnp.int32


def _params(semantics, vmem_bytes):
    return pltpu.CompilerParams(dimension_semantics=semantics,
                                vmem_limit_bytes=min(int(vmem_bytes), V7X_VMEM_BUDGET))


def _rms(xf, g):
    ms = jnp.mean(xf * xf, axis=-1, keepdims=True)
    return xf * lax.rsqrt(ms + NORM_EPS) * g


def _dot(a, b):
    return jnp.dot(a, b, preferred_element_type=F32)


def _pack_halves(a, b):
    ai = lax.bitcast_convert_type(a.astype(BF16).astype(F32), U32)
    bi = lax.bitcast_convert_type(b.astype(BF16).astype(F32), U32)
    return ai | (bi >> 16)


def _unpack_halves(w):
    hi = lax.bitcast_convert_type(w & jnp.uint32(0xFFFF0000), F32)
    lo = lax.bitcast_convert_type(w << 16, F32)
    return hi, lo


def _store_row_tiles(ref, val):
    rows = val.shape[0]
    for s in range(SUBLANES):
        ref[pl.ds(s, rows, stride=SUBLANES), :] = val[:, s * LANES:(s + 1) * LANES]


def _load_row_tiles(ref):
    rows = ref.shape[0] // SUBLANES
    return jnp.concatenate([ref[pl.ds(s, rows, stride=SUBLANES), :] for s in range(SUBLANES)], axis=1)


def _row_tile(ref, r):
    return ref.at[pl.ds(pl.multiple_of(r * SUBLANES, SUBLANES), SUBLANES)]


def _inproj_kernel(x_ref, g_ref, w_ref, cos_ref, sin_ref, feat_ref,
                   qm_ref, qt_ref, kt_ref, vm_ref, kmean_ref, cq_ref, ckv_ref, kr_ref, *, q_scale):
    h = _rms(x_ref[...], g_ref[...]).astype(BF16)
    w = MOBA_HEADS * MOBA_HEAD_DIM
    dh = MOBA_HEAD_DIM
    tm = h.shape[0]

    def mm(lo, hi):
        return _dot(h, w_ref[:, lo:hi])

    q = mm(0, w) * q_scale
    qm_ref[...] = q.astype(BF16)
    k = mm(w, 2 * w)
    for b in range(tm // MOBA_BLOCK):
        kmean_ref[b] = jnp.mean(k[b * MOBA_BLOCK:(b + 1) * MOBA_BLOCK, :], axis=0, keepdims=True)
    v = mm(2 * w, 3 * w).astype(BF16)
    ones = jnp.ones((tm, QK_WIDTH - dh), BF16)
    for hh in range(MOBA_HEADS):
        qt_ref[hh, 0] = q[:, hh * dh:(hh + 1) * dh].T.astype(BF16)
        kt_ref[hh, 0, :dh, :] = k[:, hh * dh:(hh + 1) * dh].T.astype(BF16)
        kt_ref[hh, 0, dh:, :] = feat_ref[0]
        vm_ref[:, hh * QK_WIDTH:hh * QK_WIDTH + dh] = v[:, hh * dh:(hh + 1) * dh]
        vm_ref[:, hh * QK_WIDTH + dh:(hh + 1) * QK_WIDTH] = ones
    c0 = 3 * w
    cq_ref[...] = mm(c0, c0 + MLA_Q_RANK)
    c1 = c0 + MLA_Q_RANK
    ckv_ref[...] = mm(c1, c1 + MLA_KV_RANK)
    c2 = c1 + MLA_KV_RANK
    kr = mm(c2, c2 + QK_WIDTH) * cos_ref[...] + mm(c2 + QK_WIDTH, c2 + 2 * QK_WIDTH) * sin_ref[...]
    kr_ref[...] = kr.astype(BF16)


def _in_proj(x2, g, w_all, cos_t, sin_t, feat, seq, tm):
    t, d = x2.shape
    w = MOBA_HEADS * MOBA_HEAD_DIM
    hw = MOBA_HEADS * QK_WIDTH
    nc = w_all.shape[1]
    n_pos = seq // tm
    nb = tm // MOBA_BLOCK
    row = lambda i: (i, 0)
    const = lambda i: (0, 0)
    pos = lambda i: (i % n_pos, 0)
    out_shape = [jax.ShapeDtypeStruct((t, w), BF16),
                 jax.ShapeDtypeStruct((MOBA_HEADS, t // tm, MOBA_HEAD_DIM, tm), BF16),
                 jax.ShapeDtypeStruct((MOBA_HEADS, t // tm, QK_WIDTH, tm), BF16),
                 jax.ShapeDtypeStruct((t, hw), BF16),
                 jax.ShapeDtypeStruct((t // MOBA_BLOCK, 1, w), F32),
                 jax.ShapeDtypeStruct((t, MLA_Q_RANK), F32), jax.ShapeDtypeStruct((t, MLA_KV_RANK), F32),
                 jax.ShapeDtypeStruct((t, QK_WIDTH), BF16)]
    out_specs = [pl.BlockSpec((tm, w), row),
                 pl.BlockSpec((MOBA_HEADS, 1, MOBA_HEAD_DIM, tm), lambda i: (0, i, 0, 0)),
                 pl.BlockSpec((MOBA_HEADS, 1, QK_WIDTH, tm), lambda i: (0, i, 0, 0)),
                 pl.BlockSpec((tm, hw), row),
                 pl.BlockSpec((nb, 1, w), lambda i: (i, 0, 0)),
                 pl.BlockSpec((tm, MLA_Q_RANK), row), pl.BlockSpec((tm, MLA_KV_RANK), row),
                 pl.BlockSpec((tm, QK_WIDTH), row)]
    vmem = (2 * tm * d * 4 + d * nc * 2 + 2 * tm * (w * 2 + 2 * hw * 2 + 2 * 512 * 4 + 256 * 2)
            + 4 * tm * QK_WIDTH * 4 + tm * d * 2 + 3 * tm * w * 4 + (4 << 20))
    return pl.pallas_call(
        functools.partial(_inproj_kernel, q_scale=MOBA_HEAD_DIM ** -0.5 * LOG2E),
        grid=(t // tm,),
        in_specs=[pl.BlockSpec((tm, d), row), pl.BlockSpec((1, d), const),
                  pl.BlockSpec((d, nc), const, pipeline_mode=pl.Buffered(1)),
                  pl.BlockSpec((tm, QK_WIDTH), pos), pl.BlockSpec((tm, QK_WIDTH), pos),
                  pl.BlockSpec((1, QK_WIDTH - MOBA_HEAD_DIM, tm), lambda i: (i % n_pos, 0, 0))],
        out_specs=out_specs, out_shape=out_shape,
        compiler_params=_params(("parallel",), vmem), name="in_proj",
    )(x2, g, w_all, cos_t, sin_t, feat)


def _mlaup_kernel(cq_ref, ckv_ref, gq_ref, gkv_ref, wqa_ref, wqb_ref, wk_ref, wv_ref,
                  kr_ref, cos_ref, sin_ref, q_ref, kt_ref, v_ref, *, q_scale):
    nq = _rms(cq_ref[...], gq_ref[...]).astype(BF16)
    nkv = _rms(ckv_ref[...], gkv_ref[...]).astype(BF16)
    cos = cos_ref[...]
    sin = sin_ref[...]
    kr = kr_ref[...].astype(F32)
    ones = jnp.ones((nq.shape[0], QK_WIDTH - MLA_V_DIM), BF16)
    for hh in range(MLA_HEADS):
        cols = slice(hh * QK_WIDTH, (hh + 1) * QK_WIDTH)
        q = _dot(nq, wqa_ref[:, cols]) * cos + _dot(nq, wqb_ref[:, cols]) * sin
        q_ref[:, cols] = (q * q_scale).astype(BF16)
        kt_ref[hh, 0] = (_dot(nkv, wk_ref[:, cols]) + kr).T.astype(BF16)
        v_ref[:, hh * QK_WIDTH:hh * QK_WIDTH + MLA_V_DIM] = _dot(
            nkv, wv_ref[:, hh * MLA_V_DIM:(hh + 1) * MLA_V_DIM]).astype(BF16)
        v_ref[:, hh * QK_WIDTH + MLA_V_DIM:(hh + 1) * QK_WIDTH] = ones


def _mla_up(cq, ckv, gq, gkv, wqa, wqb, wk, wv, kr, cos_t, sin_t, seq, tm):
    t = cq.shape[0]
    n_pos = seq // tm
    row = lambda i: (i, 0)
    const = lambda i: (0, 0)
    pos = lambda i: (i % n_pos, 0)
    hq = MLA_HEADS * QK_WIDTH
    hv = MLA_HEADS * MLA_V_DIM
    scale = (MLA_NOPE_DIM + MLA_ROPE_DIM) ** -0.5 * LOG2E
    vmem = (4 * tm * 512 * 4 + 2 * (3 * 512 * hq * 2 + 512 * hv * 2) + 8 * tm * QK_WIDTH * 4
            + 2 * tm * 3 * hq * 2 + (8 << 20))
    return pl.pallas_call(
        functools.partial(_mlaup_kernel, q_scale=scale),
        grid=(t // tm,),
        in_specs=[pl.BlockSpec((tm, MLA_Q_RANK), row), pl.BlockSpec((tm, MLA_KV_RANK), row),
                  pl.BlockSpec((1, MLA_Q_RANK), const), pl.BlockSpec((1, MLA_KV_RANK), const),
                  pl.BlockSpec((MLA_Q_RANK, hq), const), pl.BlockSpec((MLA_Q_RANK, hq), const),
                  pl.BlockSpec((MLA_KV_RANK, hq), const), pl.BlockSpec((MLA_KV_RANK, hv), const),
                  pl.BlockSpec((tm, QK_WIDTH), row),
                  pl.BlockSpec((tm, QK_WIDTH), pos), pl.BlockSpec((tm, QK_WIDTH), pos)],
        out_specs=[pl.BlockSpec((tm, hq), row),
                   pl.BlockSpec((MLA_HEADS, 1, QK_WIDTH, tm), lambda i: (0, i, 0, 0)),
                   pl.BlockSpec((tm, hq), row)],
        out_shape=[jax.ShapeDtypeStruct((t, hq), BF16),
                   jax.ShapeDtypeStruct((MLA_HEADS, t // tm, QK_WIDTH, tm), BF16),
                   jax.ShapeDtypeStruct((t, hq), BF16)],
        compiler_params=_params(("parallel",), vmem), name="mla_up",
    )(cq, ckv, gq, gkv, wqa, wqb, wk, wv, kr, cos_t, sin_t)


def _flash_stages(refs, item_tables, *, pv=None, softmax=None, scores=None, tile, dv):
    q_ref, kt_ref, v_ref, o_ref, bias_sc, l_all, m_all, s_sc, p_sc, alpha_sc = refs
    ti_ref, tj_ref = item_tables
    rows = tile // FLASH_STRIPS

    def q_rows(t, r):
        return pl.ds(pl.multiple_of(ti_ref[t] * tile + r * rows, rows), rows)

    for r in range(FLASH_STRIPS):
        sl = slice(r * rows, (r + 1) * rows)
        if pv is not None:
            qr = q_rows(pv, r)
            res = _dot(p_sc[sl], v_ref[pl.ds(pl.multiple_of(tj_ref[pv] * tile, tile), tile), :])
            a = alpha_sc[sl]
            o_ref[qr, :] = a * o_ref[qr, :] + res[:, :dv]
            l_all[qr, :] = a * l_all[qr, :] + res[:, dv:2 * dv]
        if softmax is not None:
            qr = q_rows(softmax, r)
            diag = (ti_ref[softmax] == tj_ref[softmax]).astype(I32)
            s = s_sc[sl] + bias_sc[diag, sl]
            m_old = m_all[qr, :]
            m_new = jnp.maximum(m_old, jnp.max(s, axis=-1, keepdims=True))
            p_sc[sl] = jnp.exp2(s - m_new).astype(BF16)
            alpha_sc[sl] = jnp.exp2(m_old - m_new)
            m_all[qr, :] = m_new
        if scores is not None:
            s_sc[sl] = _dot(q_ref[q_rows(scores, r), :], kt_ref[0, tj_ref[scores]])


def _flash_sweep(refs, tables, n_items, *, tile, dv):
    ti_ref, tj_ref, st_ref = tables
    q_ref, kt_ref, v_ref, o_ref, bias_sc, l_all, m_all, s_sc, p_sc, alpha_sc = refs
    g = pl.program_id(2)
    last = pl.num_programs(2) - 1
    stages = functools.partial(_flash_stages, refs, (ti_ref, tj_ref), tile=tile, dv=dv)

    @pl.when(g == 0)
    def _():
        o_ref[...] = jnp.zeros_like(o_ref)
        l_all[...] = jnp.zeros_like(l_all)
        m_all[...] = jnp.full_like(m_all, M_INIT)
        stages(scores=0)
        stages(softmax=0, scores=1)

    lo = jnp.maximum(st_ref[g], 2)
    n_here = jnp.maximum(st_ref[g + 1] - lo, 0)

    def steady(k, carry):
        for d in range(FLASH_UNROLL):
            t = lo + FLASH_UNROLL * k + d
            stages(pv=t - 2, softmax=t - 1, scores=t)
        return carry

    def leftover(t, carry):
        stages(pv=t - 2, softmax=t - 1, scores=t)
        return carry

    n_main = n_here // FLASH_UNROLL
    lax.fori_loop(0, n_main, steady, 0)
    lax.fori_loop(lo + FLASH_UNROLL * n_main, lo + n_here, leftover, 0)

    @pl.when(g == last)
    def _():
        stages(pv=n_items - 2, softmax=n_items - 1)
        stages(pv=n_items - 1)

        def finish(u, carry):
            rows = pl.ds(pl.multiple_of(u * tile, tile), tile)
            o_ref[rows, :] = o_ref[rows, :] / l_all[rows, :]
            return carry

        lax.fori_loop(0, o_ref.shape[0] // tile, finish, 0)


def _attn_items(nq):
    ti = np.array([i for j in range(nq) for i in range(j, nq)], np.int32)
    tj = np.array([j for j in range(nq) for i in range(j, nq)], np.int32)
    per = ((len(ti) - 2) // nq) // FLASH_UNROLL * FLASH_UNROLL
    st = np.array([0] + [2 + per * g for g in range(1, nq)] + [len(ti)], np.int32)
    return jnp.asarray(ti), jnp.asarray(tj), jnp.asarray(st), len(ti)


def _mla_attn_kernel(ti_ref, tj_ref, st_ref, q_ref, kt_ref, v_ref, w_ref, o_ref, wb_ref,
                     bias_sc, *bufs, tile, n_items):
    wb_ref[...] = w_ref[...].astype(BF16)

    @pl.when(pl.program_id(2) == 0)
    def _():
        row = lax.broadcasted_iota(I32, (tile, tile), 0)
        col = lax.broadcasted_iota(I32, (tile, tile), 1)
        bias_sc[0] = jnp.zeros((tile, tile), F32)
        bias_sc[1] = jnp.where(col > row, MASKED, 0.0)

    _flash_sweep((q_ref, kt_ref, v_ref, o_ref, bias_sc) + bufs, (ti_ref, tj_ref, st_ref), n_items,
                 tile=tile, dv=MLA_V_DIM)


def _attn_call(kernel, name, q_width, extra_specs, scratch, args, w_f32, batch, heads, seq, tile, dv):
    t = batch * seq
    nq = seq // tile
    assert nq >= 2
    ti, tj, st, n_items = _attn_items(nq)
    steps = batch * heads * nq
    w2d = w_f32.reshape(-1, w_f32.shape[-1])
    assert w2d.shape[0] % steps == 0
    w_rows = w2d.shape[0] // steps
    w_spec = pl.BlockSpec((w_rows, w2d.shape[1]), lambda b, h, g, *_: ((b * heads + h) * nq + g, 0))
    once = pl.Buffered(1)
    grid_spec = pltpu.PrefetchScalarGridSpec(
        num_scalar_prefetch=3, grid=(batch, heads, nq),
        in_specs=[pl.BlockSpec((seq, q_width), lambda b, h, g, *_: (b, h), pipeline_mode=once),
                  pl.BlockSpec((1, nq, QK_WIDTH, tile), lambda b, h, g, *_: (h, b, 0, 0), pipeline_mode=once),
                  pl.BlockSpec((seq, QK_WIDTH), lambda b, h, g, *_: (b, h), pipeline_mode=once)]
                 + extra_specs + [w_spec],
        out_specs=[pl.BlockSpec((seq, dv), lambda b, h, g, *_: (b, h)), w_spec],
        scratch_shapes=scratch + [pltpu.VMEM((2, tile, tile), F32), pltpu.VMEM((seq, dv), F32),
                                  pltpu.VMEM((seq, 1), F32), pltpu.VMEM((tile, tile), F32),
                                  pltpu.VMEM((tile, tile), BF16), pltpu.VMEM((tile, 1), F32)])
    vmem = (seq * (q_width + 2 * QK_WIDTH) * 2 + 2 * seq * dv * 4 + seq * dv * 4 + seq * LANES * 4
            + 8 * tile * tile * 4 + 3 * w_rows * w2d.shape[1] * 6 + seq * QK_WIDTH * 2 + (4 << 20))
    out, wb = pl.pallas_call(
        functools.partial(kernel, n_items=n_items),
        grid_spec=grid_spec,
        out_shape=[jax.ShapeDtypeStruct((t, heads * dv), F32), jax.ShapeDtypeStruct(w2d.shape, BF16)],
        compiler_params=_params(("parallel", "parallel", "arbitrary"), vmem), name=name,
    )(ti, tj, st, *args, w2d)
    return out, wb.reshape(w_f32.shape)


def _mla_attn(qc, kt, v, w_f32, batch, seq, tile):
    return _attn_call(functools.partial(_mla_attn_kernel, tile=tile), "mla_attn", QK_WIDTH, [], [],
                      (qc, kt, v), w_f32, batch, MLA_HEADS, seq, tile, MLA_V_DIM)


def _moba_attn_kernel(ti_ref, tj_ref, st_ref, q_ref, kt_ref, v_ref, qt_ref, km_ref, aq_ref, w_ref, o_ref, wb_ref,
                      qa_sc, bias_sc, *bufs, tile, n_items):
    wb_ref[...] = w_ref[...].astype(BF16)
    blk = MOBA_BLOCK

    @pl.when(pl.program_id(2) == 0)
    def _():
        row = lax.broadcasted_iota(I32, (tile, tile), 0)
        col = lax.broadcasted_iota(I32, (tile, tile), 1)
        bias_sc[0] = jnp.zeros((tile, tile), F32)
        bias_sc[1] = jnp.where((row // blk == col // blk) & (col > row), MASKED, 0.0)
        km = km_ref[0]
        km_hi = km.astype(BF16)
        km_lo = (km - km_hi.astype(F32)).astype(BF16)
        blk_id = lax.broadcasted_iota(I32, (FEAT_BLOCKS, tile), 0)
        in_tile = lax.broadcasted_iota(I32, (FEAT_BLOCKS, tile), 1) // blk
        consts = jnp.broadcast_to(aq_ref[0, FEAT_BLOCKS:, :], (LANES - FEAT_BLOCKS, tile))

        def augment(u, carry):
            rows = pl.ds(pl.multiple_of(u * tile, tile), tile)
            qt = qt_ref[0, u]
            gate = (_dot(km_hi, qt) + _dot(km_lo, qt))[:FEAT_BLOCKS]
            own = u * (tile // blk) + in_tile
            g = jnp.where(blk_id < own, gate, -jnp.inf)
            keep = blk_id == own
            for _ in range(MOBA_TOPK):
                mx = jnp.max(g, axis=0, keepdims=True)
                cand = jnp.where((g == mx) & (mx > -jnp.inf), blk_id, LANES)
                pick = blk_id == jnp.min(cand, axis=0, keepdims=True)
                keep = keep | pick
                g = jnp.where(pick, -jnp.inf, g)
            feat_t = jnp.concatenate([jnp.where(keep, 0.0, 1.0), consts], axis=0)
            qa_sc[rows, :MOBA_HEAD_DIM] = q_ref[rows, :]
            qa_sc[rows, MOBA_HEAD_DIM:] = feat_t.T.astype(BF16)
            return carry

        lax.fori_loop(0, q_ref.shape[0] // tile, augment, 0)

    _flash_sweep((qa_sc, kt_ref, v_ref, o_ref, bias_sc) + bufs, (ti_ref, tj_ref, st_ref), n_items,
                 tile=tile, dv=MOBA_HEAD_DIM)


def _moba_attn(qm, qt, kt, vm, kmr, aq, w_f32, batch, seq, tile):
    dh = MOBA_HEAD_DIM
    nq = seq // tile
    return _attn_call(functools.partial(_moba_attn_kernel, tile=tile), "moba_attn", dh,
                      [pl.BlockSpec((1, nq, dh, tile), lambda b, h, g, *_: (h, b, 0, 0),
                                    pipeline_mode=pl.Buffered(1)),
                       pl.BlockSpec((1, LANES, dh), lambda b, h, g, *_: (b * MOBA_HEADS + h, 0, 0)),
                       pl.BlockSpec((1, LANES, 1), lambda b, h, g, *_: (h, 0, 0))],
                      [pltpu.VMEM((seq, QK_WIDTH), BF16)],
                      (qm, kt, vm, qt, kmr, aq), w_f32, batch, MOBA_HEADS, seq, tile, dh)


def _outproj_kernel(om_ref, ol_ref, gm_ref, gl_ref, wo_ref, x_ref, gf_ref, wrh_ref, wrl_ref, br_ref,
                    x1_ref, hp_ref, lg_ref):
    wm = om_ref.shape[1]
    a = _rms(om_ref[...], gm_ref[...]).astype(BF16)
    b = _rms(ol_ref[...], gl_ref[...]).astype(BF16)
    x1 = x_ref[...] + _dot(a, wo_ref[:wm, :]) + _dot(b, wo_ref[wm:, :])
    x1_ref[...] = x1
    h2 = _rms(x1, gf_ref[...])
    half = h2.shape[1] // 2
    _store_row_tiles(hp_ref, _pack_halves(h2[:, :half], h2[:, half:]))
    h_hi = h2.astype(BF16)
    h_lo = (h2 - h_hi.astype(F32)).astype(BF16)
    lg_ref[...] = (_dot(h_hi, wrh_ref[...]) + _dot(h_lo, wrh_ref[...]) + _dot(h_hi, wrl_ref[...])
                   + br_ref[...])


def _out_proj(om, ol, gm, gl, wo, x2, gf, wr_hi, wr_lo, br, tm):
    t, d = x2.shape
    wm = om.shape[1]
    wl = ol.shape[1]
    row = lambda i: (i, 0)
    const = lambda i: (0, 0)
    vmem = 2 * (wm + wl) * d * 2 + 2 * tm * (wm + wl + 2 * d) * 4 + 2 * tm * d * 2 + 6 * tm * d * 4 + (4 << 20)
    return pl.pallas_call(
        _outproj_kernel,
        grid=(t // tm,),
        in_specs=[pl.BlockSpec((tm, wm), row), pl.BlockSpec((tm, wl), row),
                  pl.BlockSpec((1, wm), const), pl.BlockSpec((1, wl), const),
                  pl.BlockSpec((wm + wl, d), const), pl.BlockSpec((tm, d), row), pl.BlockSpec((1, d), const),
                  pl.BlockSpec((d, LANES), const), pl.BlockSpec((d, LANES), const), pl.BlockSpec((1, LANES), const)],
        out_specs=[pl.BlockSpec((tm, d), row), pl.BlockSpec((tm * SUBLANES, LANES), row),
                   pl.BlockSpec((tm, LANES), row)],
        out_shape=[jax.ShapeDtypeStruct((t, d), F32), jax.ShapeDtypeStruct((t * SUBLANES, LANES), U32),
                   jax.ShapeDtypeStruct((t, LANES), F32)],
        compiler_params=_params(("parallel",), vmem), name="out_proj",
    )(om, ol, gm, gl, wo, x2, gf, wr_hi, wr_lo, br)


def _router_kernel(lg_ref, idx_ref, gate_ref, pos_ref, cnt_ref, *, tm):
    i = pl.program_id(0)

    @pl.when(i == 0)
    def _():
        cnt_ref[...] = jnp.zeros_like(cnt_ref)

    lane = lax.broadcasted_iota(I32, (tm, LANES), 1)
    work = jnp.where(lane < N_EXPERTS, lg_ref[...], -jnp.inf)
    vals, firsts, picks = [], [], []
    for _ in range(TOP_K):
        mx = jnp.max(work, axis=-1, keepdims=True)
        first = jnp.min(jnp.where(work == mx, lane, LANES), axis=-1, keepdims=True)
        pick = lane == first
        work = jnp.where(pick, -jnp.inf, work)
        vals.append(mx)
        firsts.append(first)
        picks.append(pick)
    es = [jnp.exp(v - vals[0]) for v in vals]
    denom = es[0] + es[1] + es[2] + es[3]
    chosen = jnp.zeros((tm, LANES), F32)
    for p in picks:
        chosen = jnp.where(p, 1.0, chosen)
    r = lax.broadcasted_iota(I32, (tm, tm), 0)
    cc = lax.broadcasted_iota(I32, (tm, tm), 1)
    before = (r > cc).astype(BF16)
    prior = _dot(before, chosen.astype(BF16)) + cnt_ref[...]
    lane4 = lax.broadcasted_iota(I32, (tm, TOP_K), 1)
    idx = jnp.zeros((tm, TOP_K), I32)
    gate = jnp.zeros((tm, TOP_K), F32)
    pos = jnp.zeros((tm, TOP_K), I32)
    for j in range(TOP_K):
        pj = jnp.sum(jnp.where(picks[j], prior, 0.0), axis=-1, keepdims=True)
        idx = jnp.where(lane4 == j, firsts[j], idx)
        gate = jnp.where(lane4 == j, es[j] / denom, gate)
        pos = jnp.where(lane4 == j, pj.astype(I32), pos)
    idx_ref[...] = idx
    gate_ref[...] = gate
    pos_ref[...] = pos
    cnt_ref[...] += jnp.sum(chosen, axis=0, keepdims=True)


def _router(logits, tm):
    t = logits.shape[0]
    row = lambda i: (i, 0)
    vmem = 4 * tm * tm * 4 + 24 * tm * LANES * 4 + (4 << 20)
    return pl.pallas_call(
        functools.partial(_router_kernel, tm=tm),
        grid=(t // tm,),
        in_specs=[pl.BlockSpec((tm, LANES), row)],
        out_specs=[pl.BlockSpec((tm, TOP_K), row)] * 3 + [pl.BlockSpec((1, LANES), lambda i: (0, 0))],
        out_shape=[jax.ShapeDtypeStruct((t, TOP_K), I32), jax.ShapeDtypeStruct((t, TOP_K), F32),
                   jax.ShapeDtypeStruct((t, TOP_K), I32), jax.ShapeDtypeStruct((1, LANES), F32)],
        compiler_params=_params(("arbitrary",), vmem), name="router",
    )(logits)


def _moe_up_kernel(be_ref, nu_ref, src_ref, hp_ref, w1_ref, b1_ref, act_ref, xs_sc, sem, *, fc):
    i = pl.program_id(0)
    blk = act_ref.shape[0]
    slot = i % 2

    def row_copy(block, r, s):
        return pltpu.make_async_copy(_row_tile(hp_ref, src_ref[block * blk + r]),
                                     _row_tile(xs_sc.at[s], r), sem.at[s])

    def start_block(block, s):
        def body(r, carry):
            row_copy(block, r, s).start()
            return carry
        lax.fori_loop(0, blk, body, 0, unroll=8)

    def wait_block(block, s):
        def body(r, carry):
            row_copy(block, r, s).wait()
            return carry
        lax.fori_loop(0, blk, body, 0, unroll=8)

    @pl.when(i == 0)
    def _():
        start_block(0, 0)

    @pl.when(i + 1 < nu_ref[0])
    def _():
        start_block(i + 1, 1 - slot)

    @pl.when(i < nu_ref[0])
    def _():
        wait_block(i, slot)
        hi, lo = _unpack_halves(_load_row_tiles(xs_sc.at[slot]))
        a = hi.astype(BF16)
        b = lo.astype(BF16)
        half = a.shape[1]
        d_ff = act_ref.shape[1]

        def pre(c0):
            return (_dot(a, w1_ref[0, :half, c0:c0 + fc]) + _dot(b, w1_ref[0, half:, c0:c0 + fc])
                    + b1_ref[0, :, c0:c0 + fc])

        for c in range(d_ff // fc):
            glu = jnp.minimum(pre(c * fc), SWIGLU_LIMIT)
            lin = jnp.clip(pre(d_ff + c * fc), -SWIGLU_LIMIT, SWIGLU_LIMIT)
            act = glu * jax.nn.sigmoid(SWIGLU_ALPHA * glu) * (lin + 1.0)
            act_ref[:, c * fc:(c + 1) * fc] = act.astype(BF16)

    @pl.when(i >= nu_ref[0])
    def _():
        act_ref[...] = jnp.zeros_like(act_ref)


def _moe_up(block_expert, n_used, src, hp, w1, b1):
    n_rows = src.shape[0]
    half = SUBLANES * hp.shape[1]
    e, d, f2 = w1.shape
    assert d == 2 * half
    d_ff = f2 // 2
    blk = EXPERT_BLOCK
    n_blocks = n_rows // blk
    fc = 512
    live = lambda i, be, nu: jnp.minimum(i, nu[0] - 1)
    grid_spec = pltpu.PrefetchScalarGridSpec(
        num_scalar_prefetch=3, grid=(n_blocks,),
        in_specs=[pl.BlockSpec(memory_space=pl.ANY),
                  pl.BlockSpec((1, d, f2), lambda i, be, nu, src: (be[live(i, be, nu)], 0, 0)),
                  pl.BlockSpec((1, 1, f2), lambda i, be, nu, src: (be[live(i, be, nu)], 0, 0))],
        out_specs=pl.BlockSpec((blk, d_ff), lambda i, be, nu, src: (i, 0)),
        scratch_shapes=[pltpu.VMEM((2, blk * SUBLANES, LANES), U32), pltpu.SemaphoreType.DMA((2,))])
    vmem = 2 * d * f2 * 2 + 2 * blk * half * 4 + 2 * blk * d_ff * 2 + 8 * blk * fc * 4 + 2 * blk * d * 2 + (4 << 20)
    return pl.pallas_call(
        functools.partial(_moe_up_kernel, fc=fc),
        grid_spec=grid_spec,
        out_shape=jax.ShapeDtypeStruct((n_rows, d_ff), BF16),
        compiler_params=_params(("arbitrary",), vmem), name="moe_up",
    )(block_expert, n_used, src, hp, w1, b1)


def _moe_down_kernel(be_ref, nu_ref, a_ref, w2_ref, b2_ref, y_ref):
    i = pl.program_id(0)

    @pl.when(i < nu_ref[0])
    def _():
        y = _dot(a_ref[...], w2_ref[0]) + b2_ref[0]
        half = y.shape[1] // 2
        _store_row_tiles(y_ref, _pack_halves(y[:, :half], y[:, half:]))

    @pl.when(i >= nu_ref[0])
    def _():
        y_ref[...] = jnp.zeros_like(y_ref)


def _moe_down(block_expert, n_used, act, w2, b2):
    n_rows, d_ff = act.shape
    e, _, d = w2.shape
    blk = EXPERT_BLOCK
    n_blocks = n_rows // blk
    live = lambda i, be, nu: jnp.minimum(i, nu[0] - 1)
    grid_spec = pltpu.PrefetchScalarGridSpec(
        num_scalar_prefetch=2, grid=(n_blocks,),
        in_specs=[pl.BlockSpec((blk, d_ff), lambda i, be, nu: (live(i, be, nu), 0)),
                  pl.BlockSpec((1, d_ff, d), lambda i, be, nu: (be[live(i, be, nu)], 0, 0)),
                  pl.BlockSpec((1, 1, d), lambda i, be, nu: (be[live(i, be, nu)], 0, 0))],
        out_specs=pl.BlockSpec((blk * SUBLANES, LANES), lambda i, be, nu: (i, 0)))
    vmem = 2 * d_ff * d * 2 + 2 * blk * d_ff * 2 + 2 * blk * d * 2 + 4 * blk * d * 4 + (4 << 20)
    return pl.pallas_call(
        _moe_down_kernel,
        grid_spec=grid_spec,
        out_shape=jax.ShapeDtypeStruct((n_rows * SUBLANES, LANES), U32),
        compiler_params=_params(("arbitrary",), vmem), name="moe_down",
    )(block_expert, n_used, act, w2, b2)


def _combine_kernel(dest_ref, gate_ref, x1_ref, gfin_ref, yb_ref, o_ref, rows_sc, sem, *, tm):
    i = pl.program_id(0)
    slot = i % 2

    def row_copy(tile, r, j, s):
        return pltpu.make_async_copy(_row_tile(yb_ref, dest_ref[(tile * tm + r) * TOP_K + j]),
                                     _row_tile(rows_sc.at[s, j], r), sem.at[s])

    def start_tile(tile, s):
        def body(r, carry):
            for j in range(TOP_K):
                row_copy(tile, r, j, s).start()
            return carry
        lax.fori_loop(0, tm, body, 0, unroll=4)

    def wait_tile(tile, s):
        def body(r, carry):
            for j in range(TOP_K):
                row_copy(tile, r, j, s).wait()
            return carry
        lax.fori_loop(0, tm, body, 0, unroll=4)

    @pl.when(i == 0)
    def _():
        start_tile(0, 0)

    @pl.when(i + 1 < pl.num_programs(0))
    def _():
        start_tile(i + 1, 1 - slot)

    wait_tile(i, slot)

    x1 = x1_ref[...]
    half = x1.shape[1] // 2
    gates = gate_ref[...]
    left = x1[:, :half]
    right = x1[:, half:]
    for j in range(TOP_K):
        hi, lo = _unpack_halves(_load_row_tiles(rows_sc.at[slot, j]))
        gj = gates[:, j:j + 1]
        left = left + gj * hi
        right = right + gj * lo
    ms = (jnp.sum(left * left, axis=-1, keepdims=True) + jnp.sum(right * right, axis=-1, keepdims=True)) / (2 * half)
    inv = lax.rsqrt(ms + NORM_EPS)
    g = gfin_ref[...]
    o_ref[:, :half] = left * inv * g[:, :half]
    o_ref[:, half:] = right * inv * g[:, half:]


def _combine(dest, gates, x1, gfin, yb, tm):
    t, d = x1.shape
    half = d // 2
    grid_spec = pltpu.PrefetchScalarGridSpec(
        num_scalar_prefetch=1, grid=(t // tm,),
        in_specs=[pl.BlockSpec((tm, TOP_K), lambda i, dst: (i, 0)),
                  pl.BlockSpec((tm, d), lambda i, dst: (i, 0)),
                  pl.BlockSpec((1, d), lambda i, dst: (0, 0)),
                  pl.BlockSpec(memory_space=pl.ANY)],
        out_specs=pl.BlockSpec((tm, d), lambda i, dst: (i, 0)),
        scratch_shapes=[pltpu.VMEM((2, TOP_K, tm * SUBLANES, LANES), U32), pltpu.SemaphoreType.DMA((2,))])
    vmem = 2 * TOP_K * tm * half * 4 + 4 * tm * d * 4 + 6 * tm * d * 4 + 2 * tm * LANES * 4 + (4 << 20)
    return pl.pallas_call(
        functools.partial(_combine_kernel, tm=tm),
        grid_spec=grid_spec,
        out_shape=jax.ShapeDtypeStruct((t, d), F32),
        compiler_params=_params(("arbitrary",), vmem), name="combine",
    )(dest, gates, x1, gfin, yb)


def _rope_tables(seq):
    inv = 1.0 / (ROPE_THETA ** (jnp.arange(0, MLA_ROPE_DIM, 2, dtype=F32) / MLA_ROPE_DIM))
    ang = jnp.arange(seq, dtype=F32)[:, None] * inv[None, :]
    ang = jnp.concatenate([ang, ang], axis=-1)
    ones = jnp.ones((seq, MLA_NOPE_DIM), F32)
    zeros = jnp.zeros((seq, QK_WIDTH - MLA_NOPE_DIM - MLA_ROPE_DIM), F32)
    cos_t = jnp.concatenate([ones, jnp.cos(ang), zeros], axis=-1)
    sin_t = jnp.concatenate([0.0 * ones, jnp.sin(ang), zeros], axis=-1)
    return cos_t, sin_t


def _bf16_pieces(x, n=3):
    out = []
    for _ in range(n):
        p = float(np.asarray(x, dtype=np.float32).astype(jnp.bfloat16).astype(np.float32))
        out.append(p)
        x = x - p
    return out


def _moba_features(seq, tile):
    pos = np.arange(seq)
    blk, off = pos // MOBA_BLOCK, pos % MOBA_BLOCK
    assert seq // MOBA_BLOCK <= FEAT_BLOCKS
    feat = np.zeros((QK_WIDTH - MOBA_HEAD_DIM, seq), np.float32)
    feat[:FEAT_BLOCKS] = np.where(blk[None, :] == np.arange(FEAT_BLOCKS)[:, None], MASKED, 0.0)
    feat[FEAT_ALIBI:FEAT_ALIBI + 3] = blk[None, :]
    feat[FEAT_ALIBI + 3:FEAT_ALIBI + 6] = off[None, :]
    feat = feat.reshape(feat.shape[0], seq // tile, tile).transpose(1, 0, 2)
    aq = np.zeros((MOBA_HEADS, 1, LANES), np.float32)
    for h in range(MOBA_HEADS):
        slope = 2.0 ** (-8.0 * (h + 1) / MOBA_HEADS)
        aq[h, 0, FEAT_ALIBI:FEAT_ALIBI + 3] = _bf16_pieces(slope * LOG2E * MOBA_BLOCK)
        aq[h, 0, FEAT_ALIBI + 3:FEAT_ALIBI + 6] = _bf16_pieces(slope * LOG2E)
    return jnp.asarray(feat, BF16), jnp.asarray(aq.transpose(0, 2, 1), F32)


def _rot_cols(w):
    half = w.shape[-1] // 2
    return jnp.concatenate([-w[..., half:], w[..., :half]], axis=-1)


def _rope_slot(w):
    k = w.shape[0]
    return jnp.concatenate([jnp.zeros((k, MLA_NOPE_DIM), w.dtype), w,
                            jnp.zeros((k, QK_WIDTH - MLA_NOPE_DIM - MLA_ROPE_DIM), w.dtype)], axis=-1)


def _layer(x2, batch, seq, norm_attn_g, w_in, moba_out_g, q_a_norm_g, kv_a_norm_g, w_uq, w_ukv, mla_out_g,
           w_o, norm_ffn_g, w_router, b_router, w1, b1, w2, b2, final_g):
    t, d = x2.shape
    wm = MOBA_HEADS * MOBA_HEAD_DIM
    tile = ATT_TILE
    assert seq % tile == 0 and tile % MOBA_BLOCK == 0
    cos_t, sin_t = _rope_tables(seq)
    feat, aq = _moba_features(seq, tile)

    c_kr = 3 * wm + MLA_Q_RANK + MLA_KV_RANK
    w_kr = w_in[:, c_kr:c_kr + MLA_ROPE_DIM]
    w_all = jnp.concatenate([w_in[:, :c_kr], _rope_slot(w_kr), _rope_slot(_rot_cols(w_kr))], axis=1).astype(BF16)
    wq = w_uq.reshape(MLA_Q_RANK, MLA_HEADS, MLA_NOPE_DIM + MLA_ROPE_DIM)
    zq = jnp.zeros((MLA_Q_RANK, MLA_HEADS, QK_WIDTH - MLA_NOPE_DIM - MLA_ROPE_DIM), F32)
    wqa = jnp.concatenate([wq, zq], axis=-1).reshape(MLA_Q_RANK, -1).astype(BF16)
    wqb = jnp.concatenate([jnp.zeros((MLA_Q_RANK, MLA_HEADS, MLA_NOPE_DIM), F32),
                           _rot_cols(wq[..., MLA_NOPE_DIM:]), zq], axis=-1).reshape(MLA_Q_RANK, -1).astype(BF16)
    wkv = w_ukv.reshape(MLA_KV_RANK, MLA_HEADS, MLA_NOPE_DIM + MLA_V_DIM)
    wk = jnp.concatenate([wkv[..., :MLA_NOPE_DIM],
                          jnp.zeros((MLA_KV_RANK, MLA_HEADS, QK_WIDTH - MLA_NOPE_DIM), F32)],
                         axis=-1).reshape(MLA_KV_RANK, -1).astype(BF16)
    wv = wkv[..., MLA_NOPE_DIM:].reshape(MLA_KV_RANK, -1).astype(BF16)
    wr = jnp.pad(w_router, ((0, 0), (0, LANES - N_EXPERTS)))
    wr_hi = wr.astype(BF16)
    wr_lo = (wr - wr_hi.astype(F32)).astype(BF16)
    br = jnp.pad(b_router, (0, LANES - N_EXPERTS))[None, :]

    qm, qtm, ktm, vm, kmean, cq, ckv, kr = _in_proj(x2, norm_attn_g[None, :], w_all, cos_t, sin_t, feat, seq, tm=tile)
    qc, ktl, vl = _mla_up(cq, ckv, q_a_norm_g[None, :], kv_a_norm_g[None, :], wqa, wqb, wk, wv, kr,
                          cos_t, sin_t, seq, tm=tile)
    n_blk = seq // MOBA_BLOCK
    kmr = kmean.reshape(batch, n_blk, MOBA_HEADS, MOBA_HEAD_DIM).transpose(0, 2, 1, 3)
    kmr = jnp.pad(kmr, ((0, 0), (0, 0), (0, LANES - n_blk), (0, 0))).reshape(batch * MOBA_HEADS, LANES, MOBA_HEAD_DIM)
    o_moba, w2b = _moba_attn(qm, qtm, ktm, vm, kmr, aq, w2, batch, seq, tile)
    o_mla, w1b = _mla_attn(qc, ktl, vl, w1, batch, seq, tile)
    x1, hp, logits = _out_proj(o_moba, o_mla, moba_out_g[None, :], mla_out_g[None, :], w_o.astype(BF16), x2,
                               norm_ffn_g[None, :], wr_hi, wr_lo, br, tm=256)

    idx, gates, pos, counts = _router(logits, tm=min(512, t))
    blk = EXPERT_BLOCK
    n_blocks = -(-(t * TOP_K) // blk) + N_EXPERTS
    counts = counts[0, :N_EXPERTS].astype(I32)
    padded = ((counts + blk - 1) // blk) * blk
    pad_end = jnp.cumsum(padded)
    pad_start = pad_end - padded
    dest = (pad_start[idx] + pos).reshape(-1)
    block_expert = jnp.minimum(
        jnp.sum((jnp.arange(n_blocks, dtype=I32) * blk)[:, None] >= pad_end[None, :], axis=1), N_EXPERTS - 1).astype(I32)
    n_used = (pad_end[-1:] // blk).astype(I32)
    src = jnp.zeros((n_blocks * blk,), I32).at[dest].set(jnp.arange(t * TOP_K, dtype=I32) // TOP_K,
                                                          unique_indices=True)

    act = _moe_up(block_expert, n_used, src, hp, w1b, b1[:, None, :])
    yb = _moe_down(block_expert, n_used, act, w2b, b2[:, None, :])
    return _combine(dest, gates, x1, final_g[None, :], yb, tm=256)


def kernel(x, norm_attn_g, w_in, moba_out_g, q_a_norm_g, kv_a_norm_g, w_uq, w_ukv, mla_out_g, w_o, norm_ffn_g,
           w_router, b_router, w1, b1, w2, b2, norm_final_g):
    batch, seq, d = x.shape
    depth = w_in.shape[0]
    assert depth == 1, "the final norm is fused into the last layer's combine step"
    x2 = x.reshape(batch * seq, d)
    out = _layer(x2, batch, seq, norm_attn_g[0], w_in[0], moba_out_g[0], q_a_norm_g[0], kv_a_norm_g[0], w_uq[0],
                 w_ukv[0], mla_out_g[0], w_o[0], norm_ffn_g[0], w_router[0], b_router[0], w1[0], b1[0], w2[0],
                 b2[0], norm_final_g)
    return out.reshape(batch, seq, d)
```

```python
import functools

import numpy as np
import jax
import jax.numpy as jnp
from jax import lax
from jax.experimental import pallas as pl
from jax.experimental.pallas import tpu as pltpu

MOBA_HEADS = 8
MOBA_HEAD_DIM = 128
MOBA_BLOCK = 256
MOBA_TOPK = 3
MLA_HEADS = 8
MLA_NOPE_DIM = 128
MLA_ROPE_DIM = 64
MLA_V_DIM = 128
MLA_Q_RANK = 512
MLA_KV_RANK = 512
ROPE_THETA = 10000.0
N_EXPERTS = 32
TOP_K = 4
SWIGLU_LIMIT = 7.0
SWIGLU_ALPHA = 1.702
EXPERT_BLOCK = 256
NORM_EPS = 1e-5

LANES = 128
SUBLANES = 8
QK_WIDTH = 256
ATT_TILE = 512
FLASH_STRIPS = 2
FLASH_UNROLL = 4
LOG2E = 1.4426950408889634
MASKED = -2.0 ** 100
M_INIT = -2.0 ** 98
FEAT_BLOCKS = 64
FEAT_ALIBI = FEAT_BLOCKS
V7X_VMEM_BUDGET = 56 * 1024 * 1024

F32 = jnp.float32
BF16 = jnp.bfloat16
U32 = jnp.uint32
I32 = jnp.int32


def _params(semantics, vmem_bytes):
    return pltpu.CompilerParams(dimension_semantics=semantics,
                                vmem_limit_bytes=min(int(vmem_bytes), V7X_VMEM_BUDGET))


def _rms(xf, g):
    ms = jnp.mean(xf * xf, axis=-1, keepdims=True)
    return xf * lax.rsqrt(ms + NORM_EPS) * g


def _dot(a, b):
    return jnp.dot(a, b, preferred_element_type=F32)


def _pack_halves(a, b):
    ai = lax.bitcast_convert_type(a.astype(BF16).astype(F32), U32)
    bi = lax.bitcast_convert_type(b.astype(BF16).astype(F32), U32)
    return ai | (bi >> 16)


def _unpack_halves(w):
    hi = lax.bitcast_convert_type(w & jnp.uint32(0xFFFF0000), F32)
    lo = lax.bitcast_convert_type(w << 16, F32)
    return hi, lo


def _store_row_tiles(ref, val):
    rows = val.shape[0]
    for s in range(SUBLANES):
        ref[pl.ds(s, rows, stride=SUBLANES), :] = val[:, s * LANES:(s + 1) * LANES]


def _load_row_tiles(ref):
    rows = ref.shape[0] // SUBLANES
    return jnp.concatenate([ref[pl.ds(s, rows, stride=SUBLANES), :] for s in range(SUBLANES)], axis=1)


def _row_tile(ref, r):
    return ref.at[pl.ds(pl.multiple_of(r * SUBLANES, SUBLANES), SUBLANES)]


def _inproj_kernel(x_ref, g_ref, w_ref, cos_ref, sin_ref, feat_ref,
                   qm_ref, qt_ref, kt_ref, vm_ref, kmean_ref, cq_ref, ckv_ref, kr_ref, *, q_scale):
    h = _rms(x_ref[...], g_ref[...]).astype(BF16)
    w = MOBA_HEADS * MOBA_HEAD_DIM
    dh = MOBA_HEAD_DIM
    tm = h.shape[0]

    def mm(lo, hi):
        return _dot(h, w_ref[:, lo:hi])

    q = mm(0, w) * q_scale
    qm_ref[...] = q.astype(BF16)
    k = mm(w, 2 * w)
    for b in range(tm // MOBA_BLOCK):
        kmean_ref[b] = jnp.mean(k[b * MOBA_BLOCK:(b + 1) * MOBA_BLOCK, :], axis=0, keepdims=True)
    v = mm(2 * w, 3 * w).astype(BF16)
    ones = jnp.ones((tm, QK_WIDTH - dh), BF16)
    for hh in range(MOBA_HEADS):
        qt_ref[hh, 0] = q[:, hh * dh:(hh + 1) * dh].T.astype(BF16)
        kt_ref[hh, 0, :dh, :] = k[:, hh * dh:(hh + 1) * dh].T.astype(BF16)
        kt_ref[hh, 0, dh:, :] = feat_ref[0]
        vm_ref[:, hh * QK_WIDTH:hh * QK_WIDTH + dh] = v[:, hh * dh:(hh + 1) * dh]
        vm_ref[:, hh * QK_WIDTH + dh:(hh + 1) * QK_WIDTH] = ones
    c0 = 3 * w
    cq_ref[...] = mm(c0, c0 + MLA_Q_RANK)
    c1 = c0 + MLA_Q_RANK
    ckv_ref[...] = mm(c1, c1 + MLA_KV_RANK)
    c2 = c1 + MLA_KV_RANK
    kr = mm(c2, c2 + QK_WIDTH) * cos_ref[...] + mm(c2 + QK_WIDTH, c2 + 2 * QK_WIDTH) * sin_ref[...]
    kr_ref[...] = kr.astype(BF16)


def _in_proj(x2, g, w_all, cos_t, sin_t, feat, seq, tm):
    t, d = x2.shape
    w = MOBA_HEADS * MOBA_HEAD_DIM
    hw = MOBA_HEADS * QK_WIDTH
    nc = w_all.shape[1]
    n_pos = seq // tm
    nb = tm // MOBA_BLOCK
    row = lambda i: (i, 0)
    const = lambda i: (0, 0)
    pos = lambda i: (i % n_pos, 0)
    out_shape = [jax.ShapeDtypeStruct((t, w), BF16),
                 jax.ShapeDtypeStruct((MOBA_HEADS, t // tm, MOBA_HEAD_DIM, tm), BF16),
                 jax.ShapeDtypeStruct((MOBA_HEADS, t // tm, QK_WIDTH, tm), BF16),
                 jax.ShapeDtypeStruct((t, hw), BF16),
                 jax.ShapeDtypeStruct((t // MOBA_BLOCK, 1, w), F32),
                 jax.ShapeDtypeStruct((t, MLA_Q_RANK), F32), jax.ShapeDtypeStruct((t, MLA_KV_RANK), F32),
                 jax.ShapeDtypeStruct((t, QK_WIDTH), BF16)]
    out_specs = [pl.BlockSpec((tm, w), row),
                 pl.BlockSpec((MOBA_HEADS, 1, MOBA_HEAD_DIM, tm), lambda i: (0, i, 0, 0)),
                 pl.BlockSpec((MOBA_HEADS, 1, QK_WIDTH, tm), lambda i: (0, i, 0, 0)),
                 pl.BlockSpec((tm, hw), row),
                 pl.BlockSpec((nb, 1, w), lambda i: (i, 0, 0)),
                 pl.BlockSpec((tm, MLA_Q_RANK), row), pl.BlockSpec((tm, MLA_KV_RANK), row),
                 pl.BlockSpec((tm, QK_WIDTH), row)]
    vmem = (2 * tm * d * 4 + d * nc * 2 + 2 * tm * (w * 2 + 2 * hw * 2 + 2 * 512 * 4 + 256 * 2)
            + 4 * tm * QK_WIDTH * 4 + tm * d * 2 + 3 * tm * w * 4 + (4 << 20))
    return pl.pallas_call(
        functools.partial(_inproj_kernel, q_scale=MOBA_HEAD_DIM ** -0.5 * LOG2E),
        grid=(t // tm,),
        in_specs=[pl.BlockSpec((tm, d), row), pl.BlockSpec((1, d), const),
                  pl.BlockSpec((d, nc), const, pipeline_mode=pl.Buffered(1)),
                  pl.BlockSpec((tm, QK_WIDTH), pos), pl.BlockSpec((tm, QK_WIDTH), pos),
                  pl.BlockSpec((1, QK_WIDTH - MOBA_HEAD_DIM, tm), lambda i: (i % n_pos, 0, 0))],
        out_specs=out_specs, out_shape=out_shape,
        compiler_params=_params(("parallel",), vmem), name="in_proj",
    )(x2, g, w_all, cos_t, sin_t, feat)


def _mlaup_kernel(cq_ref, ckv_ref, gq_ref, gkv_ref, wqa_ref, wqb_ref, wk_ref, wv_ref,
                  kr_ref, cos_ref, sin_ref, q_ref, kt_ref, v_ref, *, q_scale):
    nq = _rms(cq_ref[...], gq_ref[...]).astype(BF16)
    nkv = _rms(ckv_ref[...], gkv_ref[...]).astype(BF16)
    cos = cos_ref[...]
    sin = sin_ref[...]
    kr = kr_ref[...].astype(F32)
    ones = jnp.ones((nq.shape[0], QK_WIDTH - MLA_V_DIM), BF16)
    for hh in range(MLA_HEADS):
        cols = slice(hh * QK_WIDTH, (hh + 1) * QK_WIDTH)
        q = _dot(nq, wqa_ref[:, cols]) * cos + _dot(nq, wqb_ref[:, cols]) * sin
        q_ref[:, cols] = (q * q_scale).astype(BF16)
        kt_ref[hh, 0] = (_dot(nkv, wk_ref[:, cols]) + kr).T.astype(BF16)
        v_ref[:, hh * QK_WIDTH:hh * QK_WIDTH + MLA_V_DIM] = _dot(
            nkv, wv_ref[:, hh * MLA_V_DIM:(hh + 1) * MLA_V_DIM]).astype(BF16)
        v_ref[:, hh * QK_WIDTH + MLA_V_DIM:(hh + 1) * QK_WIDTH] = ones


def _mla_up(cq, ckv, gq, gkv, wqa, wqb, wk, wv, kr, cos_t, sin_t, seq, tm):
    t = cq.shape[0]
    n_pos = seq // tm
    row = lambda i: (i, 0)
    const = lambda i: (0, 0)
    pos = lambda i: (i % n_pos, 0)
    hq = MLA_HEADS * QK_WIDTH
    hv = MLA_HEADS * MLA_V_DIM
    scale = (MLA_NOPE_DIM + MLA_ROPE_DIM) ** -0.5 * LOG2E
    vmem = (4 * tm * 512 * 4 + 2 * (3 * 512 * hq * 2 + 512 * hv * 2) + 8 * tm * QK_WIDTH * 4
            + 2 * tm * 3 * hq * 2 + (8 << 20))
    return pl.pallas_call(
        functools.partial(_mlaup_kernel, q_scale=scale),
        grid=(t // tm,),
        in_specs=[pl.BlockSpec((tm, MLA_Q_RANK), row), pl.BlockSpec((tm, MLA_KV_RANK), row),
                  pl.BlockSpec((1, MLA_Q_RANK), const), pl.BlockSpec((1, MLA_KV_RANK), const),
                  pl.BlockSpec((MLA_Q_RANK, hq), const), pl.BlockSpec((MLA_Q_RANK, hq), const),
                  pl.BlockSpec((MLA_KV_RANK, hq), const), pl.BlockSpec((MLA_KV_RANK, hv), const),
                  pl.BlockSpec((tm, QK_WIDTH), row),
                  pl.BlockSpec((tm, QK_WIDTH), pos), pl.BlockSpec((tm, QK_WIDTH), pos)],
        out_specs=[pl.BlockSpec((tm, hq), row),
                   pl.BlockSpec((MLA_HEADS, 1, QK_WIDTH, tm), lambda i: (0, i, 0, 0)),
                   pl.BlockSpec((tm, hq), row)],
        out_shape=[jax.ShapeDtypeStruct((t, hq), BF16),
                   jax.ShapeDtypeStruct((MLA_HEADS, t // tm, QK_WIDTH, tm), BF16),
                   jax.ShapeDtypeStruct((t, hq), BF16)],
        compiler_params=_params(("parallel",), vmem), name="mla_up",
    )(cq, ckv, gq, gkv, wqa, wqb, wk, wv, kr, cos_t, sin_t)


def _flash_stages(refs, item_tables, *, pv=None, softmax=None, scores=None, tile, dv):
    q_ref, kt_ref, v_ref, o_ref, bias_sc, l_all, m_all, s_sc, p_sc, alpha_sc = refs
    ti_ref, tj_ref = item_tables
    rows = tile // FLASH_STRIPS

    def q_rows(t, r):
        return pl.ds(pl.multiple_of(ti_ref[t] * tile + r * rows, rows), rows)

    for r in range(FLASH_STRIPS):
        sl = slice(r * rows, (r + 1) * rows)
        if pv is not None:
            qr = q_rows(pv, r)
            res = _dot(p_sc[sl], v_ref[pl.ds(pl.multiple_of(tj_ref[pv] * tile, tile), tile), :])
            a = alpha_sc[sl]
            o_ref[qr, :] = a * o_ref[qr, :] + res[:, :dv]
            l_all[qr, :] = a * l_all[qr, :] + res[:, dv:2 * dv]
        if softmax is not None:
            qr = q_rows(softmax, r)
            diag = (ti_ref[softmax] == tj_ref[softmax]).astype(I32)
            s = s_sc[sl] + bias_sc[diag, sl]
            m_old = m_all[qr, :]
            m_new = jnp.maximum(m_old, jnp.max(s, axis=-1, keepdims=True))
            p_sc[sl] = jnp.exp2(s - m_new).astype(BF16)
            alpha_sc[sl] = jnp.exp2(m_old - m_new)
            m_all[qr, :] = m_new
        if scores is not None:
            s_sc[sl] = _dot(q_ref[q_rows(scores, r), :], kt_ref[0, tj_ref[scores]])


def _flash_sweep(refs, tables, n_items, *, tile, dv):
    ti_ref, tj_ref, st_ref = tables
    q_ref, kt_ref, v_ref, o_ref, bias_sc, l_all, m_all, s_sc, p_sc, alpha_sc = refs
    g = pl.program_id(2)
    last = pl.num_programs(2) - 1
    stages = functools.partial(_flash_stages, refs, (ti_ref, tj_ref), tile=tile, dv=dv)

    @pl.when(g == 0)
    def _():
        o_ref[...] = jnp.zeros_like(o_ref)
        l_all[...] = jnp.zeros_like(l_all)
        m_all[...] = jnp.full_like(m_all, M_INIT)
        stages(scores=0)
        stages(softmax=0, scores=1)

    lo = jnp.maximum(st_ref[g], 2)
    n_here = jnp.maximum(st_ref[g + 1] - lo, 0)

    def steady(k, carry):
        for d in range(FLASH_UNROLL):
            t = lo + FLASH_UNROLL * k + d
            stages(pv=t - 2, softmax=t - 1, scores=t)
        return carry

    def leftover(t, carry):
        stages(pv=t - 2, softmax=t - 1, scores=t)
        return carry

    n_main = n_here // FLASH_UNROLL
    lax.fori_loop(0, n_main, steady, 0)
    lax.fori_loop(lo + FLASH_UNROLL * n_main, lo + n_here, leftover, 0)

    @pl.when(g == last)
    def _():
        stages(pv=n_items - 2, softmax=n_items - 1)
        stages(pv=n_items - 1)

        def finish(u, carry):
            rows = pl.ds(pl.multiple_of(u * tile, tile), tile)
            o_ref[rows, :] = o_ref[rows, :] / l_all[rows, :]
            return carry

        lax.fori_loop(0, o_ref.shape[0] // tile, finish, 0)


def _attn_items(nq):
    ti = np.array([i for j in range(nq) for i in range(j, nq)], np.int32)
    tj = np.array([j for j in range(nq) for i in range(j, nq)], np.int32)
    per = ((len(ti) - 2) // nq) // FLASH_UNROLL * FLASH_UNROLL
    st = np.array([0] + [2 + per * g for g in range(1, nq)] + [len(ti)], np.int32)
    return jnp.asarray(ti), jnp.asarray(tj), jnp.asarray(st), len(ti)


def _mla_attn_kernel(ti_ref, tj_ref, st_ref, q_ref, kt_ref, v_ref, w_ref, o_ref, wb_ref,
                     bias_sc, *bufs, tile, n_items):
    wb_ref[...] = w_ref[...].astype(BF16)

    @pl.when(pl.program_id(2) == 0)
    def _():
        row = lax.broadcasted_iota(I32, (tile, tile), 0)
        col = lax.broadcasted_iota(I32, (tile, tile), 1)
        bias_sc[0] = jnp.zeros((tile, tile), F32)
        bias_sc[1] = jnp.where(col > row, MASKED, 0.0)

    _flash_sweep((q_ref, kt_ref, v_ref, o_ref, bias_sc) + bufs, (ti_ref, tj_ref, st_ref), n_items,
                 tile=tile, dv=MLA_V_DIM)


def _attn_call(kernel, name, q_width, extra_specs, scratch, args, w_f32, batch, heads, seq, tile, dv):
    t = batch * seq
    nq = seq // tile
    assert nq >= 2
    ti, tj, st, n_items = _attn_items(nq)
    steps = batch * heads * nq
    w2d = w_f32.reshape(-1, w_f32.shape[-1])
    assert w2d.shape[0] % steps == 0
    w_rows = w2d.shape[0] // steps
    w_spec = pl.BlockSpec((w_rows, w2d.shape[1]), lambda b, h, g, *_: ((b * heads + h) * nq + g, 0))
    once = pl.Buffered(1)
    grid_spec = pltpu.PrefetchScalarGridSpec(
        num_scalar_prefetch=3, grid=(batch, heads, nq),
        in_specs=[pl.BlockSpec((seq, q_width), lambda b, h, g, *_: (b, h), pipeline_mode=once),
                  pl.BlockSpec((1, nq, QK_WIDTH, tile), lambda b, h, g, *_: (h, b, 0, 0)),
                  pl.BlockSpec((seq, QK_WIDTH), lambda b, h, g, *_: (b, h))]
                 + extra_specs + [w_spec],
        out_specs=[pl.BlockSpec((seq, dv), lambda b, h, g, *_: (b, h)), w_spec],
        scratch_shapes=scratch + [pltpu.VMEM((2, tile, tile), F32), pltpu.VMEM((seq, dv), F32),
                                  pltpu.VMEM((seq, 1), F32), pltpu.VMEM((tile, tile), F32),
                                  pltpu.VMEM((tile, tile), BF16), pltpu.VMEM((tile, 1), F32)])
    vmem = (seq * (q_width + 2 * QK_WIDTH) * 2 + 2 * seq * dv * 4 + seq * dv * 4 + seq * LANES * 4
            + 8 * tile * tile * 4 + 3 * w_rows * w2d.shape[1] * 6 + seq * QK_WIDTH * 2 + (4 << 20))
    out, wb = pl.pallas_call(
        functools.partial(kernel, n_items=n_items),
        grid_spec=grid_spec,
        out_shape=[jax.ShapeDtypeStruct((t, heads * dv), F32), jax.ShapeDtypeStruct(w2d.shape, BF16)],
        compiler_params=_params(("parallel", "parallel", "arbitrary"), vmem), name=name,
    )(ti, tj, st, *args, w2d)
    return out, wb.reshape(w_f32.shape)


def _mla_attn(qc, kt, v, w_f32, batch, seq, tile):
    return _attn_call(functools.partial(_mla_attn_kernel, tile=tile), "mla_attn", QK_WIDTH, [], [],
                      (qc, kt, v), w_f32, batch, MLA_HEADS, seq, tile, MLA_V_DIM)


def _moba_attn_kernel(ti_ref, tj_ref, st_ref, q_ref, kt_ref, v_ref, qt_ref, km_ref, aq_ref, w_ref, o_ref, wb_ref,
                      qa_sc, bias_sc, *bufs, tile, n_items):
    wb_ref[...] = w_ref[...].astype(BF16)
    blk = MOBA_BLOCK

    @pl.when(pl.program_id(2) == 0)
    def _():
        row = lax.broadcasted_iota(I32, (tile, tile), 0)
        col = lax.broadcasted_iota(I32, (tile, tile), 1)
        bias_sc[0] = jnp.zeros((tile, tile), F32)
        bias_sc[1] = jnp.where((row // blk == col // blk) & (col > row), MASKED, 0.0)
        km = km_ref[0]
        km_hi = km.astype(BF16)
        km_lo = (km - km_hi.astype(F32)).astype(BF16)
        blk_id = lax.broadcasted_iota(I32, (FEAT_BLOCKS, tile), 0)
        in_tile = lax.broadcasted_iota(I32, (FEAT_BLOCKS, tile), 1) // blk
        consts = jnp.broadcast_to(aq_ref[0, FEAT_BLOCKS:, :], (LANES - FEAT_BLOCKS, tile))

        def augment(u, carry):
            rows = pl.ds(pl.multiple_of(u * tile, tile), tile)
            qt = qt_ref[0, u]
            gate = (_dot(km_hi, qt) + _dot(km_lo, qt))[:FEAT_BLOCKS]
            own = u * (tile // blk) + in_tile
            g = jnp.where(blk_id < own, gate, -jnp.inf)
            keep = blk_id == own
            for _ in range(MOBA_TOPK):
                mx = jnp.max(g, axis=0, keepdims=True)
                cand = jnp.where((g == mx) & (mx > -jnp.inf), blk_id, LANES)
                pick = blk_id == jnp.min(cand, axis=0, keepdims=True)
                keep = keep | pick
                g = jnp.where(pick, -jnp.inf, g)
            feat_t = jnp.concatenate([jnp.where(keep, 0.0, 1.0), consts], axis=0)
            qa_sc[rows, :MOBA_HEAD_DIM] = q_ref[rows, :]
            qa_sc[rows, MOBA_HEAD_DIM:] = feat_t.T.astype(BF16)
            return carry

        lax.fori_loop(0, q_ref.shape[0] // tile, augment, 0)

    _flash_sweep((qa_sc, kt_ref, v_ref, o_ref, bias_sc) + bufs, (ti_ref, tj_ref, st_ref), n_items,
                 tile=tile, dv=MOBA_HEAD_DIM)


def _moba_attn(qm, qt, kt, vm, kmr, aq, w_f32, batch, seq, tile):
    dh = MOBA_HEAD_DIM
    nq = seq // tile
    return _attn_call(functools.partial(_moba_attn_kernel, tile=tile), "moba_attn", dh,
                      [pl.BlockSpec((1, nq, dh, tile), lambda b, h, g, *_: (h, b, 0, 0),
                                    pipeline_mode=pl.Buffered(1)),
                       pl.BlockSpec((1, LANES, dh), lambda b, h, g, *_: (b * MOBA_HEADS + h, 0, 0)),
                       pl.BlockSpec((1, LANES, 1), lambda b, h, g, *_: (h, 0, 0))],
                      [pltpu.VMEM((seq, QK_WIDTH), BF16)],
                      (qm, kt, vm, qt, kmr, aq), w_f32, batch, MOBA_HEADS, seq, tile, dh)


def _outproj_kernel(om_ref, ol_ref, gm_ref, gl_ref, wo_ref, x_ref, gf_ref, wrh_ref, wrl_ref, br_ref,
                    x1_ref, hp_ref, lg_ref):
    wm = om_ref.shape[1]
    a = _rms(om_ref[...], gm_ref[...]).astype(BF16)
    b = _rms(ol_ref[...], gl_ref[...]).astype(BF16)
    x1 = x_ref[...] + _dot(a, wo_ref[:wm, :]) + _dot(b, wo_ref[wm:, :])
    x1_ref[...] = x1
    h2 = _rms(x1, gf_ref[...])
    half = h2.shape[1] // 2
    _store_row_tiles(hp_ref, _pack_halves(h2[:, :half], h2[:, half:]))
    h_hi = h2.astype(BF16)
    h_lo = (h2 - h_hi.astype(F32)).astype(BF16)
    lg_ref[...] = (_dot(h_hi, wrh_ref[...]) + _dot(h_lo, wrh_ref[...]) + _dot(h_hi, wrl_ref[...])
                   + br_ref[...])


def _out_proj(om, ol, gm, gl, wo, x2, gf, wr_hi, wr_lo, br, tm):
    t, d = x2.shape
    wm = om.shape[1]
    wl = ol.shape[1]
    row = lambda i: (i, 0)
    const = lambda i: (0, 0)
    vmem = 2 * (wm + wl) * d * 2 + 2 * tm * (wm + wl + 2 * d) * 4 + 2 * tm * d * 2 + 6 * tm * d * 4 + (4 << 20)
    return pl.pallas_call(
        _outproj_kernel,
        grid=(t // tm,),
        in_specs=[pl.BlockSpec((tm, wm), row), pl.BlockSpec((tm, wl), row),
                  pl.BlockSpec((1, wm), const), pl.BlockSpec((1, wl), const),
                  pl.BlockSpec((wm + wl, d), const), pl.BlockSpec((tm, d), row), pl.BlockSpec((1, d), const),
                  pl.BlockSpec((d, LANES), const), pl.BlockSpec((d, LANES), const), pl.BlockSpec((1, LANES), const)],
        out_specs=[pl.BlockSpec((tm, d), row), pl.BlockSpec((tm * SUBLANES, LANES), row),
                   pl.BlockSpec((tm, LANES), row)],
        out_shape=[jax.ShapeDtypeStruct((t, d), F32), jax.ShapeDtypeStruct((t * SUBLANES, LANES), U32),
                   jax.ShapeDtypeStruct((t, LANES), F32)],
        compiler_params=_params(("parallel",), vmem), name="out_proj",
    )(om, ol, gm, gl, wo, x2, gf, wr_hi, wr_lo, br)


def _router_kernel(lg_ref, idx_ref, gate_ref, pos_ref, cnt_ref, *, tm):
    i = pl.program_id(0)

    @pl.when(i == 0)
    def _():
        cnt_ref[...] = jnp.zeros_like(cnt_ref)

    lane = lax.broadcasted_iota(I32, (tm, LANES), 1)
    work = jnp.where(lane < N_EXPERTS, lg_ref[...], -jnp.inf)
    vals, firsts, picks = [], [], []
    for _ in range(TOP_K):
        mx = jnp.max(work, axis=-1, keepdims=True)
        first = jnp.min(jnp.where(work == mx, lane, LANES), axis=-1, keepdims=True)
        pick = lane == first
        work = jnp.where(pick, -jnp.inf, work)
        vals.append(mx)
        firsts.append(first)
        picks.append(pick)
    es = [jnp.exp(v - vals[0]) for v in vals]
    denom = es[0] + es[1] + es[2] + es[3]
    chosen = jnp.zeros((tm, LANES), F32)
    for p in picks:
        chosen = jnp.where(p, 1.0, chosen)
    r = lax.broadcasted_iota(I32, (tm, tm), 0)
    cc = lax.broadcasted_iota(I32, (tm, tm), 1)
    before = (r > cc).astype(BF16)
    prior = _dot(before, chosen.astype(BF16)) + cnt_ref[...]
    lane4 = lax.broadcasted_iota(I32, (tm, TOP_K), 1)
    idx = jnp.zeros((tm, TOP_K), I32)
    gate = jnp.zeros((tm, TOP_K), F32)
    pos = jnp.zeros((tm, TOP_K), I32)
    for j in range(TOP_K):
        pj = jnp.sum(jnp.where(picks[j], prior, 0.0), axis=-1, keepdims=True)
        idx = jnp.where(lane4 == j, firsts[j], idx)
        gate = jnp.where(lane4 == j, es[j] / denom, gate)
        pos = jnp.where(lane4 == j, pj.astype(I32), pos)
    idx_ref[...] = idx
    gate_ref[...] = gate
    pos_ref[...] = pos
    cnt_ref[...] += jnp.sum(chosen, axis=0, keepdims=True)


def _router(logits, tm):
    t = logits.shape[0]
    row = lambda i: (i, 0)
    vmem = 4 * tm * tm * 4 + 24 * tm * LANES * 4 + (4 << 20)
    return pl.pallas_call(
        functools.partial(_router_kernel, tm=tm),
        grid=(t // tm,),
        in_specs=[pl.BlockSpec((tm, LANES), row)],
        out_specs=[pl.BlockSpec((tm, TOP_K), row)] * 3 + [pl.BlockSpec((1, LANES), lambda i: (0, 0))],
        out_shape=[jax.ShapeDtypeStruct((t, TOP_K), I32), jax.ShapeDtypeStruct((t, TOP_K), F32),
                   jax.ShapeDtypeStruct((t, TOP_K), I32), jax.ShapeDtypeStruct((1, LANES), F32)],
        compiler_params=_params(("arbitrary",), vmem), name="router",
    )(logits)


def _moe_up_kernel(be_ref, nu_ref, src_ref, hp_ref, w1_ref, b1_ref, act_ref, xs_sc, sem, *, fc):
    i = pl.program_id(0)
    blk = act_ref.shape[0]
    slot = i % 2

    def row_copy(block, r, s):
        return pltpu.make_async_copy(_row_tile(hp_ref, src_ref[block * blk + r]),
                                     _row_tile(xs_sc.at[s], r), sem.at[s])

    def start_block(block, s):
        def body(r, carry):
            row_copy(block, r, s).start()
            return carry
        lax.fori_loop(0, blk, body, 0, unroll=8)

    def wait_block(block, s):
        def body(r, carry):
            row_copy(block, r, s).wait()
            return carry
        lax.fori_loop(0, blk, body, 0, unroll=8)

    @pl.when(i == 0)
    def _():
        start_block(0, 0)

    @pl.when(i + 1 < nu_ref[0])
    def _():
        start_block(i + 1, 1 - slot)

    @pl.when(i < nu_ref[0])
    def _():
        wait_block(i, slot)
        hi, lo = _unpack_halves(_load_row_tiles(xs_sc.at[slot]))
        a = hi.astype(BF16)
        b = lo.astype(BF16)
        half = a.shape[1]
        d_ff = act_ref.shape[1]

        def pre(c0):
            return (_dot(a, w1_ref[0, :half, c0:c0 + fc]) + _dot(b, w1_ref[0, half:, c0:c0 + fc])
                    + b1_ref[0, :, c0:c0 + fc])

        for c in range(d_ff // fc):
            glu = jnp.minimum(pre(c * fc), SWIGLU_LIMIT)
            lin = jnp.clip(pre(d_ff + c * fc), -SWIGLU_LIMIT, SWIGLU_LIMIT)
            act = glu * jax.nn.sigmoid(SWIGLU_ALPHA * glu) * (lin + 1.0)
            act_ref[:, c * fc:(c + 1) * fc] = act.astype(BF16)

    @pl.when(i >= nu_ref[0])
    def _():
        act_ref[...] = jnp.zeros_like(act_ref)


def _moe_up(block_expert, n_used, src, hp, w1, b1):
    n_rows = src.shape[0]
    half = SUBLANES * hp.shape[1]
    e, d, f2 = w1.shape
    assert d == 2 * half
    d_ff = f2 // 2
    blk = EXPERT_BLOCK
    n_blocks = n_rows // blk
    fc = 512
    live = lambda i, be, nu: jnp.minimum(i, nu[0] - 1)
    grid_spec = pltpu.PrefetchScalarGridSpec(
        num_scalar_prefetch=3, grid=(n_blocks,),
        in_specs=[pl.BlockSpec(memory_space=pl.ANY),
                  pl.BlockSpec((1, d, f2), lambda i, be, nu, src: (be[live(i, be, nu)], 0, 0)),
                  pl.BlockSpec((1, 1, f2), lambda i, be, nu, src: (be[live(i, be, nu)], 0, 0))],
        out_specs=pl.BlockSpec((blk, d_ff), lambda i, be, nu, src: (i, 0)),
        scratch_shapes=[pltpu.VMEM((2, blk * SUBLANES, LANES), U32), pltpu.SemaphoreType.DMA((2,))])
    vmem = 2 * d * f2 * 2 + 2 * blk * half * 4 + 2 * blk * d_ff * 2 + 8 * blk * fc * 4 + 2 * blk * d * 2 + (4 << 20)
    return pl.pallas_call(
        functools.partial(_moe_up_kernel, fc=fc),
        grid_spec=grid_spec,
        out_shape=jax.ShapeDtypeStruct((n_rows, d_ff), BF16),
        compiler_params=_params(("arbitrary",), vmem), name="moe_up",
    )(block_expert, n_used, src, hp, w1, b1)


def _moe_down_kernel(be_ref, nu_ref, a_ref, w2_ref, b2_ref, y_ref):
    i = pl.program_id(0)

    @pl.when(i < nu_ref[0])
    def _():
        y = _dot(a_ref[...], w2_ref[0]) + b2_ref[0]
        half = y.shape[1] // 2
        _store_row_tiles(y_ref, _pack_halves(y[:, :half], y[:, half:]))

    @pl.when(i >= nu_ref[0])
    def _():
        y_ref[...] = jnp.zeros_like(y_ref)


def _moe_down(block_expert, n_used, act, w2, b2):
    n_rows, d_ff = act.shape
    e, _, d = w2.shape
    blk = EXPERT_BLOCK
    n_blocks = n_rows // blk
    live = lambda i, be, nu: jnp.minimum(i, nu[0] - 1)
    grid_spec = pltpu.PrefetchScalarGridSpec(
        num_scalar_prefetch=2, grid=(n_blocks,),
        in_specs=[pl.BlockSpec((blk, d_ff), lambda i, be, nu: (live(i, be, nu), 0)),
                  pl.BlockSpec((1, d_ff, d), lambda i, be, nu: (be[live(i, be, nu)], 0, 0)),
                  pl.BlockSpec((1, 1, d), lambda i, be, nu: (be[live(i, be, nu)], 0, 0))],
        out_specs=pl.BlockSpec((blk * SUBLANES, LANES), lambda i, be, nu: (i, 0)))
    vmem = 2 * d_ff * d * 2 + 2 * blk * d_ff * 2 + 2 * blk * d * 2 + 4 * blk * d * 4 + (4 << 20)
    return pl.pallas_call(
        _moe_down_kernel,
        grid_spec=grid_spec,
        out_shape=jax.ShapeDtypeStruct((n_rows * SUBLANES, LANES), U32),
        compiler_params=_params(("arbitrary",), vmem), name="moe_down",
    )(block_expert, n_used, act, w2, b2)


def _combine_kernel(dest_ref, gate_ref, x1_ref, gfin_ref, yb_ref, o_ref, rows_sc, sem, *, tm):
    i = pl.program_id(0)
    slot = i % 2

    def row_copy(tile, r, j, s):
        return pltpu.make_async_copy(_row_tile(yb_ref, dest_ref[(tile * tm + r) * TOP_K + j]),
                                     _row_tile(rows_sc.at[s, j], r), sem.at[s])

    def start_tile(tile, s):
        def body(r, carry):
            for j in range(TOP_K):
                row_copy(tile, r, j, s).start()
            return carry
        lax.fori_loop(0, tm, body, 0, unroll=4)

    def wait_tile(tile, s):
        def body(r, carry):
            for j in range(TOP_K):
                row_copy(tile, r, j, s).wait()
            return carry
        lax.fori_loop(0, tm, body, 0, unroll=4)

    @pl.when(i == 0)
    def _():
        start_tile(0, 0)

    @pl.when(i + 1 < pl.num_programs(0))
    def _():
        start_tile(i + 1, 1 - slot)

    wait_tile(i, slot)

    x1 = x1_ref[...]
    half = x1.shape[1] // 2
    gates = gate_ref[...]
    left = x1[:, :half]
    right = x1[:, half:]
    for j in range(TOP_K):
        hi, lo = _unpack_halves(_load_row_tiles(rows_sc.at[slot, j]))
        gj = gates[:, j:j + 1]
        left = left + gj * hi
        right = right + gj * lo
    ms = (jnp.sum(left * left, axis=-1, keepdims=True) + jnp.sum(right * right, axis=-1, keepdims=True)) / (2 * half)
    inv = lax.rsqrt(ms + NORM_EPS)
    g = gfin_ref[...]
    o_ref[:, :half] = left * inv * g[:, :half]
    o_ref[:, half:] = right * inv * g[:, half:]


def _combine(dest, gates, x1, gfin, yb, tm):
    t, d = x1.shape
    half = d // 2
    grid_spec = pltpu.PrefetchScalarGridSpec(
        num_scalar_prefetch=1, grid=(t // tm,),
        in_specs=[pl.BlockSpec((tm, TOP_K), lambda i, dst: (i, 0)),
                  pl.BlockSpec((tm, d), lambda i, dst: (i, 0)),
                  pl.BlockSpec((1, d), lambda i, dst: (0, 0)),
                  pl.BlockSpec(memory_space=pl.ANY)],
        out_specs=pl.BlockSpec((tm, d), lambda i, dst: (i, 0)),
        scratch_shapes=[pltpu.VMEM((2, TOP_K, tm * SUBLANES, LANES), U32), pltpu.SemaphoreType.DMA((2,))])
    vmem = 2 * TOP_K * tm * half * 4 + 4 * tm * d * 4 + 6 * tm * d * 4 + 2 * tm * LANES * 4 + (4 << 20)
    return pl.pallas_call(
        functools.partial(_combine_kernel, tm=tm),
        grid_spec=grid_spec,
        out_shape=jax.ShapeDtypeStruct((t, d), F32),
        compiler_params=_params(("arbitrary",), vmem), name="combine",
    )(dest, gates, x1, gfin, yb)


def _rope_tables(seq):
    inv = 1.0 / (ROPE_THETA ** (jnp.arange(0, MLA_ROPE_DIM, 2, dtype=F32) / MLA_ROPE_DIM))
    ang = jnp.arange(seq, dtype=F32)[:, None] * inv[None, :]
    ang = jnp.concatenate([ang, ang], axis=-1)
    ones = jnp.ones((seq, MLA_NOPE_DIM), F32)
    zeros = jnp.zeros((seq, QK_WIDTH - MLA_NOPE_DIM - MLA_ROPE_DIM), F32)
    cos_t = jnp.concatenate([ones, jnp.cos(ang), zeros], axis=-1)
    sin_t = jnp.concatenate([0.0 * ones, jnp.sin(ang), zeros], axis=-1)
    return cos_t, sin_t


def _bf16_pieces(x, n=3):
    out = []
    for _ in range(n):
        p = float(np.asarray(x, dtype=np.float32).astype(jnp.bfloat16).astype(np.float32))
        out.append(p)
        x = x - p
    return out


def _moba_features(seq, tile):
    pos = np.arange(seq)
    blk, off = pos // MOBA_BLOCK, pos % MOBA_BLOCK
    assert seq // MOBA_BLOCK <= FEAT_BLOCKS
    feat = np.zeros((QK_WIDTH - MOBA_HEAD_DIM, seq), np.float32)
    feat[:FEAT_BLOCKS] = np.where(blk[None, :] == np.arange(FEAT_BLOCKS)[:, None], MASKED, 0.0)
    feat[FEAT_ALIBI:FEAT_ALIBI + 3] = blk[None, :]
    feat[FEAT_ALIBI + 3:FEAT_ALIBI + 6] = off[None, :]
    feat = feat.reshape(feat.shape[0], seq // tile, tile).transpose(1, 0, 2)
    aq = np.zeros((MOBA_HEADS, 1, LANES), np.float32)
    for h in range(MOBA_HEADS):
        slope = 2.0 ** (-8.0 * (h + 1) / MOBA_HEADS)
        aq[h, 0, FEAT_ALIBI:FEAT_ALIBI + 3] = _bf16_pieces(slope * LOG2E * MOBA_BLOCK)
        aq[h, 0, FEAT_ALIBI + 3:FEAT_ALIBI + 6] = _bf16_pieces(slope * LOG2E)
    return jnp.asarray(feat, BF16), jnp.asarray(aq.transpose(0, 2, 1), F32)


def _rot_cols(w):
    half = w.shape[-1] // 2
    return jnp.concatenate([-w[..., half:], w[..., :half]], axis=-1)


def _rope_slot(w):
    k = w.shape[0]
    return jnp.concatenate([jnp.zeros((k, MLA_NOPE_DIM), w.dtype), w,
                            jnp.zeros((k, QK_WIDTH - MLA_NOPE_DIM - MLA_ROPE_DIM), w.dtype)], axis=-1)


def _layer(x2, batch, seq, norm_attn_g, w_in, moba_out_g, q_a_norm_g, kv_a_norm_g, w_uq, w_ukv, mla_out_g,
           w_o, norm_ffn_g, w_router, b_router, w1, b1, w2, b2, final_g):
    t, d = x2.shape
    wm = MOBA_HEADS * MOBA_HEAD_DIM
    tile = ATT_TILE
    assert seq % tile == 0 and tile % MOBA_BLOCK == 0
    cos_t, sin_t = _rope_tables(seq)
    feat, aq = _moba_features(seq, tile)

    c_kr = 3 * wm + MLA_Q_RANK + MLA_KV_RANK
    w_kr = w_in[:, c_kr:c_kr + MLA_ROPE_DIM]
    w_all = jnp.concatenate([w_in[:, :c_kr], _rope_slot(w_kr), _rope_slot(_rot_cols(w_kr))], axis=1).astype(BF16)
    wq = w_uq.reshape(MLA_Q_RANK, MLA_HEADS, MLA_NOPE_DIM + MLA_ROPE_DIM)
    zq = jnp.zeros((MLA_Q_RANK, MLA_HEADS, QK_WIDTH - MLA_NOPE_DIM - MLA_ROPE_DIM), F32)
    wqa = jnp.concatenate([wq, zq], axis=-1).reshape(MLA_Q_RANK, -1).astype(BF16)
    wqb = jnp.concatenate([jnp.zeros((MLA_Q_RANK, MLA_HEADS, MLA_NOPE_DIM), F32),
                           _rot_cols(wq[..., MLA_NOPE_DIM:]), zq], axis=-1).reshape(MLA_Q_RANK, -1).astype(BF16)
    wkv = w_ukv.reshape(MLA_KV_RANK, MLA_HEADS, MLA_NOPE_DIM + MLA_V_DIM)
    wk = jnp.concatenate([wkv[..., :MLA_NOPE_DIM],
                          jnp.zeros((MLA_KV_RANK, MLA_HEADS, QK_WIDTH - MLA_NOPE_DIM), F32)],
                         axis=-1).reshape(MLA_KV_RANK, -1).astype(BF16)
    wv = wkv[..., MLA_NOPE_DIM:].reshape(MLA_KV_RANK, -1).astype(BF16)
    wr = jnp.pad(w_router, ((0, 0), (0, LANES - N_EXPERTS)))
    wr_hi = wr.astype(BF16)
    wr_lo = (wr - wr_hi.astype(F32)).astype(BF16)
    br = jnp.pad(b_router, (0, LANES - N_EXPERTS))[None, :]

    qm, qtm, ktm, vm, kmean, cq, ckv, kr = _in_proj(x2, norm_attn_g[None, :], w_all, cos_t, sin_t, feat, seq, tm=tile)
    qc, ktl, vl = _mla_up(cq, ckv, q_a_norm_g[None, :], kv_a_norm_g[None, :], wqa, wqb, wk, wv, kr,
                          cos_t, sin_t, seq, tm=tile)
    n_blk = seq // MOBA_BLOCK
    kmr = kmean.reshape(batch, n_blk, MOBA_HEADS, MOBA_HEAD_DIM).transpose(0, 2, 1, 3)
    kmr = jnp.pad(kmr, ((0, 0), (0, 0), (0, LANES - n_blk), (0, 0))).reshape(batch * MOBA_HEADS, LANES, MOBA_HEAD_DIM)
    o_moba, w2b = _moba_attn(qm, qtm, ktm, vm, kmr, aq, w2, batch, seq, tile)
    o_mla, w1b = _mla_attn(qc, ktl, vl, w1, batch, seq, tile)
    x1, hp, logits = _out_proj(o_moba, o_mla, moba_out_g[None, :], mla_out_g[None, :], w_o.astype(BF16), x2,
                               norm_ffn_g[None, :], wr_hi, wr_lo, br, tm=256)

    idx, gates, pos, counts = _router(logits, tm=min(512, t))
    blk = EXPERT_BLOCK
    n_blocks = -(-(t * TOP_K) // blk) + N_EXPERTS
    counts = counts[0, :N_EXPERTS].astype(I32)
    padded = ((counts + blk - 1) // blk) * blk
    pad_end = jnp.cumsum(padded)
    pad_start = pad_end - padded
    dest = (pad_start[idx] + pos).reshape(-1)
    block_expert = jnp.minimum(
        jnp.sum((jnp.arange(n_blocks, dtype=I32) * blk)[:, None] >= pad_end[None, :], axis=1), N_EXPERTS - 1).astype(I32)
    n_used = (pad_end[-1:] // blk).astype(I32)
    src = jnp.zeros((n_blocks * blk,), I32).at[dest].set(jnp.arange(t * TOP_K, dtype=I32) // TOP_K,
                                                          unique_indices=True)

    act = _moe_up(block_expert, n_used, src, hp, w1b, b1[:, None, :])
    yb = _moe_down(block_expert, n_used, act, w2b, b2[:, None, :])
    return _combine(dest, gates, x1, final_g[None, :], yb, tm=256)


def kernel(x, norm_attn_g, w_in, moba_out_g, q_a_norm_g, kv_a_norm_g, w_uq, w_ukv, mla_out_g, w_o, norm_ffn_g,
           w_router, b_router, w1, b1, w2, b2, norm_final_g):
    batch, seq, d = x.shape
    depth = w_in.shape[0]
    assert depth == 1, "the final norm is fused into the last layer's combine step"
    x2 = x.reshape(batch * seq, d)
    out = _layer(x2, batch, seq, norm_attn_g[0], w_in[0], moba_out_g[0], q_a_norm_g[0], kv_a_norm_g[0], w_uq[0],
                 w_ukv[0], mla_out_g[0], w_o[0], norm_ffn_g[0], w_router[0], b_router[0], w1[0], b1[0], w2[0],
                 b2[0], norm_final_g)
    return out.reshape(batch, seq, d)
```
